```python
import jax
import jax.numpy as jnp
from jax import lax
import numpy as np

D_MODEL = 1024
BATCH = 2
SEQ = 8192
DEPTH = 2
DEC_BATCH = 32
DEC_SEQ = 8
PAST_LEN = 16384
PAGE_SIZE = 128

N_EVEN = (DEPTH + 1) // 2
N_ODD = DEPTH // 2
HEAD_DIM = 64
NORM_EPS = 1e-6
SC_DIM = D_MODEL // 2
SC_WIDTH = 3
RW_HEADS = (D_MODEL // 2) // HEAD_DIM
RW_DIM = RW_HEADS * HEAD_DIM
DECAY_LORA = 64
AAA_LORA = 64
GATE_LORA = 128
RW_SHIFT_W = 3 * RW_DIM + DECAY_LORA + AAA_LORA + GATE_LORA
RW_GN_EPS = 64e-5
IN0 = 3 * SC_DIM + RW_SHIFT_W
FOX_HEADS = (D_MODEL // 2) // HEAD_DIM
FOX_DIM = FOX_HEADS * HEAD_DIM
FOX_IN = 3 * FOX_DIM + FOX_HEADS
Q_BLOCK = 128
SSM_DIM = D_MODEL // 2
SSM_HEAD_DIM = 64
SSM_HEADS = SSM_DIM // SSM_HEAD_DIM
SSM_GROUPS = 2
SSM_STATE = 128
SSM_CONV = 4
SSM_CONV_DIM = SSM_DIM + 2 * SSM_GROUPS * SSM_STATE
SSD_CHUNK = 128
IN1 = FOX_IN + SSM_DIM + SSM_CONV_DIM + SSM_HEADS
FFN_HIDDEN = -(-8 * D_MODEL // (3 * 256)) * 256

kernel_name = 'hybrid_conv_rwkv7_fox_mamba2_step'

F32 = jnp.float32


def rmsnorm(x, g):
    xf = x.astype(F32)
    y = xf * lax.rsqrt(jnp.mean(xf * xf, axis=-1, keepdims=True) + NORM_EPS)
    return (y * g.astype(F32)).astype(x.dtype)


def swiglu(h, w_gate, w_up, w_down):
    return (jax.nn.silu(h @ w_gate) * (h @ w_up)) @ w_down


def causal_dwconv(u, buf, w):
    K = w.shape[0]
    L = u.shape[1]
    up = jnp.concatenate([buf.astype(u.dtype), u], axis=1)
    y = sum(w[k] * up[:, k:k + L] for k in range(K))
    return y, up[:, L:]


def short_conv_mixer(z, buf, conv_w):
    gb, gc, h = jnp.split(z, 3, axis=-1)
    y, new_buf = causal_dwconv(gc * h, buf, conv_w)
    return gb * y, new_buf


def rwkv7_mixer(z, shift_prev, wkv0, p):
    Bsz, L, _ = z.shape
    zs = jnp.concatenate([shift_prev[:, None].astype(z.dtype), z[:, :-1]], axis=1)
    zx = z + p['mu'] * (zs - z)
    idx = [RW_DIM, 2 * RW_DIM, 3 * RW_DIM, 3 * RW_DIM + DECAY_LORA, 3 * RW_DIM + DECAY_LORA + AAA_LORA]
    r, k, v, wd, ad, gd = jnp.split(zx, idx, axis=-1)
    w_log = -jax.nn.softplus(-(p['w0'] + jnp.tanh(wd) @ p['w2']).astype(F32)) - 0.5
    decay = jnp.exp(-jnp.exp(w_log))
    a = jax.nn.sigmoid((p['a0'] + ad @ p['a2']).astype(F32))
    g = jax.nn.sigmoid(gd) @ p['g2']

    def heads(t):
        return t.reshape(Bsz, L, RW_HEADS, HEAD_DIM).astype(F32)

    r, k, v, decay, a = heads(r), heads(k), heads(v), heads(decay), heads(a)
    kk = k * p['k_k'].reshape(RW_HEADS, HEAD_DIM).astype(F32)
    kk = kk / jnp.maximum(jnp.sqrt(jnp.sum(kk * kk, axis=-1, keepdims=True)), 1e-12)
    k = k * (1.0 + (a - 1.0) * p['k_a'].reshape(RW_HEADS, HEAD_DIM).astype(F32))

    def step(S, inp):
        r_t, w_t, k_t, v_t, kk_t, a_t = inp
        S = (S * w_t[:, :, None, :]
             - jnp.einsum('bhij,bhj->bhi', S, kk_t)[..., None] * (kk_t * a_t)[:, :, None, :]
             + v_t[..., None] * k_t[:, :, None, :])
        return S, jnp.einsum('bhij,bhj->bhi', S, r_t)

    xs = tuple(jnp.moveaxis(t, 1, 0) for t in (r, decay, k, v, kk, a))
    S, o = lax.scan(step, wkv0.astype(F32), xs)
    o = jnp.moveaxis(o, 0, 1)
    mean = jnp.mean(o, axis=-1, keepdims=True)
    var = jnp.mean(jnp.square(o - mean), axis=-1, keepdims=True)
    o = ((o - mean) * lax.rsqrt(var + RW_GN_EPS) * p['ln_w'].reshape(RW_HEADS, HEAD_DIM).astype(F32)
         + p['ln_b'].reshape(RW_HEADS, HEAD_DIM).astype(F32))
    o = o + jnp.sum(r * k * p['r_k'].astype(F32), axis=-1, keepdims=True) * v
    o = o.reshape(Bsz, L, RW_DIM) * g
    return o.astype(z.dtype), z[:, -1], S


def mix_even(h, sc_buf, shift_prev, wkv0, p):
    z = h @ p['w_in']
    ya, sc_new = short_conv_mixer(z[..., :3 * SC_DIM], sc_buf, p['sc_conv_w'])
    yb, shift_new, wkv_new = rwkv7_mixer(z[..., 3 * SC_DIM:], shift_prev, wkv0, p)
    out = jnp.concatenate([ya.astype(h.dtype), yb.astype(h.dtype)], axis=-1) @ p['w_out']
    return out, sc_new, shift_new, wkv_new


def fox_attention(q, c_q, q_pos, segments):
    Bsz, Lq, H, D = q.shape
    qb = Q_BLOCK if Lq % Q_BLOCK == 0 else Lq
    nb = Lq // qb
    scale = D ** -0.5
    segs = [(k, v, jnp.swapaxes(c_k, 1, 2).astype(F32), k_pos) for (k, v, c_k, k_pos) in segments]

    def block(args):
        q_blk, c_blk, p_blk = args
        cq = jnp.swapaxes(c_blk, 1, 2).astype(F32)[..., None]
        scores = []
        for k, v, ck, kp in segs:
            s = jnp.einsum('bqhd,bkhd->bhqk', q_blk, k).astype(F32) * scale + (cq - ck[:, :, None, :])
            scores.append(jnp.where(kp[None, :] <= p_blk[:, None], s, -jnp.inf))
        prob = jax.nn.softmax(jnp.concatenate(scores, axis=-1), axis=-1)
        out = 0.0
        off = 0
        for k, v, ck, kp in segs:
            n = kp.shape[0]
            out = out + jnp.einsum('bhqk,bkhd->bqhd', prob[..., off:off + n].astype(v.dtype), v)
            off += n
        return out

    blocks = (jnp.moveaxis(q.reshape(Bsz, nb, qb, H, D), 1, 0),
              jnp.moveaxis(c_q.reshape(Bsz, nb, qb, H), 1, 0),
              q_pos.reshape(nb, qb))
    o = lax.map(block, blocks)
    return jnp.moveaxis(o, 0, 1).reshape(Bsz, Lq, H * D)


def ssd_scan(x, dt, A, Bm, Cm, S0):
    Bsz, L, H, P = x.shape
    cs = SSD_CHUNK if L % SSD_CHUNK == 0 else L
    nc = L // cs
    rep = H // SSM_GROUPS
    Bh = jnp.repeat(Bm, rep, axis=2)
    Ch = jnp.repeat(Cm, rep, axis=2)

    def chunk(t):
        return t.reshape((Bsz, nc, cs) + t.shape[2:])

    xc, dtc, Bc, Cc = chunk(x), chunk(dt), chunk(Bh), chunk(Ch)
    acum = jnp.cumsum(dtc * A, axis=2)
    seg = acum[:, :, :, None, :] - acum[:, :, None, :, :]
    causal = jnp.tril(jnp.ones((cs, cs), bool))[None, None, :, :, None]
    lmat = jnp.exp(jnp.where(causal, seg, -jnp.inf))
    gmat = jnp.einsum('bcthn,bcshn->bctsh', Cc, Bc) * lmat * dtc[:, :, None, :, :]
    y_intra = jnp.einsum('bctsh,bcshp->bcthp', gmat, xc)
    decay_end = jnp.exp(acum[:, :, -1:, :] - acum)
    chunk_states = jnp.einsum('bcsh,bcshn,bcshp->bchpn', decay_end * dtc, Bc, xc)
    chunk_decay = jnp.exp(acum[:, :, -1, :])

    def step(S, inp):
        st, dec = inp
        return S * dec[:, :, None, None] + st, S

    S_final, S_in = lax.scan(step, S0, (jnp.moveaxis(chunk_states, 1, 0), jnp.moveaxis(chunk_decay, 1, 0)))
    S_in = jnp.moveaxis(S_in, 0, 1)
    y_inter = jnp.einsum('bcthn,bchpn->bcthp', Cc, S_in) * jnp.exp(acum)[..., None]
    return (y_intra + y_inter).reshape(Bsz, L, H, P), S_final


def mamba2_mixer(zm, conv_buf, ssm0, p):
    Bsz, L, _ = zm.shape
    z, xbc, dt = jnp.split(zm, [SSM_DIM, SSM_DIM + SSM_CONV_DIM], axis=-1)
    xbc, new_buf = causal_dwconv(xbc, conv_buf, p['ssm_conv_w'])
    xbc = jax.nn.silu(xbc + p['ssm_conv_b'])
    x, Bm, Cm = jnp.split(xbc, [SSM_DIM, SSM_DIM + SSM_GROUPS * SSM_STATE], axis=-1)
    x = x.reshape(Bsz, L, SSM_HEADS, SSM_HEAD_DIM).astype(F32)
    Bm = Bm.reshape(Bsz, L, SSM_GROUPS, SSM_STATE).astype(F32)
    Cm = Cm.reshape(Bsz, L, SSM_GROUPS, SSM_STATE).astype(F32)
    dt = jax.nn.softplus((dt + p['ssm_dt_bias']).astype(F32))
    A = -jnp.exp(p['ssm_a_log'].astype(F32))
    y, S = ssd_scan(x, dt, A, Bm, Cm, ssm0.astype(F32))
    y = y + p['ssm_d'].astype(F32)[:, None] * x
    y = y.reshape(Bsz, L, SSM_DIM) * jax.nn.silu(z.astype(F32))
    y = rmsnorm(y, p['ssm_norm_w'])
    return y.astype(zm.dtype), new_buf, S


def mix_odd(h, q_pos, past, conv_buf, ssm0, p):
    Bsz, L, _ = h.shape
    z = h @ p['w_in']
    zf, zm = z[..., :FOX_IN], z[..., FOX_IN:]
    q, k, v, fpre = jnp.split(zf, [FOX_DIM, 2 * FOX_DIM, 3 * FOX_DIM], axis=-1)
    q = q.reshape(Bsz, L, FOX_HEADS, HEAD_DIM)
    k = k.reshape(Bsz, L, FOX_HEADS, HEAD_DIM)
    v = v.reshape(Bsz, L, FOX_HEADS, HEAD_DIM)
    logf = jax.nn.log_sigmoid((fpre + p['fox_f_bias']).astype(F32))
    if past is None:
        c = jnp.cumsum(logf, axis=1)
        segments = [(k, v, c, q_pos)]
    else:
        k_past, v_past, logf_past = past
        c_past = jnp.cumsum(logf_past.astype(F32), axis=1)
        c = c_past[:, -1:] + jnp.cumsum(logf, axis=1)
        segments = [(k_past, v_past, c_past, jnp.arange(k_past.shape[1])), (k, v, c, q_pos)]
    yf = fox_attention(q, c, q_pos, segments)
    ym, conv_new, ssm_new = mamba2_mixer(zm, conv_buf, ssm0, p)
    out = jnp.concatenate([yf.astype(h.dtype), ym.astype(h.dtype)], axis=-1) @ p['w_out']
    return out, k, v, logf, conv_new, ssm_new


def gather_pages(pool, j, page_table):
    g = pool[j, page_table]
    return g.reshape((g.shape[0], g.shape[1] * g.shape[2]) + g.shape[3:])


def setup_inputs(seed: int = 0) -> dict:
    key = jax.random.key(seed)
    ks = iter(jax.random.split(key, 64))

    def nrm(shape, scale=1.0):
        return scale * jax.random.normal(next(ks), shape, F32)

    def uni(shape, lo, hi):
        return jax.random.uniform(next(ks), shape, F32, lo, hi)

    n_pages = PAST_LEN // PAGE_SIZE
    n_used = DEC_BATCH * n_pages
    n_pool = n_used + (n_used + 3) // 4
    page_table = jax.random.permutation(next(ks), n_pool)[:n_used].reshape(DEC_BATCH, n_pages).astype(jnp.int32)
    dt0 = jnp.exp(uni((N_ODD, SSM_HEADS), float(np.log(1e-3)), float(np.log(1e-1))))
    mix0 = SC_DIM + RW_DIM
    mix1 = FOX_DIM + SSM_DIM
    return {
        'x_prompt': nrm((BATCH, SEQ, D_MODEL)),
        'x_sample': nrm((DEC_BATCH, DEC_SEQ, D_MODEL)),
        'state_sc': nrm((N_EVEN, DEC_BATCH, SC_WIDTH - 1, SC_DIM)),
        'state_shift': nrm((N_EVEN, DEC_BATCH, RW_SHIFT_W)),
        'state_wkv': nrm((N_EVEN, DEC_BATCH, RW_HEADS, HEAD_DIM, HEAD_DIM), 0.3),
        'cache_k': nrm((N_ODD, n_pool, PAGE_SIZE, FOX_HEADS, HEAD_DIM)),
        'cache_v': nrm((N_ODD, n_pool, PAGE_SIZE, FOX_HEADS, HEAD_DIM)),
        'cache_logf': jax.nn.log_sigmoid(nrm((N_ODD, n_pool, PAGE_SIZE, FOX_HEADS)) + 2.5),
        'state_ssm_conv': nrm((N_ODD, DEC_BATCH, SSM_CONV - 1, SSM_CONV_DIM)),
        'state_ssm': nrm((N_ODD, DEC_BATCH, SSM_HEADS, SSM_HEAD_DIM, SSM_STATE), 0.3),
        'page_table': page_table,
        'norm_mix': 1.0 + nrm((DEPTH, D_MODEL), 0.05),
        'norm_ffn': 1.0 + nrm((DEPTH, D_MODEL), 0.05),
        'norm_final': 1.0 + nrm((D_MODEL,), 0.05),
        'w_in0': nrm((N_EVEN, D_MODEL, IN0), D_MODEL ** -0.5),
        'sc_conv_w': nrm((N_EVEN, SC_WIDTH, SC_DIM), SC_WIDTH ** -0.5),
        'rw_mu': uni((N_EVEN, RW_SHIFT_W), 0.0, 1.0),
        'rw_w0': uni((N_EVEN, RW_DIM), -5.0, -0.5),
        'rw_w2': nrm((N_EVEN, DECAY_LORA, RW_DIM), 0.1),
        'rw_a0': nrm((N_EVEN, RW_DIM), 0.1),
        'rw_a2': nrm((N_EVEN, AAA_LORA, RW_DIM), AAA_LORA ** -0.5),
        'rw_g2': nrm((N_EVEN, GATE_LORA, RW_DIM), GATE_LORA ** -0.5),
        'rw_k_k': 0.85 + nrm((N_EVEN, RW_DIM), 0.05),
        'rw_k_a': 1.0 + nrm((N_EVEN, RW_DIM), 0.05),
        'rw_r_k': nrm((N_EVEN, RW_HEADS, HEAD_DIM), 0.1),
        'rw_ln_w': 1.0 + nrm((N_EVEN, RW_DIM), 0.05),
        'rw_ln_b': nrm((N_EVEN, RW_DIM), 0.02),
        'w_out0': nrm((N_EVEN, mix0, D_MODEL), mix0 ** -0.5),
        'w_in1': nrm((N_ODD, D_MODEL, IN1), D_MODEL ** -0.5),
        'fox_f_bias': uni((N_ODD, FOX_HEADS), 1.0, 4.0),
        'ssm_conv_w': nrm((N_ODD, SSM_CONV, SSM_CONV_DIM), SSM_CONV ** -0.5),
        'ssm_conv_b': nrm((N_ODD, SSM_CONV_DIM), 0.02),
        'ssm_dt_bias': dt0 + jnp.log(-jnp.expm1(-dt0)),
        'ssm_a_log': jnp.log(uni((N_ODD, SSM_HEADS), 1.0, 16.0)),
        'ssm_d': 1.0 + nrm((N_ODD, SSM_HEADS), 0.05),
        'ssm_norm_w': 1.0 + nrm((N_ODD, SSM_DIM), 0.05),
        'w_out1': nrm((N_ODD, mix1, D_MODEL), mix1 ** -0.5),
        'w_gate': nrm((DEPTH, D_MODEL, FFN_HIDDEN), D_MODEL ** -0.5),
        'w_up': nrm((DEPTH, D_MODEL, FFN_HIDDEN), D_MODEL ** -0.5),
        'w_down': nrm((DEPTH, FFN_HIDDEN, D_MODEL), FFN_HIDDEN ** -0.5),
    }


def reference(x_prompt, x_sample, state_sc, state_shift, state_wkv, cache_k, cache_v, cache_logf,
              state_ssm_conv, state_ssm, page_table, norm_mix, norm_ffn, norm_final,
              w_in0, sc_conv_w, rw_mu, rw_w0, rw_w2, rw_a0, rw_a2, rw_g2, rw_k_k, rw_k_a, rw_r_k,
              rw_ln_w, rw_ln_b, w_out0, w_in1, fox_f_bias, ssm_conv_w, ssm_conv_b, ssm_dt_bias,
              ssm_a_log, ssm_d, ssm_norm_w, w_out1, w_gate, w_up, w_down):
    hp, hs = x_prompt, x_sample
    Bp, Lp = x_prompt.shape[0], x_prompt.shape[1]
    Ls = x_sample.shape[1]
    past_len = page_table.shape[1] * cache_k.shape[2]
    sc_p, sc_s, sh_p, sh_s, wkv_p, wkv_s = [], [], [], [], [], []
    k_p, k_s, v_p, v_s, lf_p, lf_s, cv_p, cv_s, ssm_p, ssm_s = [], [], [], [], [], [], [], [], [], []
    for layer in range(DEPTH):
        j = layer // 2
        np_ = rmsnorm(hp, norm_mix[layer])
        ns = rmsnorm(hs, norm_mix[layer])
        if layer % 2 == 0:
            p = {'w_in': w_in0[j], 'sc_conv_w': sc_conv_w[j], 'mu': rw_mu[j], 'w0': rw_w0[j], 'w2': rw_w2[j],
                 'a0': rw_a0[j], 'a2': rw_a2[j], 'g2': rw_g2[j], 'k_k': rw_k_k[j], 'k_a': rw_k_a[j],
                 'r_k': rw_r_k[j], 'ln_w': rw_ln_w[j], 'ln_b': rw_ln_b[j], 'w_out': w_out0[j]}
            dp, a1, a2, a3 = mix_even(np_, jnp.zeros((Bp, SC_WIDTH - 1, SC_DIM), hp.dtype),
                                      jnp.zeros((Bp, RW_SHIFT_W), hp.dtype),
                                      jnp.zeros((Bp, RW_HEADS, HEAD_DIM, HEAD_DIM), F32), p)
            ds, b1, b2, b3 = mix_even(ns, state_sc[j], state_shift[j], state_wkv[j], p)
            sc_p.append(a1); sh_p.append(a2); wkv_p.append(a3)
            sc_s.append(b1); sh_s.append(b2); wkv_s.append(b3)
        else:
            p = {'w_in': w_in1[j], 'fox_f_bias': fox_f_bias[j], 'ssm_conv_w': ssm_conv_w[j],
                 'ssm_conv_b': ssm_conv_b[j], 'ssm_dt_bias': ssm_dt_bias[j], 'ssm_a_log': ssm_a_log[j],
                 'ssm_d': ssm_d[j], 'ssm_norm_w': ssm_norm_w[j], 'w_out': w_out1[j]}
            dp, a1, a2, a3, a4, a5 = mix_odd(np_, jnp.arange(Lp), None,
                                             jnp.zeros((Bp, SSM_CONV - 1, SSM_CONV_DIM), hp.dtype),
                                             jnp.zeros((Bp, SSM_HEADS, SSM_HEAD_DIM, SSM_STATE), F32), p)
            past = (gather_pages(cache_k, j, page_table), gather_pages(cache_v, j, page_table),
                    gather_pages(cache_logf, j, page_table))
            ds, b1, b2, b3, b4, b5 = mix_odd(ns, past_len + jnp.arange(Ls), past,
                                             state_ssm_conv[j], state_ssm[j], p)
            k_p.append(a1); v_p.append(a2); lf_p.append(a3); cv_p.append(a4); ssm_p.append(a5)
            k_s.append(b1); v_s.append(b2); lf_s.append(b3); cv_s.append(b4); ssm_s.append(b5)
        hp = hp + dp
        hs = hs + ds
        hp = hp + swiglu(rmsnorm(hp, norm_ffn[layer]), w_gate[layer], w_up[layer], w_down[layer])
        hs = hs + swiglu(rmsnorm(hs, norm_ffn[layer]), w_gate[layer], w_up[layer], w_down[layer])
    y_prompt = rmsnorm(hp, norm_final)
    y_sample = rmsnorm(hs, norm_final)
    return (y_prompt, y_sample,
            jnp.stack(sc_p), jnp.stack(sc_s), jnp.stack(sh_p), jnp.stack(sh_s),
            jnp.stack(wkv_p), jnp.stack(wkv_s),
            jnp.stack(k_p), jnp.stack(k_s), jnp.stack(v_p), jnp.stack(v_s),
            jnp.stack(lf_p), jnp.stack(lf_s), jnp.stack(cv_p), jnp.stack(cv_s),
            jnp.stack(ssm_p), jnp.stack(ssm_s))
```

```python
import functools

import jax
import jax.numpy as jnp
from jax import lax
from jax.experimental import pallas as pl
from jax.experimental.pallas import tpu as pltpu

F32 = jnp.float32
BF16 = jnp.bfloat16

D_MODEL = 1024
HEAD_DIM = 64
NORM_EPS = 1e-6
SC_DIM = 512
RW_DIM = 512
RW_HEADS = 8
RW_SHIFT_W = 3 * RW_DIM + 64 + 64 + 128
RW_GN_EPS = 64e-5
FOX_DIM = 512
FOX_HEADS = 8
SSM_DIM = 512
SSM_HEADS = 8
SSM_GROUPS = 2
SSM_STATE = 128
SSM_CONV_DIM = SSM_DIM + 2 * SSM_GROUPS * SSM_STATE
FFN_HIDDEN = 2816
PAGE = 128
SSD_CHUNK = 128
RW_CHUNK = 64

LANES = 128
SUBLANES = 8
VMEM_LIMIT = 56 * 1024 * 1024

NEG_BIG = -1e30


def _cparams(sem):
    return pltpu.CompilerParams(dimension_semantics=sem, vmem_limit_bytes=VMEM_LIMIT)


_NN = (((1,), (0,)), ((), ()))
_NT = (((1,), (1,)), ((), ()))
_TN = (((0,), (0,)), ((), ()))


def _mm(a, b, dims=_NN):
    return lax.dot_general(a.astype(BF16), b.astype(BF16), dims, preferred_element_type=F32)


def _split2(x):
    hi = x.astype(BF16)
    lo = (x - hi.astype(F32)).astype(BF16)
    return hi, lo


def _split3(x):
    hi = x.astype(BF16)
    r1 = x - hi.astype(F32)
    mid = r1.astype(BF16)
    lo = (r1 - mid.astype(F32)).astype(BF16)
    return hi, mid, lo


def _mm3(a, b, dims=_NN):
    ah, al = _split2(a)
    bh, bl = _split2(b)
    d = functools.partial(lax.dot_general, dimension_numbers=dims, preferred_element_type=F32)
    return d(ah, bh) + (d(al, bh) + d(ah, bl))


def _mm_sel_r(x, e, dims=_NN):
    h, m, l = _split3(x)
    d = functools.partial(lax.dot_general, dimension_numbers=dims, preferred_element_type=F32)
    return d(h, e) + (d(m, e) + d(l, e))


def _mm_sel_l(e, x, dims=_NN):
    h, m, l = _split3(x)
    d = functools.partial(lax.dot_general, dimension_numbers=dims, preferred_element_type=F32)
    return d(e, h) + (d(e, m) + d(e, l))


def _rmsnorm(x, g):
    return x * lax.rsqrt(jnp.mean(x * x, axis=-1, keepdims=True) + NORM_EPS) * g


def _sigmoid(x):
    return 1.0 / (1.0 + jnp.exp(-x))


def _silu(x):
    return x * _sigmoid(x)


def _softplus(x):
    return jnp.maximum(x, 0.0) + jnp.log(1.0 + jnp.exp(-jnp.abs(x)))


def _iota(shape, dim):
    return lax.broadcasted_iota(jnp.int32, shape, dim)


def _shifted(u, carry8, d):
    tl = u.shape[0]
    s = pltpu.roll(u, d, 0)
    c = pltpu.roll(carry8, d, 0)
    row = _iota((SUBLANES, u.shape[1]), 0)
    head = jnp.where(row < d, c, s[:SUBLANES])
    if tl == SUBLANES:
        return head
    return jnp.concatenate([head, s[SUBLANES:]], axis=0)


def _in_proj_kernel(*refs, splits, has_t):
    x_ref, g_ref, w_ref = refs[:3]
    pos = 3
    if has_t:
        wt_ref = refs[pos]
        pos += 1
    outs = refs[pos:pos + len(splits)]
    nb = _rmsnorm(x_ref[...], g_ref[...]).astype(BF16)
    off = 0
    for o_ref, width in zip(outs, splits):
        o_ref[...] = jnp.dot(nb, w_ref[:, off:off + width], preferred_element_type=F32)
        off += width
    if has_t:
        refs[pos + len(splits)][...] = lax.dot_general(wt_ref[...], nb, _NT, preferred_element_type=F32)


def _in_proj(x, g, w, splits, wt=None):
    m = x.shape[0]
    tm = min(512, m)
    n = sum(splits)
    in_specs = [pl.BlockSpec((tm, D_MODEL), lambda i: (i, 0)),
                pl.BlockSpec((1, D_MODEL), lambda i: (0, 0)),
                pl.BlockSpec((D_MODEL, n), lambda i: (0, 0))]
    args = [x, g, w]
    out_shape = [jax.ShapeDtypeStruct((m, s), F32) for s in splits]
    out_specs = [pl.BlockSpec((tm, s), lambda i: (i, 0)) for s in splits]
    if wt is not None:
        in_specs.append(pl.BlockSpec((16, D_MODEL), lambda i: (0, 0)))
        args.append(wt)
        out_shape.append(jax.ShapeDtypeStruct((16, m), F32))
        out_specs.append(pl.BlockSpec((16, tm), lambda i: (0, i)))
    return pl.pallas_call(
        functools.partial(_in_proj_kernel, splits=tuple(splits), has_t=wt is not None),
        grid=(m // tm,), in_specs=in_specs, out_specs=out_specs, out_shape=out_shape,
        compiler_params=_cparams(("parallel",)), name="in_proj")(*args)


def _ffn_kernel(x_ref, g_ref, wg_ref, wu_ref, wd_ref, gf_ref, o_ref, xn_ref, acc_ref, *, final_norm):
    j = pl.program_id(1)

    @pl.when(j == 0)
    def _():
        x = x_ref[...]
        xn_ref[...] = _rmsnorm(x, g_ref[...]).astype(BF16)
        acc_ref[...] = x

    xn = xn_ref[...]
    gate = jnp.dot(xn, wg_ref[...], preferred_element_type=F32)
    up = jnp.dot(xn, wu_ref[...], preferred_element_type=F32)
    act = (_silu(gate) * up).astype(BF16)
    acc_ref[...] += jnp.dot(act, wd_ref[...], preferred_element_type=F32)

    @pl.when(j == pl.num_programs(1) - 1)
    def _():
        y = acc_ref[...]
        if final_norm:
            y = _rmsnorm(y, gf_ref[...])
        o_ref[...] = y


def _ffn(x, g, wg, wu, wd, gfin, final_norm):
    m = x.shape[0]
    tm = min(512, m)
    th = FFN_HIDDEN // 2
    return pl.pallas_call(
        functools.partial(_ffn_kernel, final_norm=final_norm),
        grid=(m // tm, FFN_HIDDEN // th),
        in_specs=[pl.BlockSpec((tm, D_MODEL), lambda i, j: (i, 0)),
                  pl.BlockSpec((1, D_MODEL), lambda i, j: (0, 0)),
                  pl.BlockSpec((D_MODEL, th), lambda i, j: (0, j)),
                  pl.BlockSpec((D_MODEL, th), lambda i, j: (0, j)),
                  pl.BlockSpec((th, D_MODEL), lambda i, j: (j, 0)),
                  pl.BlockSpec((1, D_MODEL), lambda i, j: (0, 0))],
        out_specs=pl.BlockSpec((tm, D_MODEL), lambda i, j: (i, 0)),
        out_shape=jax.ShapeDtypeStruct((m, D_MODEL), F32),
        scratch_shapes=[pltpu.VMEM((tm, D_MODEL), BF16), pltpu.VMEM((tm, D_MODEL), F32)],
        compiler_params=_cparams(("parallel", "arbitrary")), name="ffn")(x, g, wg, wu, wd, gfin)


def _sconv_kernel(z_ref, buf_ref, w_ref, y_ref, last_ref, carry_ref):
    @pl.when(pl.program_id(1) == 0)
    def _():
        carry_ref[...] = buf_ref[...]

    z = z_ref[...]
    tl = z.shape[0]
    gb, gc, h = z[:, :SC_DIM], z[:, SC_DIM:2 * SC_DIM], z[:, 2 * SC_DIM:]
    u = gc * h
    carry = carry_ref[...]
    w = w_ref[...]
    y = w[0:1] * _shifted(u, carry, 2) + w[1:2] * _shifted(u, carry, 1) + w[2:3] * u
    y_ref[...] = gb * y
    tail = u[tl - SUBLANES:]
    carry_ref[...] = tail
    last_ref[...] = tail


def _sconv(z, buf8, w8, tl):
    b, l, _ = z.shape
    return pl.pallas_call(
        _sconv_kernel, grid=(b, l // tl),
        in_specs=[pl.BlockSpec((None, tl, 3 * SC_DIM), lambda i, j: (i, j, 0)),
                  pl.BlockSpec((None, SUBLANES, SC_DIM), lambda i, j: (i, 0, 0)),
                  pl.BlockSpec((SUBLANES, SC_DIM), lambda i, j: (0, 0))],
        out_specs=[pl.BlockSpec((None, tl, SC_DIM), lambda i, j: (i, j, 0)),
                   pl.BlockSpec((None, SUBLANES, SC_DIM), lambda i, j: (i, 0, 0))],
        out_shape=[jax.ShapeDtypeStruct((b, l, SC_DIM), F32),
                   jax.ShapeDtypeStruct((b, SUBLANES, SC_DIM), F32)],
        scratch_shapes=[pltpu.VMEM((SUBLANES, SC_DIM), F32)],
        compiler_params=_cparams(("parallel", "arbitrary")), name="sconv")(z, buf8, w8)


def _rwkv_pre_kernel(z_ref, sh_ref, mu_ref, w0_ref, w2_ref, a0_ref, a2_ref, g2_ref, kk_ref, ka_ref, ones_ref,
                     r_o, lw_o, kp_o, v_o, kn_o, kb_o, g_o, carry_ref):
    @pl.when(pl.program_id(1) == 0)
    def _():
        carry_ref[...] = sh_ref[...]

    z = z_ref[...]
    tl = z.shape[0]
    zs = _shifted(z, carry_ref[...], 1)
    carry_ref[...] = z[tl - SUBLANES:]
    zx = z + mu_ref[...] * (zs - z)
    r = zx[:, :RW_DIM]
    k = zx[:, RW_DIM:2 * RW_DIM]
    v = zx[:, 2 * RW_DIM:3 * RW_DIM]
    wa = zx[:, 3 * RW_DIM:3 * RW_DIM + 128]
    gd = zx[:, 3 * RW_DIM + 128:]
    w_log = -_softplus(-(w0_ref[...] + _mm(jnp.tanh(wa), w2_ref[...]))) - 0.5
    lw_o[...] = -jnp.exp(w_log)
    alpha = _sigmoid(a0_ref[...] + _mm(wa, a2_ref[...]))
    g_o[...] = _mm(_sigmoid(gd), g2_ref[...])
    kk = k * kk_ref[...]
    n2 = _mm_sel_r(kk * kk, ones_ref[...])
    kk = kk / jnp.maximum(jnp.sqrt(n2), 1e-12)
    r_o[...] = r
    kp_o[...] = k * (1.0 + (alpha - 1.0) * ka_ref[...])
    v_o[...] = v
    kn_o[...] = kk
    kb_o[...] = kk * alpha


def _rwkv_pre(z, sh8, mu, w0, w2p, a0, a2p, g2, k_k, k_a, ones_bd, tl):
    b, l, _ = z.shape
    full = lambda s: pl.BlockSpec(s, lambda i, j: (0,) * len(s))
    out = jax.ShapeDtypeStruct((b, l, RW_DIM), F32)
    ospec = pl.BlockSpec((None, tl, RW_DIM), lambda i, j: (i, j, 0))
    return pl.pallas_call(
        _rwkv_pre_kernel, grid=(b, l // tl),
        in_specs=[pl.BlockSpec((None, tl, RW_SHIFT_W), lambda i, j: (i, j, 0)),
                  pl.BlockSpec((None, SUBLANES, RW_SHIFT_W), lambda i, j: (i, 0, 0)),
                  full((1, RW_SHIFT_W)), full((1, RW_DIM)), full((128, RW_DIM)), full((1, RW_DIM)),
                  full((128, RW_DIM)), full((128, RW_DIM)), full((1, RW_DIM)), full((1, RW_DIM)),
                  full((RW_DIM, RW_DIM))],
        out_specs=[ospec] * 7, out_shape=[out] * 7,
        scratch_shapes=[pltpu.VMEM((SUBLANES, RW_SHIFT_W), F32)],
        compiler_params=_cparams(("parallel", "arbitrary")), name="rwkv_pre")(
            z, sh8, mu, w0, w2p, a0, a2p, g2, k_k, k_a, ones_bd)


def _unit_lower_inverse(n, row, col, t):
    size = n.shape[0]
    eye = (row == col).astype(F32)
    lower = row > col
    same8 = (row // 8) == (col // 8)
    n8 = jnp.where(same8 & lower, n, 0.0)
    p2 = _mm3(n8, n8)
    p4 = _mm3(p2, p2)
    x = _mm3(eye + n8, eye + p2)
    x = _mm3(x, eye + p4)
    s = 16
    while s <= t:
        c = jnp.where(((row // s) == (col // s)) & ((row // (s // 2)) != (col // (s // 2))) & lower, n, 0.0)
        x = x + _mm3(_mm3(x, c), x)
        s *= 2
    del size
    return x


def _rwkv_pair_chunk(r, lw, k, v, kn, kb, h0, consts):
    t = r.shape[0]
    tri, lane_a, row2, col2, rowh, colh = consts
    cum = _mm_sel_l(tri, lw)
    cum_t = cum[t - 1:t]
    g_in = jnp.exp(cum)
    g_inv = jnp.exp(-cum)
    g_end = jnp.exp(cum_t - cum)
    a_t = -kn * jnp.exp(cum - lw)
    r_t = r * g_in
    b_t = kb * g_inv
    k_t = k * g_inv
    b_h = kb * g_end
    k_h = k * g_end

    def stack(x):
        return jnp.concatenate([jnp.where(lane_a, x, 0.0), jnp.where(lane_a, 0.0, x)], axis=0)

    la, lr, lb, lk, vbd = stack(a_t), stack(r_t), stack(b_t), stack(k_t), stack(v)
    same_head = (row2 // t) == (col2 // t)
    strict = same_head & (row2 > col2)
    incl = same_head & (row2 >= col2)
    n = jnp.where(strict, _mm3(la, lb, _NT), 0.0)
    aak = jnp.where(strict, _mm3(la, lk, _NT), 0.0)
    arb = jnp.where(incl, _mm(lr, lb, _NT), 0.0)
    ark = jnp.where(incl, _mm(lr, lk, _NT), 0.0)
    rhs = _mm3(la, h0) + _mm3(aak, vbd)
    u = _mm3(_unit_lower_inverse(n, row2, col2, t), rhs)
    o = _mm(lr, h0) + _mm(arb, u) + _mm(ark, vbd)
    o_pair = o[:t] + o[t:]
    u_pair = u[:t] + u[t:]
    upd = _mm3(jnp.concatenate([b_h, k_h], axis=0), jnp.concatenate([u_pair, v], axis=0), _TN)
    g_col = jnp.broadcast_to(jnp.exp(cum_t), (LANES, LANES)).T
    h_new = g_col * h0 + jnp.where((rowh // HEAD_DIM) == (colh // HEAD_DIM), upd, 0.0)
    return o_pair, h_new


def _rwkv_chunk_kernel(r_ref, lw_ref, k_ref, v_ref, kn_ref, kb_ref, h0_ref, tri_ref, o_ref, hout_ref, h_ref):
    @pl.when(pl.program_id(1) == 0)
    def _():
        h_ref[...] = h0_ref[...]

    bb, t, _ = r_ref.shape
    lane_a = _iota((t, LANES), 1) < HEAD_DIM
    row2 = _iota((2 * t, 2 * t), 0)
    col2 = _iota((2 * t, 2 * t), 1)
    rowh = _iota((LANES, LANES), 0)
    colh = _iota((LANES, LANES), 1)
    consts = (tri_ref[...], lane_a, row2, col2, rowh, colh)
    for b in range(bb):
        for p in range(RW_DIM // LANES):
            sl = slice(p * LANES, (p + 1) * LANES)
            o_pair, h_new = _rwkv_pair_chunk(r_ref[b, :, sl], lw_ref[b, :, sl], k_ref[b, :, sl], v_ref[b, :, sl],
                                             kn_ref[b, :, sl], kb_ref[b, :, sl], h_ref[b, p], consts)
            o_ref[b, :, sl] = o_pair
            h_ref[b, p] = h_new

    @pl.when(pl.program_id(1) == pl.num_programs(1) - 1)
    def _():
        hout_ref[...] = h_ref[...]


def _rwkv_chunk(r, lw, kp, v, kn, kb, h0, t):
    b, l, _ = r.shape
    bb = 2
    tri = (jnp.arange(t)[:, None] >= jnp.arange(t)[None, :]).astype(BF16)
    xspec = pl.BlockSpec((bb, t, RW_DIM), lambda i, j: (i, j, 0))
    hspec = pl.BlockSpec((bb, 4, LANES, LANES), lambda i, j: (i, 0, 0, 0))
    return pl.pallas_call(
        _rwkv_chunk_kernel, grid=(b // bb, l // t),
        in_specs=[xspec] * 6 + [hspec, pl.BlockSpec((t, t), lambda i, j: (0, 0))],
        out_specs=[xspec, hspec],
        out_shape=[jax.ShapeDtypeStruct((b, l, RW_DIM), F32), jax.ShapeDtypeStruct((b, 4, LANES, LANES), F32)],
        scratch_shapes=[pltpu.VMEM((bb, 4, LANES, LANES), F32)],
        compiler_params=_cparams(("parallel", "arbitrary")), name="rwkv_chunk")(r, lw, kp, v, kn, kb, h0, tri)


def _out0_kernel(x_ref, ya_ref, o_ref, r_ref, kp_ref, v_ref, g_ref, lnw_ref, lnb_ref, rk_ref, ones_ref,
                 wa_ref, wb_ref, out_ref):
    ones = ones_ref[...]
    o = o_ref[...]
    mean = _mm_sel_r(o, ones) * (1.0 / HEAD_DIM)
    d = o - mean
    var = _mm_sel_r(d * d, ones) * (1.0 / HEAD_DIM)
    on = d * lax.rsqrt(var + RW_GN_EPS) * lnw_ref[...] + lnb_ref[...]
    bonus = _mm_sel_r(r_ref[...] * kp_ref[...] * rk_ref[...], ones) * v_ref[...]
    yb = (on + bonus) * g_ref[...]
    out_ref[...] = x_ref[...] + _mm(ya_ref[...], wa_ref[...]) + _mm(yb, wb_ref[...])


def _out0(x, ya, o, r, kp, v, g, lnw, lnb, rk, ones_bd, wa, wb):
    m = x.shape[0]
    tm = min(512, m)
    row = lambda n: pl.BlockSpec((tm, n), lambda i: (i, 0))
    full = lambda s: pl.BlockSpec(s, lambda i: (0, 0))
    return pl.pallas_call(
        _out0_kernel, grid=(m // tm,),
        in_specs=[row(D_MODEL)] + [row(RW_DIM)] * 6 + [full((1, RW_DIM))] * 3 + [full((RW_DIM, RW_DIM))]
                 + [full((SC_DIM, D_MODEL)), full((RW_DIM, D_MODEL))],
        out_specs=row(D_MODEL), out_shape=jax.ShapeDtypeStruct((m, D_MODEL), F32),
        compiler_params=_cparams(("parallel",)), name="out0")(x, ya, o, r, kp, v, g, lnw, lnb, rk, ones_bd, wa, wb)


def _out1_kernel(x_ref, a_ref, b_ref, wa_ref, wb_ref, out_ref):
    out_ref[...] = x_ref[...] + _mm(a_ref[...], wa_ref[...]) + _mm(b_ref[...], wb_ref[...])


def _out1(x, a, b, wa, wb):
    m = x.shape[0]
    tm = min(512, m)
    row = lambda n: pl.BlockSpec((tm, n), lambda i: (i, 0))
    full = lambda s: pl.BlockSpec(s, lambda i: (0, 0))
    return pl.pallas_call(
        _out1_kernel, grid=(m // tm,),
        in_specs=[row(D_MODEL), row(FOX_DIM), row(SSM_DIM), full((FOX_DIM, D_MODEL)), full((SSM_DIM, D_MODEL))],
        out_specs=row(D_MODEL), out_shape=jax.ShapeDtypeStruct((m, D_MODEL), F32),
        compiler_params=_cparams(("parallel",)), name="out1")(x, a, b, wa, wb)


def _logf_kernel(fc_ref, fr_ref, bc_ref, br_ref, tl_ref, tu_ref, lf_o, cc_o, cr_o, carry_c, carry_r):
    @pl.when(pl.program_id(1) == 0)
    def _():
        carry_c[...] = jnp.zeros_like(carry_c)
        carry_r[...] = jnp.zeros_like(carry_r)

    tl = fc_ref.shape[0]
    lf_c = -_softplus(-(fc_ref[...] + bc_ref[...]))
    lf_o[...] = lf_c
    cc = _mm_sel_l(tl_ref[...], lf_c) + carry_c[0:1]
    cc_o[...] = cc
    carry_c[...] = jnp.broadcast_to(cc[tl - 1:tl], carry_c.shape)
    rep = tl // LANES
    lf_r = -_softplus(-(fr_ref[...] + jnp.tile(br_ref[...], (1, rep))))
    cr = _mm_sel_r(lf_r, tu_ref[...]) + jnp.tile(carry_r[...], (1, rep))
    cr_o[...] = cr
    carry_r[...] = jnp.broadcast_to(cr[:, tl - 1:tl], carry_r.shape)


def _logf(fc, fr, bias_c, bias_r, tri_l, tri_u, tl):
    b, l, _ = fc.shape
    cspec = pl.BlockSpec((None, tl, LANES), lambda i, j: (i, j, 0))
    rspec = pl.BlockSpec((None, 16, tl), lambda i, j: (i, 0, j))
    full = lambda s: pl.BlockSpec(s, lambda i, j: (0, 0))
    return pl.pallas_call(
        _logf_kernel, grid=(b, l // tl),
        in_specs=[cspec, rspec, full((1, LANES)), full((16, LANES)), full((tl, tl)), full((tl, tl))],
        out_specs=[cspec, cspec, rspec],
        out_shape=[jax.ShapeDtypeStruct((b, l, LANES), F32), jax.ShapeDtypeStruct((b, l, LANES), F32),
                   jax.ShapeDtypeStruct((b, 16, l), F32)],
        scratch_shapes=[pltpu.VMEM((SUBLANES, LANES), F32), pltpu.VMEM((16, LANES), F32)],
        compiler_params=_cparams(("parallel", "arbitrary")), name="logf")(fc, fr, bias_c, bias_r, tri_l, tri_u)


def _fox_prompt_kernel(q_ref, k_ref, v_ref, cc_ref, cr_ref, o_ref, qm_ref, cq_ref, m_ref, l_ref, acc_ref):
    p = pl.program_id(1)
    qi = pl.program_id(2)
    ki = pl.program_id(3)
    tq = q_ref.shape[0]
    tk = k_ref.shape[0]
    lane = _iota((tq, LANES), 1)

    @pl.when(ki == 0)
    def _():
        q = q_ref[...] * (HEAD_DIM ** -0.5)
        cc = cc_ref[...]
        sel_row = _iota((LANES, LANES), 0)
        for h in range(2):
            mask = (lane < HEAD_DIM) if h == 0 else (lane >= HEAD_DIM)
            qm_ref[h] = jnp.where(mask, q, 0.0).astype(BF16)
            cq_ref[h] = _mm_sel_r(cc, (sel_row == 2 * p + h).astype(BF16))
        m_ref[...] = jnp.full_like(m_ref, NEG_BIG)
        l_ref[...] = jnp.zeros_like(l_ref)
        acc_ref[...] = jnp.zeros_like(acc_ref)

    @pl.when(ki <= qi)
    def _():
        kb = k_ref[...].astype(BF16)
        vb = v_ref[...].astype(BF16)
        rep = tk // LANES
        causal = (qi * tq + _iota((tq, tk), 0)) >= (ki * tk + _iota((tq, tk), 1))
        for h in range(2):
            ck = cr_ref[pl.ds(2 * p + h, 1), :]
            s = lax.dot_general(qm_ref[h], kb, _NT, preferred_element_type=F32)
            s = s + (jnp.tile(cq_ref[h], (1, rep)) - ck)
            s = jnp.where(causal, s, NEG_BIG)
            m_prev = m_ref[h]
            m_new = jnp.maximum(m_prev, jnp.max(s, axis=1, keepdims=True))
            pr = jnp.exp(s - jnp.tile(m_new, (1, rep)))
            corr = jnp.exp(m_prev - m_new)
            l_ref[h] = corr * l_ref[h] + jnp.sum(pr, axis=1, keepdims=True)
            acc_ref[h] = corr * acc_ref[h] + jnp.dot(pr.astype(BF16), vb, preferred_element_type=F32)
            m_ref[h] = m_new

    @pl.when(ki == pl.num_programs(3) - 1)
    def _():
        o_ref[...] = jnp.where(lane < HEAD_DIM, acc_ref[0] / l_ref[0], acc_ref[1] / l_ref[1])


def _fox_prompt(q, k, v, cc, cr, tq):
    b, l, _ = q.shape
    tk = tq
    kv_idx = lambda i, p, qi, ki: (i, jnp.minimum(ki, qi), p)
    return pl.pallas_call(
        _fox_prompt_kernel, grid=(b, FOX_DIM // LANES, l // tq, l // tk),
        in_specs=[pl.BlockSpec((None, tq, LANES), lambda i, p, qi, ki: (i, qi, p)),
                  pl.BlockSpec((None, tk, LANES), kv_idx),
                  pl.BlockSpec((None, tk, LANES), kv_idx),
                  pl.BlockSpec((None, tq, LANES), lambda i, p, qi, ki: (i, qi, 0)),
                  pl.BlockSpec((None, 16, tk), lambda i, p, qi, ki: (i, 0, jnp.minimum(ki, qi)))],
        out_specs=pl.BlockSpec((None, tq, LANES), lambda i, p, qi, ki: (i, qi, p)),
        out_shape=jax.ShapeDtypeStruct((b, l, FOX_DIM), F32),
        scratch_shapes=[pltpu.VMEM((2, tq, LANES), BF16), pltpu.VMEM((2, tq, LANES), F32),
                        pltpu.VMEM((2, tq, LANES), F32), pltpu.VMEM((2, tq, LANES), F32),
                        pltpu.VMEM((2, tq, LANES), F32)],
        compiler_params=_cparams(("parallel", "parallel", "parallel", "arbitrary")),
        name="fox_prompt")(q, k, v, cc, cr)


def _fox_sample_kernel(pt_ref, qbd_ref, cq_ref, kn_ref, vn_ref, cn_ref, kp_ref, vp_ref, lf_ref, o_ref,
                       m_ref, l_ref, acc_ref, suf_ref):
    del pt_ref
    s_id = pl.program_id(1)
    nq = qbd_ref.shape[0]
    rep = nq // FOX_HEADS
    row = _iota((nq, PAGE), 0)
    col = _iota((nq, PAGE), 1)

    def attend(s, vb):
        m_prev = m_ref[...]
        m_new = jnp.maximum(m_prev, jnp.max(s, axis=1, keepdims=True))
        pr = jnp.exp(s - m_new)
        corr = jnp.exp(m_prev - m_new)
        l_ref[...] = corr * l_ref[...] + jnp.sum(pr, axis=1, keepdims=True)
        acc_ref[...] = corr[:, 0:1] * acc_ref[...] + jnp.dot(pr.astype(BF16), vb, preferred_element_type=F32)
        m_ref[...] = m_new

    @pl.when(s_id == 0)
    def _():
        m_ref[...] = jnp.full_like(m_ref, NEG_BIG)
        l_ref[...] = jnp.zeros_like(l_ref)
        acc_ref[...] = jnp.zeros_like(acc_ref)
        suf_ref[...] = jnp.zeros_like(suf_ref)
        s = lax.dot_general(qbd_ref[...], kn_ref[...].astype(BF16), _NT, preferred_element_type=F32)
        s = s + (cq_ref[...] - jnp.tile(cn_ref[...], (rep, 1)))
        s = jnp.where(col <= row // FOX_HEADS, s, NEG_BIG)
        attend(s, vn_ref[...].astype(BF16))

    @pl.when(s_id > 0)
    def _():
        lf_t = lf_ref[...]
        later = (_iota((PAGE, PAGE), 0) > _iota((PAGE, PAGE), 1)).astype(BF16)
        suffix = _mm_sel_r(lf_t, later) + suf_ref[:, 0:1]
        suf_ref[...] = jnp.broadcast_to(suffix[:, 0:1] + lf_t[:, 0:1], suf_ref.shape)
        s = lax.dot_general(qbd_ref[...], kp_ref[...].astype(BF16), _NT, preferred_element_type=F32)
        s = s + (cq_ref[...] + jnp.tile(suffix, (rep, 1)))
        attend(s, vp_ref[...].astype(BF16))

    @pl.when(s_id == pl.num_programs(1) - 1)
    def _():
        lane_head = _iota((nq, FOX_DIM), 1) // HEAD_DIM
        row_head = _iota((nq, FOX_DIM), 0) % FOX_HEADS
        own = jnp.where(lane_head == row_head, acc_ref[...], 0.0)
        fold = (_iota((FOX_DIM, HEAD_DIM), 0) % HEAD_DIM == _iota((FOX_DIM, HEAD_DIM), 1)).astype(BF16)
        o_ref[...] = _mm_sel_r(own, fold) / l_ref[:, 0:HEAD_DIM]


def _fox_sample(page_table, qbd, cq, k_new, v_new, c_new, pool_k, pool_v, pool_lf):
    b, nq, _ = qbd.shape
    n_pages = page_table.shape[1]

    def page(i, s, pt):
        return (pt[i, n_pages - 1 - jnp.maximum(s - 1, 0)], 0, 0)

    per_b = lambda shape: pl.BlockSpec((None,) + shape, lambda i, s, pt: (i, 0, 0))
    grid_spec = pltpu.PrefetchScalarGridSpec(
        num_scalar_prefetch=1, grid=(b, n_pages + 1),
        in_specs=[per_b((nq, FOX_DIM)), per_b((nq, LANES)), per_b((PAGE, FOX_DIM)), per_b((PAGE, FOX_DIM)),
                  per_b((FOX_HEADS, LANES)),
                  pl.BlockSpec((None, PAGE, FOX_DIM), page), pl.BlockSpec((None, PAGE, FOX_DIM), page),
                  pl.BlockSpec((None, FOX_HEADS, PAGE), page)],
        out_specs=per_b((nq, HEAD_DIM)),
        scratch_shapes=[pltpu.VMEM((nq, LANES), F32), pltpu.VMEM((nq, LANES), F32),
                        pltpu.VMEM((nq, FOX_DIM), F32), pltpu.VMEM((FOX_HEADS, LANES), F32)])
    return pl.pallas_call(
        _fox_sample_kernel, grid_spec=grid_spec,
        out_shape=jax.ShapeDtypeStruct((b, nq, HEAD_DIM), F32),
        compiler_params=_cparams(("parallel", "arbitrary")), name="fox_sample")(
            page_table, qbd, cq, k_new, v_new, c_new, pool_k, pool_v, pool_lf)


def _ssm_kernel(z_ref, xbc_ref, fc_ref, fr_ref, buf_ref, s0_ref, cw_ref, cb_ref, dtb_c_ref, dtb_r_ref,
                a_c_ref, a_r_ref, d_ref, nw_ref, tl_ref, tu_ref, y_ref, sout_ref, carry_ref, s_ref, *, valid):
    @pl.when(pl.program_id(1) == 0)
    def _():
        carry_ref[...] = buf_ref[...]
        s_ref[...] = s0_ref[...]

    raw = xbc_ref[...]
    cs = raw.shape[0]
    carry = carry_ref[...]
    cw = cw_ref[...]
    conv = (cw[0:1] * _shifted(raw, carry, 3) + cw[1:2] * _shifted(raw, carry, 2)
            + cw[2:3] * _shifted(raw, carry, 1) + cw[3:4] * raw)
    carry_ref[...] = raw[cs - SUBLANES:]
    xbc = _silu(conv + cb_ref[...])
    x = xbc[:, :SSM_DIM]

    dt_c = _softplus(fc_ref[...] + dtb_c_ref[...])
    dt_r = _softplus(fr_ref[...] + dtb_r_ref[...])
    if valid < cs:
        dt_c = jnp.where(_iota((cs, LANES), 0) < valid, dt_c, 0.0)
        dt_r = jnp.where(_iota((16, cs), 1) < valid, dt_r, 0.0)
    acum_c = _mm_sel_l(tl_ref[...], dt_c * a_c_ref[...])
    acum_r = _mm_sel_r(dt_r * a_r_ref[...], tu_ref[...])
    w_c = jnp.exp(acum_c[cs - 1:cs] - acum_c) * dt_c

    sel_row = _iota((LANES, LANES), 0)
    sel_lane = _iota((LANES, LANES), 1)
    lane = _iota((cs, LANES), 1)
    causal = _iota((cs, cs), 0) >= _iota((cs, cs), 1)
    y_pairs = []
    for p in range(SSM_HEADS // 2):
        g = (2 * p) // (SSM_HEADS // SSM_GROUPS)
        bm = xbc[:, SSM_DIM + g * SSM_STATE:SSM_DIM + (g + 1) * SSM_STATE]
        cm = xbc[:, SSM_DIM + (SSM_GROUPS + g) * SSM_STATE:SSM_DIM + (SSM_GROUPS + g + 1) * SSM_STATE]
        cb = _mm(cm, bm, _NT)
        xp = x[:, p * LANES:(p + 1) * LANES]
        y_intra = []
        dec_rows = []
        for h in range(2):
            hh = 8 + 2 * p + h
            a_col = _mm_sel_r(acum_c, (sel_row == hh).astype(BF16))
            seg = a_col[:, :cs] - acum_r[hh:hh + 1, :]
            lmat = jnp.exp(jnp.where(causal, seg, NEG_BIG))
            gmat = cb * lmat * dt_r[hh:hh + 1, :]
            y_intra.append(_mm(gmat, xp))
            dec_rows.append(jnp.broadcast_to(jnp.exp(a_col[cs - 1:cs]), (HEAD_DIM, LANES)))
        pair_sel = (sel_row == 8 + 2 * p + sel_lane // HEAD_DIM).astype(BF16)
        ea_pair = jnp.exp(_mm_sel_r(acum_c, pair_sel))
        w_pair = _mm_sel_r(w_c, pair_sel)
        s_pair = s_ref[p]
        y_inter = _mm(cm, s_pair, _NT) * ea_pair
        y_pairs.append(jnp.where(lane < HEAD_DIM, y_intra[0], y_intra[1]) + y_inter)
        s_ref[p] = s_pair * jnp.concatenate(dec_rows, axis=0) + _mm(xp * w_pair, bm, _TN)

    y = jnp.concatenate(y_pairs, axis=1) + d_ref[...] * x
    y = y * _silu(z_ref[...])
    y_ref[...] = _rmsnorm(y, nw_ref[...])

    @pl.when(pl.program_id(1) == pl.num_programs(1) - 1)
    def _():
        sout_ref[...] = s_ref[...]


def _ssm(z, xbc, fc, fr, buf8, s0, cw8, cb, dtb_c, dtb_r, a_c, a_r, d_exp, nw, tri_l, tri_u, valid):
    b, l, _ = z.shape
    cs = LANES
    full = lambda s: pl.BlockSpec(s, lambda i, j: (0,) * len(s))
    sspec = pl.BlockSpec((None, 4, LANES, LANES), lambda i, j: (i, 0, 0, 0))
    return pl.pallas_call(
        functools.partial(_ssm_kernel, valid=valid), grid=(b, l // cs),
        in_specs=[pl.BlockSpec((None, cs, SSM_DIM), lambda i, j: (i, j, 0)),
                  pl.BlockSpec((None, cs, SSM_CONV_DIM), lambda i, j: (i, j, 0)),
                  pl.BlockSpec((None, cs, LANES), lambda i, j: (i, j, 0)),
                  pl.BlockSpec((None, 16, cs), lambda i, j: (i, 0, j)),
                  pl.BlockSpec((None, SUBLANES, SSM_CONV_DIM), lambda i, j: (i, 0, 0)),
                  sspec,
                  full((SUBLANES, SSM_CONV_DIM)), full((1, SSM_CONV_DIM)), full((1, LANES)), full((16, LANES)),
                  full((1, LANES)), full((16, LANES)), full((1, SSM_DIM)), full((1, SSM_DIM)),
                  full((cs, cs)), full((cs, cs))],
        out_specs=[pl.BlockSpec((None, cs, SSM_DIM), lambda i, j: (i, j, 0)), sspec],
        out_shape=[jax.ShapeDtypeStruct((b, l, SSM_DIM), F32), jax.ShapeDtypeStruct((b, 4, LANES, LANES), F32)],
        scratch_shapes=[pltpu.VMEM((SUBLANES, SSM_CONV_DIM), F32), pltpu.VMEM((4, LANES, LANES), F32)],
        compiler_params=_cparams(("parallel", "arbitrary")), name="ssm")(
            z, xbc, fc, fr, buf8, s0, cw8, cb, dtb_c, dtb_r, a_c, a_r, d_exp, nw, tri_l, tri_u)


def _pad_rows(a, rows):
    return jnp.pad(a, ((0, 0), (rows - a.shape[1], 0), (0, 0)))


def _tri(n, seg):
    i = jnp.arange(n)
    same = (i[:, None] // seg) == (i[None, :] // seg)
    lower = (same & (i[:, None] >= i[None, :])).astype(BF16)
    return lower, lower.T


def _lane_vec(v, offset):
    n = v.shape[0]
    col = jnp.zeros((1, LANES), F32).at[0, offset:offset + n].set(v)
    row = jnp.zeros((16, LANES), F32).at[offset:offset + n, :].set(jnp.broadcast_to(v[:, None], (n, LANES)))
    return col, row


def _layer0(h, bsz, l, sc_buf, shift_prev, wkv0, p):
    m = bsz * l
    tl = min(512, l)
    z_sc, z_rw = _in_proj(h, p['g_mix'], p['w_in'], (3 * SC_DIM, RW_SHIFT_W))
    z_sc = z_sc.reshape(bsz, l, 3 * SC_DIM)
    z_rw = z_rw.reshape(bsz, l, RW_SHIFT_W)
    ya, sc_last = _sconv(z_sc, _pad_rows(sc_buf, SUBLANES), p['sc_w8'], tl)
    r, lw, kp, v, kn, kb, g = _rwkv_pre(z_rw, _pad_rows(shift_prev[:, None], SUBLANES), p['mu'], p['w0'], p['w2p'],
                                        p['a0'], p['a2p'], p['g2'], p['k_k'], p['k_a'], p['ones_bd'], tl)
    ht = jnp.swapaxes(wkv0, 2, 3).reshape(bsz, 4, 2, HEAD_DIM, HEAD_DIM)
    eye2 = jnp.eye(2, dtype=F32)
    h_bd = jnp.einsum('bpgji,gk->bpgjki', ht, eye2).reshape(bsz, 4, LANES, LANES)
    lpad = -(-l // RW_CHUNK) * RW_CHUNK
    padl = lambda t: jnp.pad(t, ((0, 0), (0, lpad - l), (0, 0)))
    o, h_fin = _rwkv_chunk(padl(r), padl(lw), padl(kp), padl(v), padl(kn), padl(kb), h_bd, RW_CHUNK)
    o = o[:, :l]
    h_fin = h_fin.reshape(bsz, 4, 2, HEAD_DIM, 2, HEAD_DIM)
    wkv_new = jnp.stack([h_fin[:, :, 0, :, 0], h_fin[:, :, 1, :, 1]], axis=2).reshape(bsz, RW_HEADS, HEAD_DIM, HEAD_DIM)
    wkv_new = jnp.swapaxes(wkv_new, 2, 3)
    flat = lambda t: t.reshape(m, t.shape[-1])
    h = _out0(h, flat(ya), flat(o), flat(r), flat(kp), flat(v), flat(g), p['ln_w'], p['ln_b'], p['r_k'],
              p['ones_bd'], p['wo_a'], p['wo_b'])
    h = _ffn(h, p['g_ffn'], p['w_gate'], p['w_up'], p['w_down'], p['g_ffn'], False)
    return h, sc_last[:, SUBLANES - 2:], z_rw[:, -1], wkv_new


def _layer1(h, bsz, l, past, conv_buf, ssm0, p, g_final):
    m = bsz * l
    q, k, v, zg, xbc, fdt, fdt_t = _in_proj(h, p['g_mix'], p['w_in'],
                                            (FOX_DIM, FOX_DIM, FOX_DIM, SSM_DIM, SSM_CONV_DIM, LANES), p['w_t'])
    fr = jnp.swapaxes(fdt_t.reshape(16, bsz, l), 0, 1)
    if past is None:
        tl = min(512, l)
        tri_l, tri_u = _tri(tl, tl)
        lf, cc, cr = _logf(fdt.reshape(bsz, l, LANES), fr, p['fb_c'], p['fb_r'], tri_l, tri_u, tl)
        yf = _fox_prompt(q.reshape(bsz, l, FOX_DIM), k.reshape(bsz, l, FOX_DIM), v.reshape(bsz, l, FOX_DIM),
                         cc, cr, tl)
        logf_new = lf[..., :FOX_HEADS]
    else:
        page_table, pool_k, pool_v, pool_lf = past
        tri_l, tri_u = _tri(m, l)
        fr_flat = fdt_t.reshape(1, 16, m)
        lf, cc, cr = _logf(fdt.reshape(1, m, LANES), fr_flat, p['fb_c'], p['fb_r'], tri_l, tri_u, m)
        logf_new = lf.reshape(bsz, l, LANES)[..., :FOX_HEADS]
        q4 = q.reshape(bsz, l, FOX_HEADS, HEAD_DIM) * (HEAD_DIM ** -0.5)
        qbd = jnp.einsum('bqhd,hg->bqhgd', q4, jnp.eye(FOX_HEADS, dtype=F32))
        qbd = qbd.reshape(bsz, l * FOX_HEADS, FOX_DIM).astype(BF16)
        c_q = cc.reshape(bsz, l, LANES)[..., :FOX_HEADS].reshape(bsz, l * FOX_HEADS, 1)
        c_q = jnp.broadcast_to(c_q, (bsz, l * FOX_HEADS, LANES))
        c_new = jnp.pad(jnp.swapaxes(cr.reshape(16, bsz, l), 0, 1)[:, :FOX_HEADS], ((0, 0), (0, 0), (0, PAGE - l)))
        pad_new = lambda t: jnp.pad(t.reshape(bsz, l, FOX_DIM), ((0, 0), (0, PAGE - l), (0, 0)))
        yf = _fox_sample(page_table, qbd, c_q, pad_new(k), pad_new(v), c_new, pool_k, pool_v, pool_lf)
        yf = yf.reshape(bsz, l, FOX_DIM)
    ctl, ctu = _tri(SSD_CHUNK, SSD_CHUNK)
    xbc3 = xbc.reshape(bsz, l, SSM_CONV_DIM)
    if l % SSD_CHUNK == 0:
        lpad, valid = l, SSD_CHUNK
    else:
        assert l < SSD_CHUNK
        lpad, valid = SSD_CHUNK, l
    padl = lambda t: jnp.pad(t, ((0, 0), (0, lpad - l), (0, 0)))
    ym, s_fin = _ssm(padl(zg.reshape(bsz, l, SSM_DIM)), padl(xbc3), padl(fdt.reshape(bsz, l, LANES)),
                     jnp.pad(fr, ((0, 0), (0, 0), (0, lpad - l))),
                     _pad_rows(conv_buf, SUBLANES), ssm0.reshape(bsz, 4, LANES, LANES), p['cw8'], p['cb'],
                     p['dtb_c'], p['dtb_r'], p['a_c'], p['a_r'], p['d_exp'], p['nw'], ctl, ctu, valid)
    ym = ym[:, :l]
    h = _out1(h, yf.reshape(m, FOX_DIM), ym.reshape(m, SSM_DIM), p['wo_a'], p['wo_b'])
    h = _ffn(h, p['g_ffn'], p['w_gate'], p['w_up'], p['w_down'], g_final, True)
    shape5 = (bsz, l, FOX_HEADS, HEAD_DIM)
    return (h, k.reshape(shape5), v.reshape(shape5), logf_new, xbc3[:, l - 3:],
            s_fin.reshape(bsz, SSM_HEADS, HEAD_DIM, SSM_STATE))


def kernel(x_prompt, x_sample, state_sc, state_shift, state_wkv, cache_k, cache_v, cache_logf, state_ssm_conv, state_ssm, page_table, norm_mix, norm_ffn, norm_final, w_in0, sc_conv_w, rw_mu, rw_w0, rw_w2, rw_a0, rw_a2, rw_g2, rw_k_k, rw_k_a, rw_r_k, rw_ln_w, rw_ln_b, w_out0, w_in1, fox_f_bias, ssm_conv_w, ssm_conv_b, ssm_dt_bias, ssm_a_log, ssm_d, ssm_norm_w, w_out1, w_gate, w_up, w_down):
    bp, lp, _ = x_prompt.shape
    bs, ls, _ = x_sample.shape
    row = lambda v: v.reshape(1, -1).astype(F32)
    head_ids = jnp.arange(RW_DIM) // HEAD_DIM
    ones_bd = (head_ids[:, None] == head_ids[None, :]).astype(BF16)
    zpad = jnp.zeros((64, RW_DIM), F32)

    p0 = dict(
        g_mix=row(norm_mix[0]), g_ffn=row(norm_ffn[0]), w_in=w_in0[0].astype(BF16),
        sc_w8=jnp.pad(sc_conv_w[0], ((0, SUBLANES - sc_conv_w.shape[1]), (0, 0))),
        mu=row(rw_mu[0]), w0=row(rw_w0[0]), a0=row(rw_a0[0]),
        w2p=jnp.concatenate([rw_w2[0], zpad], axis=0).astype(BF16),
        a2p=jnp.concatenate([zpad, rw_a2[0]], axis=0).astype(BF16),
        g2=rw_g2[0].astype(BF16), k_k=row(rw_k_k[0]), k_a=row(rw_k_a[0]), r_k=row(rw_r_k[0]),
        ln_w=row(rw_ln_w[0]), ln_b=row(rw_ln_b[0]), ones_bd=ones_bd,
        wo_a=w_out0[0, :SC_DIM].astype(BF16), wo_b=w_out0[0, SC_DIM:].astype(BF16),
        w_gate=w_gate[0].astype(BF16), w_up=w_up[0].astype(BF16), w_down=w_down[0].astype(BF16))

    w1 = w_in1[0]
    c0 = 3 * FOX_DIM
    f_cols = w1[:, c0:c0 + FOX_HEADS]
    z_cols = w1[:, c0 + FOX_HEADS:c0 + FOX_HEADS + SSM_DIM]
    xbc_cols = w1[:, c0 + FOX_HEADS + SSM_DIM:c0 + FOX_HEADS + SSM_DIM + SSM_CONV_DIM]
    dt_cols = w1[:, c0 + FOX_HEADS + SSM_DIM + SSM_CONV_DIM:]
    fdt_cols = jnp.concatenate([f_cols, dt_cols, jnp.zeros((D_MODEL, LANES - 16), F32)], axis=1)
    w1r = jnp.concatenate([w1[:, :c0], z_cols, xbc_cols, fdt_cols], axis=1).astype(BF16)
    fb_c, fb_r = _lane_vec(fox_f_bias[0], 0)
    dtb_c, dtb_r = _lane_vec(ssm_dt_bias[0], 8)
    a_c, a_r = _lane_vec(-jnp.exp(ssm_a_log[0].astype(F32)), 8)
    p1 = dict(
        g_mix=row(norm_mix[1]), g_ffn=row(norm_ffn[1]), w_in=w1r, w_t=fdt_cols[:, :16].T.astype(BF16),
        fb_c=fb_c, fb_r=fb_r, dtb_c=dtb_c, dtb_r=dtb_r, a_c=a_c, a_r=a_r,
        cw8=jnp.pad(ssm_conv_w[0], ((0, SUBLANES - ssm_conv_w.shape[1]), (0, 0))), cb=row(ssm_conv_b[0]),
        d_exp=row(jnp.repeat(ssm_d[0], HEAD_DIM)), nw=row(ssm_norm_w[0]),
        wo_a=w_out1[0, :FOX_DIM].astype(BF16), wo_b=w_out1[0, FOX_DIM:].astype(BF16),
        w_gate=w_gate[1].astype(BF16), w_up=w_up[1].astype(BF16), w_down=w_down[1].astype(BF16))
    g_final = row(norm_final)

    n_pool = cache_k.shape[1]
    past = (page_table, cache_k[0].reshape(n_pool, PAGE, FOX_DIM), cache_v[0].reshape(n_pool, PAGE, FOX_DIM),
            jnp.swapaxes(cache_logf[0], 1, 2))

    hp = x_prompt.reshape(bp * lp, D_MODEL)
    hs = x_sample.reshape(bs * ls, D_MODEL)
    hp, sc_p, sh_p, wkv_p = _layer0(hp, bp, lp, jnp.zeros((bp, 2, SC_DIM), F32), jnp.zeros((bp, RW_SHIFT_W), F32),
                                    jnp.zeros((bp, RW_HEADS, HEAD_DIM, HEAD_DIM), F32), p0)
    hs, sc_s, sh_s, wkv_s = _layer0(hs, bs, ls, state_sc[0], state_shift[0], state_wkv[0], p0)
    yp, k_p, v_p, lf_p, cv_p, ssm_p = _layer1(hp, bp, lp, None, jnp.zeros((bp, 3, SSM_CONV_DIM), F32),
                                              jnp.zeros((bp, SSM_HEADS, HEAD_DIM, SSM_STATE), F32), p1, g_final)
    ys, k_s, v_s, lf_s, cv_s, ssm_s = _layer1(hs, bs, ls, past, state_ssm_conv[0], state_ssm[0], p1, g_final)
    one = lambda t: t[None]
    return (yp.reshape(bp, lp, D_MODEL), ys.reshape(bs, ls, D_MODEL),
            one(sc_p), one(sc_s), one(sh_p), one(sh_s), one(wkv_p), one(wkv_s),
            one(k_p), one(k_s), one(v_p), one(v_s), one(lf_p), one(lf_s), one(cv_p), one(cv_s),
            one(ssm_p), one(ssm_s))
```

```python
import functools

import jax
import jax.numpy as jnp
from jax import lax
from jax.experimental import pallas as pl
from jax.experimental.pallas import tpu as pltpu

F32 = jnp.float32
BF16 = jnp.bfloat16

D_MODEL = 1024
HEAD_DIM = 64
NORM_EPS = 1e-6
SC_DIM = 512
RW_DIM = 512
RW_HEADS = 8
RW_SHIFT_W = 3 * RW_DIM + 64 + 64 + 128
RW_GN_EPS = 64e-5
FOX_DIM = 512
FOX_HEADS = 8
SSM_DIM = 512
SSM_HEADS = 8
SSM_GROUPS = 2
SSM_STATE = 128
SSM_CONV_DIM = SSM_DIM + 2 * SSM_GROUPS * SSM_STATE
FFN_HIDDEN = 2816
PAGE = 128
SSD_CHUNK = 128
FOX_PAGE_GROUP = 8
RW_CHUNK = 64

LANES = 128
SUBLANES = 8
VMEM_LIMIT = 56 * 1024 * 1024

NEG_BIG = -1e30


def _cparams(sem):
    return pltpu.CompilerParams(dimension_semantics=sem, vmem_limit_bytes=VMEM_LIMIT)


_NN = (((1,), (0,)), ((), ()))
_NT = (((1,), (1,)), ((), ()))
_TN = (((0,), (0,)), ((), ()))


def _mm(a, b, dims=_NN):
    return lax.dot_general(a.astype(BF16), b.astype(BF16), dims, preferred_element_type=F32)


def _split2(x):
    hi = x.astype(BF16)
    lo = (x - hi.astype(F32)).astype(BF16)
    return hi, lo


def _split3(x):
    hi = x.astype(BF16)
    r1 = x - hi.astype(F32)
    mid = r1.astype(BF16)
    lo = (r1 - mid.astype(F32)).astype(BF16)
    return hi, mid, lo


def _mm3(a, b, dims=_NN):
    ah, al = _split2(a)
    bh, bl = _split2(b)
    d = functools.partial(lax.dot_general, dimension_numbers=dims, preferred_element_type=F32)
    return d(ah, bh) + (d(al, bh) + d(ah, bl))


def _mm_sel_r(x, e, dims=_NN):
    h, m, l = _split3(x)
    d = functools.partial(lax.dot_general, dimension_numbers=dims, preferred_element_type=F32)
    return d(h, e) + (d(m, e) + d(l, e))


def _mm_sel_l(e, x, dims=_NN):
    h, m, l = _split3(x)
    d = functools.partial(lax.dot_general, dimension_numbers=dims, preferred_element_type=F32)
    return d(e, h) + (d(e, m) + d(e, l))


def _rmsnorm(x, g):
    return x * lax.rsqrt(jnp.mean(x * x, axis=-1, keepdims=True) + NORM_EPS) * g


def _sigmoid(x):
    return 1.0 / (1.0 + jnp.exp(-x))


def _silu(x):
    return x * _sigmoid(x)


def _softplus(x):
    return jnp.maximum(x, 0.0) + jnp.log(1.0 + jnp.exp(-jnp.abs(x)))


def _iota(shape, dim):
    return lax.broadcasted_iota(jnp.int32, shape, dim)


def _shifted(u, carry8, d):
    tl = u.shape[0]
    s = pltpu.roll(u, d, 0)
    c = pltpu.roll(carry8, d, 0)
    row = _iota((SUBLANES, u.shape[1]), 0)
    head = jnp.where(row < d, c, s[:SUBLANES])
    if tl == SUBLANES:
        return head
    return jnp.concatenate([head, s[SUBLANES:]], axis=0)


def _in_proj_kernel(*refs, splits, has_t):
    x_ref, g_ref, w_ref = refs[:3]
    pos = 3
    if has_t:
        wt_ref = refs[pos]
        pos += 1
    outs = refs[pos:pos + len(splits)]
    nb = _rmsnorm(x_ref[...], g_ref[...]).astype(BF16)
    off = 0
    for o_ref, width in zip(outs, splits):
        o_ref[...] = jnp.dot(nb, w_ref[:, off:off + width], preferred_element_type=F32)
        off += width
    if has_t:
        refs[pos + len(splits)][...] = lax.dot_general(wt_ref[...], nb, _NT, preferred_element_type=F32)


def _in_proj(x, g, w, splits, wt=None):
    m = x.shape[0]
    tm = min(512, m)
    n = sum(splits)
    in_specs = [pl.BlockSpec((tm, D_MODEL), lambda i: (i, 0)),
                pl.BlockSpec((1, D_MODEL), lambda i: (0, 0)),
                pl.BlockSpec((D_MODEL, n), lambda i: (0, 0))]
    args = [x, g, w]
    out_shape = [jax.ShapeDtypeStruct((m, s), F32) for s in splits]
    out_specs = [pl.BlockSpec((tm, s), lambda i: (i, 0)) for s in splits]
    if wt is not None:
        in_specs.append(pl.BlockSpec((16, D_MODEL), lambda i: (0, 0)))
        args.append(wt)
        out_shape.append(jax.ShapeDtypeStruct((16, m), F32))
        out_specs.append(pl.BlockSpec((16, tm), lambda i: (0, i)))
    return pl.pallas_call(
        functools.partial(_in_proj_kernel, splits=tuple(splits), has_t=wt is not None),
        grid=(m // tm,), in_specs=in_specs, out_specs=out_specs, out_shape=out_shape,
        compiler_params=_cparams(("parallel",)), name="in_proj")(*args)


def _ffn_kernel(x_ref, g_ref, wg_ref, wu_ref, wd_ref, gf_ref, o_ref, xn_ref, acc_ref, *, final_norm):
    j = pl.program_id(1)

    @pl.when(j == 0)
    def _():
        x = x_ref[...]
        xn_ref[...] = _rmsnorm(x, g_ref[...]).astype(BF16)
        acc_ref[...] = x

    xn = xn_ref[...]
    gate = jnp.dot(xn, wg_ref[...], preferred_element_type=F32)
    up = jnp.dot(xn, wu_ref[...], preferred_element_type=F32)
    act = (_silu(gate) * up).astype(BF16)
    acc_ref[...] += jnp.dot(act, wd_ref[...], preferred_element_type=F32)

    @pl.when(j == pl.num_programs(1) - 1)
    def _():
        y = acc_ref[...]
        if final_norm:
            y = _rmsnorm(y, gf_ref[...])
        o_ref[...] = y


def _ffn(x, g, wg, wu, wd, gfin, final_norm):
    m = x.shape[0]
    tm = min(512, m)
    th = FFN_HIDDEN // 2
    return pl.pallas_call(
        functools.partial(_ffn_kernel, final_norm=final_norm),
        grid=(m // tm, FFN_HIDDEN // th),
        in_specs=[pl.BlockSpec((tm, D_MODEL), lambda i, j: (i, 0)),
                  pl.BlockSpec((1, D_MODEL), lambda i, j: (0, 0)),
                  pl.BlockSpec((D_MODEL, th), lambda i, j: (0, j)),
                  pl.BlockSpec((D_MODEL, th), lambda i, j: (0, j)),
                  pl.BlockSpec((th, D_MODEL), lambda i, j: (j, 0)),
                  pl.BlockSpec((1, D_MODEL), lambda i, j: (0, 0))],
        out_specs=pl.BlockSpec((tm, D_MODEL), lambda i, j: (i, 0)),
        out_shape=jax.ShapeDtypeStruct((m, D_MODEL), F32),
        scratch_shapes=[pltpu.VMEM((tm, D_MODEL), BF16), pltpu.VMEM((tm, D_MODEL), F32)],
        compiler_params=_cparams(("parallel", "arbitrary")), name="ffn")(x, g, wg, wu, wd, gfin)


def _sconv_kernel(z_ref, buf_ref, w_ref, y_ref, last_ref, carry_ref):
    @pl.when(pl.program_id(1) == 0)
    def _():
        carry_ref[...] = buf_ref[...]

    z = z_ref[...]
    tl = z.shape[0]
    gb, gc, h = z[:, :SC_DIM], z[:, SC_DIM:2 * SC_DIM], z[:, 2 * SC_DIM:]
    u = gc * h
    carry = carry_ref[...]
    w = w_ref[...]
    y = w[0:1] * _shifted(u, carry, 2) + w[1:2] * _shifted(u, carry, 1) + w[2:3] * u
    y_ref[...] = gb * y
    tail = u[tl - SUBLANES:]
    carry_ref[...] = tail
    last_ref[...] = tail


def _sconv(z, buf8, w8, tl):
    b, l, _ = z.shape
    return pl.pallas_call(
        _sconv_kernel, grid=(b, l // tl),
        in_specs=[pl.BlockSpec((None, tl, 3 * SC_DIM), lambda i, j: (i, j, 0)),
                  pl.BlockSpec((None, SUBLANES, SC_DIM), lambda i, j: (i, 0, 0)),
                  pl.BlockSpec((SUBLANES, SC_DIM), lambda i, j: (0, 0))],
        out_specs=[pl.BlockSpec((None, tl, SC_DIM), lambda i, j: (i, j, 0)),
                   pl.BlockSpec((None, SUBLANES, SC_DIM), lambda i, j: (i, 0, 0))],
        out_shape=[jax.ShapeDtypeStruct((b, l, SC_DIM), F32),
                   jax.ShapeDtypeStruct((b, SUBLANES, SC_DIM), F32)],
        scratch_shapes=[pltpu.VMEM((SUBLANES, SC_DIM), F32)],
        compiler_params=_cparams(("parallel", "arbitrary")), name="sconv")(z, buf8, w8)


def _rwkv_pre_kernel(z_ref, sh_ref, mu_ref, w0_ref, w2_ref, a0_ref, a2_ref, g2_ref, kk_ref, ka_ref, ones_ref,
                     r_o, lw_o, kp_o, v_o, kn_o, kb_o, g_o, carry_ref):
    @pl.when(pl.program_id(1) == 0)
    def _():
        carry_ref[...] = sh_ref[...]

    z = z_ref[...]
    tl = z.shape[0]
    zs = _shifted(z, carry_ref[...], 1)
    carry_ref[...] = z[tl - SUBLANES:]
    zx = z + mu_ref[...] * (zs - z)
    r = zx[:, :RW_DIM]
    k = zx[:, RW_DIM:2 * RW_DIM]
    v = zx[:, 2 * RW_DIM:3 * RW_DIM]
    wa = zx[:, 3 * RW_DIM:3 * RW_DIM + 128]
    gd = zx[:, 3 * RW_DIM + 128:]
    w_log = -_softplus(-(w0_ref[...] + _mm(jnp.tanh(wa), w2_ref[...]))) - 0.5
    lw_o[...] = -jnp.exp(w_log)
    alpha = _sigmoid(a0_ref[...] + _mm(wa, a2_ref[...]))
    g_o[...] = _mm(_sigmoid(gd), g2_ref[...])
    kk = k * kk_ref[...]
    n2 = _mm_sel_r(kk * kk, ones_ref[...])
    kk = kk / jnp.maximum(jnp.sqrt(n2), 1e-12)
    r_o[...] = r
    kp_o[...] = k * (1.0 + (alpha - 1.0) * ka_ref[...])
    v_o[...] = v
    kn_o[...] = kk
    kb_o[...] = kk * alpha


def _rwkv_pre(z, sh8, mu, w0, w2p, a0, a2p, g2, k_k, k_a, ones_bd, tl):
    b, l, _ = z.shape
    full = lambda s: pl.BlockSpec(s, lambda i, j: (0,) * len(s))
    out = jax.ShapeDtypeStruct((b, l, RW_DIM), F32)
    ospec = pl.BlockSpec((None, tl, RW_DIM), lambda i, j: (i, j, 0))
    return pl.pallas_call(
        _rwkv_pre_kernel, grid=(b, l // tl),
        in_specs=[pl.BlockSpec((None, tl, RW_SHIFT_W), lambda i, j: (i, j, 0)),
                  pl.BlockSpec((None, SUBLANES, RW_SHIFT_W), lambda i, j: (i, 0, 0)),
                  full((1, RW_SHIFT_W)), full((1, RW_DIM)), full((128, RW_DIM)), full((1, RW_DIM)),
                  full((128, RW_DIM)), full((128, RW_DIM)), full((1, RW_DIM)), full((1, RW_DIM)),
                  full((RW_DIM, RW_DIM))],
        out_specs=[ospec] * 7, out_shape=[out] * 7,
        scratch_shapes=[pltpu.VMEM((SUBLANES, RW_SHIFT_W), F32)],
        compiler_params=_cparams(("parallel", "arbitrary")), name="rwkv_pre")(
            z, sh8, mu, w0, w2p, a0, a2p, g2, k_k, k_a, ones_bd)


def _unit_lower_inverse(ns, row, col, t):
    eye = (row == col).astype(F32)
    lower = row > col
    same8 = (row // 8) == (col // 8)
    n8 = [jnp.where(same8 & lower, n, 0.0) for n in ns]
    p2 = [_mm3(a, a) for a in n8]
    p4 = [_mm3(a, a) for a in p2]
    xs = [_mm3(eye + a, eye + b) for a, b in zip(n8, p2)]
    xs = [_mm3(x, eye + b) for x, b in zip(xs, p4)]
    s = 16
    while s <= t:
        level = ((row // s) == (col // s)) & ((row // (s // 2)) != (col // (s // 2))) & lower
        xc = [_mm3(x, jnp.where(level, n, 0.0)) for x, n in zip(xs, ns)]
        xs = [x + _mm3(y, x) for x, y in zip(xs, xc)]
        s *= 2
    return xs


def _rwkv_chunk_math(r, lw, k, v, kn, kb, h0s, tri):
    t = r.shape[0]
    pairs = range(len(h0s))
    lane_a = _iota((t, LANES), 1) < HEAD_DIM
    row2 = _iota((2 * t, 2 * t), 0)
    col2 = _iota((2 * t, 2 * t), 1)
    rowh = _iota((LANES, LANES), 0)
    colh = _iota((LANES, LANES), 1)
    cum = _mm_sel_l(tri, lw)
    cum_t = cum[t - 1:t]
    g_inv = jnp.exp(-cum)
    g_end = jnp.exp(cum_t - cum)
    a_t = -kn * jnp.exp(cum - lw)
    r_t = r * jnp.exp(cum)
    b_t = kb * g_inv
    k_t = k * g_inv
    b_h = kb * g_end
    k_h = k * g_end
    g_t = jnp.exp(cum_t)

    def pair(x, p):
        return x[:, p * LANES:(p + 1) * LANES]

    def stack(x):
        return [jnp.concatenate([jnp.where(lane_a, pair(x, p), 0.0), jnp.where(lane_a, 0.0, pair(x, p))], axis=0)
                for p in pairs]

    la, lr, lb, lk, vbd = stack(a_t), stack(r_t), stack(b_t), stack(k_t), stack(v)
    same_head = (row2 // t) == (col2 // t)
    strict = same_head & (row2 > col2)
    incl = same_head & (row2 >= col2)
    n = [jnp.where(strict, _mm3(la[p], lb[p], _NT), 0.0) for p in pairs]
    aak = [jnp.where(strict, _mm3(la[p], lk[p], _NT), 0.0) for p in pairs]
    arb = [jnp.where(incl, _mm(lr[p], lb[p], _NT), 0.0) for p in pairs]
    ark = [jnp.where(incl, _mm(lr[p], lk[p], _NT), 0.0) for p in pairs]
    rhs = [_mm3(la[p], h0s[p]) + _mm3(aak[p], vbd[p]) for p in pairs]
    inv = _unit_lower_inverse(n, row2, col2, t)
    u = [_mm3(inv[p], rhs[p]) for p in pairs]
    o = [_mm(lr[p], h0s[p]) + _mm(arb[p], u[p]) + _mm(ark[p], vbd[p]) for p in pairs]
    upd = [_mm3(jnp.concatenate([pair(b_h, p), pair(k_h, p)], axis=0),
                jnp.concatenate([u[p][:t] + u[p][t:], pair(v, p)], axis=0), _TN) for p in pairs]
    same_h = (rowh // HEAD_DIM) == (colh // HEAD_DIM)
    h_new = [jnp.broadcast_to(pair(g_t, p), (LANES, LANES)).T * h0s[p] + jnp.where(same_h, upd[p], 0.0)
             for p in pairs]
    o_wide = jnp.concatenate([o[p][:t] + o[p][t:] for p in pairs], axis=1)
    return o_wide, h_new


def _rwkv_chunk_kernel(r_ref, lw_ref, k_ref, v_ref, kn_ref, kb_ref, h0_ref, tri_ref, o_ref, hout_ref, h_ref):
    @pl.when(pl.program_id(1) == 0)
    def _():
        h_ref[...] = h0_ref[...]

    bb = r_ref.shape[0]
    npair = RW_DIM // LANES
    wide = lambda ref: jnp.concatenate([ref[b] for b in range(bb)], axis=1)
    h0s = [h_ref[b, p] for b in range(bb) for p in range(npair)]
    o_wide, h_new = _rwkv_chunk_math(wide(r_ref), wide(lw_ref), wide(k_ref), wide(v_ref), wide(kn_ref),
                                     wide(kb_ref), h0s, tri_ref[...])
    for b in range(bb):
        o_ref[b] = o_wide[:, b * RW_DIM:(b + 1) * RW_DIM]
        for p in range(npair):
            h_ref[b, p] = h_new[b * npair + p]

    @pl.when(pl.program_id(1) == pl.num_programs(1) - 1)
    def _():
        hout_ref[...] = h_ref[...]


def _rwkv_chunk(r, lw, kp, v, kn, kb, h0, t):
    b, l, _ = r.shape
    bb = 2
    tri = (jnp.arange(t)[:, None] >= jnp.arange(t)[None, :]).astype(BF16)
    xspec = pl.BlockSpec((bb, t, RW_DIM), lambda i, j: (i, j, 0))
    hspec = pl.BlockSpec((bb, 4, LANES, LANES), lambda i, j: (i, 0, 0, 0))
    return pl.pallas_call(
        _rwkv_chunk_kernel, grid=(b // bb, l // t),
        in_specs=[xspec] * 6 + [hspec, pl.BlockSpec((t, t), lambda i, j: (0, 0))],
        out_specs=[xspec, hspec],
        out_shape=[jax.ShapeDtypeStruct((b, l, RW_DIM), F32), jax.ShapeDtypeStruct((b, 4, LANES, LANES), F32)],
        scratch_shapes=[pltpu.VMEM((bb, 4, LANES, LANES), F32)],
        compiler_params=_cparams(("parallel", "arbitrary")), name="rwkv_chunk")(r, lw, kp, v, kn, kb, h0, tri)


def _out0_kernel(x_ref, ya_ref, o_ref, r_ref, kp_ref, v_ref, g_ref, lnw_ref, lnb_ref, rk_ref, ones_ref,
                 wa_ref, wb_ref, out_ref):
    ones = ones_ref[...]
    o = o_ref[...]
    mean = _mm_sel_r(o, ones) * (1.0 / HEAD_DIM)
    d = o - mean
    var = _mm_sel_r(d * d, ones) * (1.0 / HEAD_DIM)
    on = d * lax.rsqrt(var + RW_GN_EPS) * lnw_ref[...] + lnb_ref[...]
    bonus = _mm_sel_r(r_ref[...] * kp_ref[...] * rk_ref[...], ones) * v_ref[...]
    yb = (on + bonus) * g_ref[...]
    out_ref[...] = x_ref[...] + _mm(ya_ref[...], wa_ref[...]) + _mm(yb, wb_ref[...])


def _out0(x, ya, o, r, kp, v, g, lnw, lnb, rk, ones_bd, wa, wb):
    m = x.shape[0]
    tm = min(512, m)
    row = lambda n: pl.BlockSpec((tm, n), lambda i: (i, 0))
    full = lambda s: pl.BlockSpec(s, lambda i: (0, 0))
    return pl.pallas_call(
        _out0_kernel, grid=(m // tm,),
        in_specs=[row(D_MODEL)] + [row(RW_DIM)] * 6 + [full((1, RW_DIM))] * 3 + [full((RW_DIM, RW_DIM))]
                 + [full((SC_DIM, D_MODEL)), full((RW_DIM, D_MODEL))],
        out_specs=row(D_MODEL), out_shape=jax.ShapeDtypeStruct((m, D_MODEL), F32),
        compiler_params=_cparams(("parallel",)), name="out0")(x, ya, o, r, kp, v, g, lnw, lnb, rk, ones_bd, wa, wb)


def _out1_kernel(x_ref, a_ref, b_ref, wa_ref, wb_ref, out_ref):
    out_ref[...] = x_ref[...] + _mm(a_ref[...], wa_ref[...]) + _mm(b_ref[...], wb_ref[...])


def _out1(x, a, b, wa, wb):
    m = x.shape[0]
    tm = min(512, m)
    row = lambda n: pl.BlockSpec((tm, n), lambda i: (i, 0))
    full = lambda s: pl.BlockSpec(s, lambda i: (0, 0))
    return pl.pallas_call(
        _out1_kernel, grid=(m // tm,),
        in_specs=[row(D_MODEL), row(FOX_DIM), row(SSM_DIM), full((FOX_DIM, D_MODEL)), full((SSM_DIM, D_MODEL))],
        out_specs=row(D_MODEL), out_shape=jax.ShapeDtypeStruct((m, D_MODEL), F32),
        compiler_params=_cparams(("parallel",)), name="out1")(x, a, b, wa, wb)


def _logf_kernel(fc_ref, fr_ref, bc_ref, br_ref, tl_ref, tu_ref, lf_o, cc_o, cr_o, carry_c, carry_r):
    @pl.when(pl.program_id(1) == 0)
    def _():
        carry_c[...] = jnp.zeros_like(carry_c)
        carry_r[...] = jnp.zeros_like(carry_r)

    tl = fc_ref.shape[0]
    lf_c = -_softplus(-(fc_ref[...] + bc_ref[...]))
    lf_o[...] = lf_c
    cc = _mm_sel_l(tl_ref[...], lf_c) + carry_c[0:1]
    cc_o[...] = cc
    carry_c[...] = jnp.broadcast_to(cc[tl - 1:tl], carry_c.shape)
    rep = tl // LANES
    lf_r = -_softplus(-(fr_ref[...] + jnp.tile(br_ref[...], (1, rep))))
    cr = _mm_sel_r(lf_r, tu_ref[...]) + jnp.tile(carry_r[...], (1, rep))
    cr_o[...] = cr
    carry_r[...] = jnp.broadcast_to(cr[:, tl - 1:tl], carry_r.shape)


def _logf(fc, fr, bias_c, bias_r, tri_l, tri_u, tl):
    b, l, _ = fc.shape
    cspec = pl.BlockSpec((None, tl, LANES), lambda i, j: (i, j, 0))
    rspec = pl.BlockSpec((None, 16, tl), lambda i, j: (i, 0, j))
    full = lambda s: pl.BlockSpec(s, lambda i, j: (0, 0))
    return pl.pallas_call(
        _logf_kernel, grid=(b, l // tl),
        in_specs=[cspec, rspec, full((1, LANES)), full((16, LANES)), full((tl, tl)), full((tl, tl))],
        out_specs=[cspec, cspec, rspec],
        out_shape=[jax.ShapeDtypeStruct((b, l, LANES), F32), jax.ShapeDtypeStruct((b, l, LANES), F32),
                   jax.ShapeDtypeStruct((b, 16, l), F32)],
        scratch_shapes=[pltpu.VMEM((SUBLANES, LANES), F32), pltpu.VMEM((16, LANES), F32)],
        compiler_params=_cparams(("parallel", "arbitrary")), name="logf")(fc, fr, bias_c, bias_r, tri_l, tri_u)


def _fox_prompt_kernel(q_ref, k_ref, v_ref, cc_ref, cr_ref, o_ref, qm_ref, cq_ref, m_ref, l_ref, acc_ref):
    p = pl.program_id(1)
    qi = pl.program_id(2)
    ki = pl.program_id(3)
    tq = q_ref.shape[0]
    tk = k_ref.shape[0]
    lane = _iota((tq, LANES), 1)

    @pl.when(ki == 0)
    def _():
        q = q_ref[...] * (HEAD_DIM ** -0.5)
        cc = cc_ref[...]
        sel_row = _iota((LANES, LANES), 0)
        for h in range(2):
            mask = (lane < HEAD_DIM) if h == 0 else (lane >= HEAD_DIM)
            qm_ref[h] = jnp.where(mask, q, 0.0).astype(BF16)
            cq_ref[h] = _mm_sel_r(cc, (sel_row == 2 * p + h).astype(BF16))
        m_ref[...] = jnp.full_like(m_ref, NEG_BIG)
        l_ref[...] = jnp.zeros_like(l_ref)
        acc_ref[...] = jnp.zeros_like(acc_ref)

    @pl.when(ki <= qi)
    def _():
        kb = k_ref[...].astype(BF16)
        vb = v_ref[...].astype(BF16)
        rep = tk // LANES
        causal = (qi * tq + _iota((tq, tk), 0)) >= (ki * tk + _iota((tq, tk), 1))
        for h in range(2):
            ck = cr_ref[pl.ds(2 * p + h, 1), :]
            s = lax.dot_general(qm_ref[h], kb, _NT, preferred_element_type=F32)
            s = s + (jnp.tile(cq_ref[h], (1, rep)) - ck)
            s = jnp.where(causal, s, NEG_BIG)
            m_prev = m_ref[h]
            m_new = jnp.maximum(m_prev, jnp.max(s, axis=1, keepdims=True))
            pr = jnp.exp(s - jnp.tile(m_new, (1, rep)))
            corr = jnp.exp(m_prev - m_new)
            l_ref[h] = corr * l_ref[h] + jnp.sum(pr, axis=1, keepdims=True)
            acc_ref[h] = corr * acc_ref[h] + jnp.dot(pr.astype(BF16), vb, preferred_element_type=F32)
            m_ref[h] = m_new

    @pl.when(ki == pl.num_programs(3) - 1)
    def _():
        o_ref[...] = jnp.where(lane < HEAD_DIM, acc_ref[0] / l_ref[0], acc_ref[1] / l_ref[1])


def _fox_prompt(q, k, v, cc, cr, tq):
    b, l, _ = q.shape
    tk = tq
    kv_idx = lambda i, p, qi, ki: (i, jnp.minimum(ki, qi), p)
    return pl.pallas_call(
        _fox_prompt_kernel, grid=(b, FOX_DIM // LANES, l // tq, l // tk),
        in_specs=[pl.BlockSpec((None, tq, LANES), lambda i, p, qi, ki: (i, qi, p)),
                  pl.BlockSpec((None, tk, LANES), kv_idx),
                  pl.BlockSpec((None, tk, LANES), kv_idx),
                  pl.BlockSpec((None, tq, LANES), lambda i, p, qi, ki: (i, qi, 0)),
                  pl.BlockSpec((None, 16, tk), lambda i, p, qi, ki: (i, 0, jnp.minimum(ki, qi)))],
        out_specs=pl.BlockSpec((None, tq, LANES), lambda i, p, qi, ki: (i, qi, p)),
        out_shape=jax.ShapeDtypeStruct((b, l, FOX_DIM), F32),
        scratch_shapes=[pltpu.VMEM((2, tq, LANES), BF16), pltpu.VMEM((2, tq, LANES), F32),
                        pltpu.VMEM((2, tq, LANES), F32), pltpu.VMEM((2, tq, LANES), F32),
                        pltpu.VMEM((2, tq, LANES), F32)],
        compiler_params=_cparams(("parallel", "parallel", "parallel", "arbitrary")),
        name="fox_prompt")(q, k, v, cc, cr)


def _fox_sample_kernel(pt_ref, qbd_ref, cq_ref, kn_ref, vn_ref, cn_ref, *rest, group):
    del pt_ref
    kp_refs, vp_refs, lf_refs = rest[:group], rest[group:2 * group], rest[2 * group:3 * group]
    o_ref, m_ref, l_ref, acc_ref, suf_ref = rest[3 * group:]
    s_id = pl.program_id(1)
    nq = qbd_ref.shape[0]
    rep = nq // FOX_HEADS
    row = _iota((nq, PAGE), 0)
    col = _iota((nq, PAGE), 1)

    def attend(scores, values):
        m_prev = m_ref[...]
        m_new = m_prev
        for s in scores:
            m_new = jnp.maximum(m_new, jnp.max(s, axis=1, keepdims=True))
        corr = jnp.exp(m_prev - m_new)
        l_new = corr * l_ref[...]
        acc = corr[:, 0:1] * acc_ref[...]
        for s, pv in zip(scores, values):
            pr = jnp.exp(s - m_new)
            l_new = l_new + jnp.sum(pr, axis=1, keepdims=True)
            acc = acc + pv(pr.astype(BF16))
        l_ref[...] = l_new
        acc_ref[...] = acc
        m_ref[...] = m_new

    @pl.when(s_id == 0)
    def _():
        m_ref[...] = jnp.full_like(m_ref, NEG_BIG)
        l_ref[...] = jnp.zeros_like(l_ref)
        acc_ref[...] = jnp.zeros_like(acc_ref)
        suf_ref[...] = jnp.zeros_like(suf_ref)
        s = lax.dot_general(qbd_ref[...], kn_ref[...].astype(BF16), _NT, preferred_element_type=F32)
        s = s + (cq_ref[...] - jnp.tile(cn_ref[...], (rep, 1)))
        s = jnp.where(col <= row // FOX_HEADS, s, NEG_BIG)
        vn = vn_ref[...].astype(BF16)
        attend([s], [lambda pr: jnp.dot(pr, vn, preferred_element_type=F32)])

    @pl.when(s_id > 0)
    def _():
        later = (_iota((PAGE, PAGE), 0) > _iota((PAGE, PAGE), 1)).astype(BF16)
        carry = suf_ref[...]
        qbd = qbd_ref[...]
        cq = cq_ref[...]
        scores, values = [], []
        for i in range(group):
            lf = lf_refs[i][...]
            suffix = _mm_sel_r(lf, later) + carry
            carry = jnp.broadcast_to(suffix[:, 0:1] + lf[:, 0:1], carry.shape)
            s = jnp.dot(qbd, kp_refs[i][...].astype(BF16), preferred_element_type=F32)
            scores.append(s + (cq + jnp.tile(suffix, (rep, 1))))
            values.append(lambda pr, i=i: lax.dot_general(pr, vp_refs[i][...].astype(BF16), _NT,
                                                          preferred_element_type=F32))
        suf_ref[...] = carry
        attend(scores, values)

    @pl.when(s_id == pl.num_programs(1) - 1)
    def _():
        lane_head = _iota((nq, FOX_DIM), 1) // HEAD_DIM
        row_head = _iota((nq, FOX_DIM), 0) % FOX_HEADS
        own = jnp.where(lane_head == row_head, acc_ref[...], 0.0)
        fold = (_iota((FOX_DIM, HEAD_DIM), 0) % HEAD_DIM == _iota((FOX_DIM, HEAD_DIM), 1)).astype(BF16)
        o_ref[...] = _mm_sel_r(own, fold) / l_ref[:, 0:HEAD_DIM]


def _fox_sample(page_table, qbd, cq, k_new, v_new, c_new, pool_k, pool_v, pool_lf):
    b, nq, _ = qbd.shape
    n_pages = page_table.shape[1]
    group = min(FOX_PAGE_GROUP, n_pages)
    assert n_pages % group == 0

    def page(slot):
        return lambda i, s, pt: (pt[i, n_pages - 1 - (jnp.maximum(s - 1, 0) * group + slot)], 0, 0)

    per_b = lambda shape: pl.BlockSpec((None,) + shape, lambda i, s, pt: (i, 0, 0))
    slots = range(group)
    grid_spec = pltpu.PrefetchScalarGridSpec(
        num_scalar_prefetch=1, grid=(b, n_pages // group + 1),
        in_specs=[per_b((nq, FOX_DIM)), per_b((nq, LANES)), per_b((PAGE, FOX_DIM)), per_b((PAGE, FOX_DIM)),
                  per_b((FOX_HEADS, LANES))]
                 + [pl.BlockSpec((None, FOX_DIM, PAGE), page(i)) for i in slots]
                 + [pl.BlockSpec((None, FOX_DIM, PAGE), page(i)) for i in slots]
                 + [pl.BlockSpec((None, FOX_HEADS, PAGE), page(i)) for i in slots],
        out_specs=per_b((nq, HEAD_DIM)),
        scratch_shapes=[pltpu.VMEM((nq, LANES), F32), pltpu.VMEM((nq, LANES), F32),
                        pltpu.VMEM((nq, FOX_DIM), F32), pltpu.VMEM((FOX_HEADS, LANES), F32)])
    return pl.pallas_call(
        functools.partial(_fox_sample_kernel, group=group), grid_spec=grid_spec,
        out_shape=jax.ShapeDtypeStruct((b, nq, HEAD_DIM), F32),
        compiler_params=_cparams(("parallel", "arbitrary")), name="fox_sample")(
            page_table, qbd, cq, k_new, v_new, c_new, *([pool_k] * group), *([pool_v] * group),
            *([pool_lf] * group))


def _ssm_kernel(z_ref, xbc_ref, fc_ref, fr_ref, buf_ref, s0_ref, cw_ref, cb_ref, dtb_c_ref, dtb_r_ref,
                a_c_ref, a_r_ref, d_ref, nw_ref, tl_ref, tu_ref, y_ref, sout_ref, carry_ref, s_ref, *, valid):
    @pl.when(pl.program_id(1) == 0)
    def _():
        carry_ref[...] = buf_ref[...]
        s_ref[...] = s0_ref[...]

    raw = xbc_ref[...]
    cs = raw.shape[0]
    carry = carry_ref[...]
    cw = cw_ref[...]
    conv = (cw[0:1] * _shifted(raw, carry, 3) + cw[1:2] * _shifted(raw, carry, 2)
            + cw[2:3] * _shifted(raw, carry, 1) + cw[3:4] * raw)
    carry_ref[...] = raw[cs - SUBLANES:]
    xbc = _silu(conv + cb_ref[...])
    x = xbc[:, :SSM_DIM]

    dt_c = _softplus(fc_ref[...] + dtb_c_ref[...])
    dt_r = _softplus(fr_ref[...] + dtb_r_ref[...])
    if valid < cs:
        dt_c = jnp.where(_iota((cs, LANES), 0) < valid, dt_c, 0.0)
        dt_r = jnp.where(_iota((16, cs), 1) < valid, dt_r, 0.0)
    acum_c = _mm_sel_l(tl_ref[...], dt_c * a_c_ref[...])
    acum_r = _mm_sel_r(dt_r * a_r_ref[...], tu_ref[...])
    w_c = jnp.exp(acum_c[cs - 1:cs] - acum_c) * dt_c

    sel_row = _iota((LANES, LANES), 0)
    sel_lane = _iota((LANES, LANES), 1)
    lane = _iota((cs, LANES), 1)
    causal = _iota((cs, cs), 0) >= _iota((cs, cs), 1)
    y_pairs = []
    for p in range(SSM_HEADS // 2):
        g = (2 * p) // (SSM_HEADS // SSM_GROUPS)
        bm = xbc[:, SSM_DIM + g * SSM_STATE:SSM_DIM + (g + 1) * SSM_STATE]
        cm = xbc[:, SSM_DIM + (SSM_GROUPS + g) * SSM_STATE:SSM_DIM + (SSM_GROUPS + g + 1) * SSM_STATE]
        cb = _mm(cm, bm, _NT)
        xp = x[:, p * LANES:(p + 1) * LANES]
        y_intra = []
        dec_rows = []
        for h in range(2):
            hh = 8 + 2 * p + h
            a_col = _mm_sel_r(acum_c, (sel_row == hh).astype(BF16))
            seg = a_col[:, :cs] - acum_r[hh:hh + 1, :]
            lmat = jnp.exp(jnp.where(causal, seg, NEG_BIG))
            gmat = cb * lmat * dt_r[hh:hh + 1, :]
            y_intra.append(_mm(gmat, xp))
            dec_rows.append(jnp.broadcast_to(jnp.exp(a_col[cs - 1:cs]), (HEAD_DIM, LANES)))
        pair_sel = (sel_row == 8 + 2 * p + sel_lane // HEAD_DIM).astype(BF16)
        ea_pair = jnp.exp(_mm_sel_r(acum_c, pair_sel))
        w_pair = _mm_sel_r(w_c, pair_sel)
        s_pair = s_ref[p]
        y_inter = _mm(cm, s_pair, _NT) * ea_pair
        y_pairs.append(jnp.where(lane < HEAD_DIM, y_intra[0], y_intra[1]) + y_inter)
        s_ref[p] = s_pair * jnp.concatenate(dec_rows, axis=0) + _mm(xp * w_pair, bm, _TN)

    y = jnp.concatenate(y_pairs, axis=1) + d_ref[...] * x
    y = y * _silu(z_ref[...])
    y_ref[...] = _rmsnorm(y, nw_ref[...])

    @pl.when(pl.program_id(1) == pl.num_programs(1) - 1)
    def _():
        sout_ref[...] = s_ref[...]


def _ssm(z, xbc, fc, fr, buf8, s0, cw8, cb, dtb_c, dtb_r, a_c, a_r, d_exp, nw, tri_l, tri_u, valid):
    b, l, _ = z.shape
    cs = LANES
    full = lambda s: pl.BlockSpec(s, lambda i, j: (0,) * len(s))
    sspec = pl.BlockSpec((None, 4, LANES, LANES), lambda i, j: (i, 0, 0, 0))
    return pl.pallas_call(
        functools.partial(_ssm_kernel, valid=valid), grid=(b, l // cs),
        in_specs=[pl.BlockSpec((None, cs, SSM_DIM), lambda i, j: (i, j, 0)),
                  pl.BlockSpec((None, cs, SSM_CONV_DIM), lambda i, j: (i, j, 0)),
                  pl.BlockSpec((None, cs, LANES), lambda i, j: (i, j, 0)),
                  pl.BlockSpec((None, 16, cs), lambda i, j: (i, 0, j)),
                  pl.BlockSpec((None, SUBLANES, SSM_CONV_DIM), lambda i, j: (i, 0, 0)),
                  sspec,
                  full((SUBLANES, SSM_CONV_DIM)), full((1, SSM_CONV_DIM)), full((1, LANES)), full((16, LANES)),
                  full((1, LANES)), full((16, LANES)), full((1, SSM_DIM)), full((1, SSM_DIM)),
                  full((cs, cs)), full((cs, cs))],
        out_specs=[pl.BlockSpec((None, cs, SSM_DIM), lambda i, j: (i, j, 0)), sspec],
        out_shape=[jax.ShapeDtypeStruct((b, l, SSM_DIM), F32), jax.ShapeDtypeStruct((b, 4, LANES, LANES), F32)],
        scratch_shapes=[pltpu.VMEM((SUBLANES, SSM_CONV_DIM), F32), pltpu.VMEM((4, LANES, LANES), F32)],
        compiler_params=_cparams(("parallel", "arbitrary")), name="ssm")(
            z, xbc, fc, fr, buf8, s0, cw8, cb, dtb_c, dtb_r, a_c, a_r, d_exp, nw, tri_l, tri_u)


def _pad_rows(a, rows):
    return jnp.pad(a, ((0, 0), (rows - a.shape[1], 0), (0, 0)))


def _tri(n, seg):
    i = jnp.arange(n)
    same = (i[:, None] // seg) == (i[None, :] // seg)
    lower = (same & (i[:, None] >= i[None, :])).astype(BF16)
    return lower, lower.T


def _lane_vec(v, offset):
    n = v.shape[0]
    col = jnp.zeros((1, LANES), F32).at[0, offset:offset + n].set(v)
    row = jnp.zeros((16, LANES), F32).at[offset:offset + n, :].set(jnp.broadcast_to(v[:, None], (n, LANES)))
    return col, row


def _layer0(h, bsz, l, sc_buf, shift_prev, wkv0, p):
    m = bsz * l
    tl = min(512, l)
    z_sc, z_rw = _in_proj(h, p['g_mix'], p['w_in'], (3 * SC_DIM, RW_SHIFT_W))
    z_sc = z_sc.reshape(bsz, l, 3 * SC_DIM)
    z_rw = z_rw.reshape(bsz, l, RW_SHIFT_W)
    ya, sc_last = _sconv(z_sc, _pad_rows(sc_buf, SUBLANES), p['sc_w8'], tl)
    r, lw, kp, v, kn, kb, g = _rwkv_pre(z_rw, _pad_rows(shift_prev[:, None], SUBLANES), p['mu'], p['w0'], p['w2p'],
                                        p['a0'], p['a2p'], p['g2'], p['k_k'], p['k_a'], p['ones_bd'], tl)
    ht = jnp.swapaxes(wkv0, 2, 3).reshape(bsz, 4, 2, HEAD_DIM, HEAD_DIM)
    eye2 = jnp.eye(2, dtype=F32)
    h_bd = jnp.einsum('bpgji,gk->bpgjki', ht, eye2).reshape(bsz, 4, LANES, LANES)
    lpad = -(-l // RW_CHUNK) * RW_CHUNK
    padl = lambda t: jnp.pad(t, ((0, 0), (0, lpad - l), (0, 0)))
    o, h_fin = _rwkv_chunk(padl(r), padl(lw), padl(kp), padl(v), padl(kn), padl(kb), h_bd, RW_CHUNK)
    o = o[:, :l]
    h_fin = h_fin.reshape(bsz, 4, 2, HEAD_DIM, 2, HEAD_DIM)
    wkv_new = jnp.stack([h_fin[:, :, 0, :, 0], h_fin[:, :, 1, :, 1]], axis=2).reshape(bsz, RW_HEADS, HEAD_DIM, HEAD_DIM)
    wkv_new = jnp.swapaxes(wkv_new, 2, 3)
    flat = lambda t: t.reshape(m, t.shape[-1])
    h = _out0(h, flat(ya), flat(o), flat(r), flat(kp), flat(v), flat(g), p['ln_w'], p['ln_b'], p['r_k'],
              p['ones_bd'], p['wo_a'], p['wo_b'])
    h = _ffn(h, p['g_ffn'], p['w_gate'], p['w_up'], p['w_down'], p['g_ffn'], False)
    return h, sc_last[:, SUBLANES - 2:], z_rw[:, -1], wkv_new


def _layer1(h, bsz, l, past, conv_buf, ssm0, p, g_final):
    m = bsz * l
    q, k, v, zg, xbc, fdt, fdt_t = _in_proj(h, p['g_mix'], p['w_in'],
                                            (FOX_DIM, FOX_DIM, FOX_DIM, SSM_DIM, SSM_CONV_DIM, LANES), p['w_t'])
    fr = jnp.swapaxes(fdt_t.reshape(16, bsz, l), 0, 1)
    if past is None:
        tl = min(512, l)
        tri_l, tri_u = _tri(tl, tl)
        lf, cc, cr = _logf(fdt.reshape(bsz, l, LANES), fr, p['fb_c'], p['fb_r'], tri_l, tri_u, tl)
        yf = _fox_prompt(q.reshape(bsz, l, FOX_DIM), k.reshape(bsz, l, FOX_DIM), v.reshape(bsz, l, FOX_DIM),
                         cc, cr, tl)
        logf_new = lf[..., :FOX_HEADS]
    else:
        page_table, pool_k, pool_v, pool_lf = past
        tri_l, tri_u = _tri(m, l)
        fr_flat = fdt_t.reshape(1, 16, m)
        lf, cc, cr = _logf(fdt.reshape(1, m, LANES), fr_flat, p['fb_c'], p['fb_r'], tri_l, tri_u, m)
        logf_new = lf.reshape(bsz, l, LANES)[..., :FOX_HEADS]
        q4 = q.reshape(bsz, l, FOX_HEADS, HEAD_DIM) * (HEAD_DIM ** -0.5)
        qbd = jnp.einsum('bqhd,hg->bqhgd', q4, jnp.eye(FOX_HEADS, dtype=F32))
        qbd = qbd.reshape(bsz, l * FOX_HEADS, FOX_DIM).astype(BF16)
        c_q = cc.reshape(bsz, l, LANES)[..., :FOX_HEADS].reshape(bsz, l * FOX_HEADS, 1)
        c_q = jnp.broadcast_to(c_q, (bsz, l * FOX_HEADS, LANES))
        c_new = jnp.pad(jnp.swapaxes(cr.reshape(16, bsz, l), 0, 1)[:, :FOX_HEADS], ((0, 0), (0, 0), (0, PAGE - l)))
        pad_new = lambda t: jnp.pad(t.reshape(bsz, l, FOX_DIM), ((0, 0), (0, PAGE - l), (0, 0)))
        yf = _fox_sample(page_table, qbd, c_q, pad_new(k), pad_new(v), c_new, pool_k, pool_v, pool_lf)
        yf = yf.reshape(bsz, l, FOX_DIM)
    ctl, ctu = _tri(SSD_CHUNK, SSD_CHUNK)
    xbc3 = xbc.reshape(bsz, l, SSM_CONV_DIM)
    if l % SSD_CHUNK == 0:
        lpad, valid = l, SSD_CHUNK
    else:
        assert l < SSD_CHUNK
        lpad, valid = SSD_CHUNK, l
    padl = lambda t: jnp.pad(t, ((0, 0), (0, lpad - l), (0, 0)))
    ym, s_fin = _ssm(padl(zg.reshape(bsz, l, SSM_DIM)), padl(xbc3), padl(fdt.reshape(bsz, l, LANES)),
                     jnp.pad(fr, ((0, 0), (0, 0), (0, lpad - l))),
                     _pad_rows(conv_buf, SUBLANES), ssm0.reshape(bsz, 4, LANES, LANES), p['cw8'], p['cb'],
                     p['dtb_c'], p['dtb_r'], p['a_c'], p['a_r'], p['d_exp'], p['nw'], ctl, ctu, valid)
    ym = ym[:, :l]
    h = _out1(h, yf.reshape(m, FOX_DIM), ym.reshape(m, SSM_DIM), p['wo_a'], p['wo_b'])
    h = _ffn(h, p['g_ffn'], p['w_gate'], p['w_up'], p['w_down'], g_final, True)
    shape5 = (bsz, l, FOX_HEADS, HEAD_DIM)
    return (h, k.reshape(shape5), v.reshape(shape5), logf_new, xbc3[:, l - 3:],
            s_fin.reshape(bsz, SSM_HEADS, HEAD_DIM, SSM_STATE))


def kernel(x_prompt, x_sample, state_sc, state_shift, state_wkv, cache_k, cache_v, cache_logf, state_ssm_conv, state_ssm, page_table, norm_mix, norm_ffn, norm_final, w_in0, sc_conv_w, rw_mu, rw_w0, rw_w2, rw_a0, rw_a2, rw_g2, rw_k_k, rw_k_a, rw_r_k, rw_ln_w, rw_ln_b, w_out0, w_in1, fox_f_bias, ssm_conv_w, ssm_conv_b, ssm_dt_bias, ssm_a_log, ssm_d, ssm_norm_w, w_out1, w_gate, w_up, w_down):
    bp, lp, _ = x_prompt.shape
    bs, ls, _ = x_sample.shape
    row = lambda v: v.reshape(1, -1).astype(F32)
    head_ids = jnp.arange(RW_DIM) // HEAD_DIM
    ones_bd = (head_ids[:, None] == head_ids[None, :]).astype(BF16)
    zpad = jnp.zeros((64, RW_DIM), F32)

    p0 = dict(
        g_mix=row(norm_mix[0]), g_ffn=row(norm_ffn[0]), w_in=w_in0[0].astype(BF16),
        sc_w8=jnp.pad(sc_conv_w[0], ((0, SUBLANES - sc_conv_w.shape[1]), (0, 0))),
        mu=row(rw_mu[0]), w0=row(rw_w0[0]), a0=row(rw_a0[0]),
        w2p=jnp.concatenate([rw_w2[0], zpad], axis=0).astype(BF16),
        a2p=jnp.concatenate([zpad, rw_a2[0]], axis=0).astype(BF16),
        g2=rw_g2[0].astype(BF16), k_k=row(rw_k_k[0]), k_a=row(rw_k_a[0]), r_k=row(rw_r_k[0]),
        ln_w=row(rw_ln_w[0]), ln_b=row(rw_ln_b[0]), ones_bd=ones_bd,
        wo_a=w_out0[0, :SC_DIM].astype(BF16), wo_b=w_out0[0, SC_DIM:].astype(BF16),
        w_gate=w_gate[0].astype(BF16), w_up=w_up[0].astype(BF16), w_down=w_down[0].astype(BF16))

    w1 = w_in1[0]
    c0 = 3 * FOX_DIM
    f_cols = w1[:, c0:c0 + FOX_HEADS]
    z_cols = w1[:, c0 + FOX_HEADS:c0 + FOX_HEADS + SSM_DIM]
    xbc_cols = w1[:, c0 + FOX_HEADS + SSM_DIM:c0 + FOX_HEADS + SSM_DIM + SSM_CONV_DIM]
    dt_cols = w1[:, c0 + FOX_HEADS + SSM_DIM + SSM_CONV_DIM:]
    fdt_cols = jnp.concatenate([f_cols, dt_cols, jnp.zeros((D_MODEL, LANES - 16), F32)], axis=1)
    w1r = jnp.concatenate([w1[:, :c0], z_cols, xbc_cols, fdt_cols], axis=1).astype(BF16)
    fb_c, fb_r = _lane_vec(fox_f_bias[0], 0)
    dtb_c, dtb_r = _lane_vec(ssm_dt_bias[0], 8)
    a_c, a_r = _lane_vec(-jnp.exp(ssm_a_log[0].astype(F32)), 8)
    p1 = dict(
        g_mix=row(norm_mix[1]), g_ffn=row(norm_ffn[1]), w_in=w1r, w_t=fdt_cols[:, :16].T.astype(BF16),
        fb_c=fb_c, fb_r=fb_r, dtb_c=dtb_c, dtb_r=dtb_r, a_c=a_c, a_r=a_r,
        cw8=jnp.pad(ssm_conv_w[0], ((0, SUBLANES - ssm_conv_w.shape[1]), (0, 0))), cb=row(ssm_conv_b[0]),
        d_exp=row(jnp.repeat(ssm_d[0], HEAD_DIM)), nw=row(ssm_norm_w[0]),
        wo_a=w_out1[0, :FOX_DIM].astype(BF16), wo_b=w_out1[0, FOX_DIM:].astype(BF16),
        w_gate=w_gate[1].astype(BF16), w_up=w_up[1].astype(BF16), w_down=w_down[1].astype(BF16))
    g_final = row(norm_final)

    n_pool = cache_k.shape[1]
    page_t = lambda c: jnp.transpose(c[0], (0, 2, 3, 1)).reshape(n_pool, FOX_DIM, PAGE)
    past = (page_table, page_t(cache_k), page_t(cache_v), jnp.swapaxes(cache_logf[0], 1, 2))

    hp = x_prompt.reshape(bp * lp, D_MODEL)
    hs = x_sample.reshape(bs * ls, D_MODEL)
    hp, sc_p, sh_p, wkv_p = _layer0(hp, bp, lp, jnp.zeros((bp, 2, SC_DIM), F32), jnp.zeros((bp, RW_SHIFT_W), F32),
                                    jnp.zeros((bp, RW_HEADS, HEAD_DIM, HEAD_DIM), F32), p0)
    hs, sc_s, sh_s, wkv_s = _layer0(hs, bs, ls, state_sc[0], state_shift[0], state_wkv[0], p0)
    yp, k_p, v_p, lf_p, cv_p, ssm_p = _layer1(hp, bp, lp, None, jnp.zeros((bp, 3, SSM_CONV_DIM), F32),
                                              jnp.zeros((bp, SSM_HEADS, HEAD_DIM, SSM_STATE), F32), p1, g_final)
    ys, k_s, v_s, lf_s, cv_s, ssm_s = _layer1(hs, bs, ls, past, state_ssm_conv[0], state_ssm[0], p1, g_final)
    one = lambda t: t[None]
    return (yp.reshape(bp, lp, D_MODEL), ys.reshape(bs, ls, D_MODEL),
            one(sc_p), one(sc_s), one(sh_p), one(sh_s), one(wkv_p), one(wkv_s),
            one(k_p), one(k_s), one(v_p), one(v_s), one(lf_p), one(lf_s), one(cv_p), one(cv_s),
            one(ssm_p), one(ssm_s))
```

```python
import functools

import jax
import jax.numpy as jnp
from jax import lax
from jax.experimental import pallas as pl
from jax.experimental.pallas import tpu as pltpu

F32 = jnp.float32
BF16 = jnp.bfloat16

D_MODEL = 1024
HEAD_DIM = 64
NORM_EPS = 1e-6
SC_DIM = 512
RW_DIM = 512
RW_HEADS = 8
RW_SHIFT_W = 3 * RW_DIM + 64 + 64 + 128
RW_GN_EPS = 64e-5
FOX_DIM = 512
FOX_HEADS = 8
SSM_DIM = 512
SSM_HEADS = 8
SSM_GROUPS = 2
SSM_STATE = 128
SSM_CONV_DIM = SSM_DIM + 2 * SSM_GROUPS * SSM_STATE
FFN_HIDDEN = 2816
PAGE = 128
SSD_CHUNK = 128
FOX_PAGE_GROUP = 8
RW_CHUNK = 64

LANES = 128
SUBLANES = 8
VMEM_LIMIT = 56 * 1024 * 1024

NEG_BIG = -1e30
LOG2E = 1.4426950408889634


def _cparams(sem):
    return pltpu.CompilerParams(dimension_semantics=sem, vmem_limit_bytes=VMEM_LIMIT)


_NN = (((1,), (0,)), ((), ()))
_NT = (((1,), (1,)), ((), ()))
_TN = (((0,), (0,)), ((), ()))


def _mm(a, b, dims=_NN):
    return lax.dot_general(a.astype(BF16), b.astype(BF16), dims, preferred_element_type=F32)


def _split2(x):
    hi = x.astype(BF16)
    lo = (x - hi.astype(F32)).astype(BF16)
    return hi, lo


def _split3(x):
    hi = x.astype(BF16)
    r1 = x - hi.astype(F32)
    mid = r1.astype(BF16)
    lo = (r1 - mid.astype(F32)).astype(BF16)
    return hi, mid, lo


def _mm3(a, b, dims=_NN):
    ah, al = _split2(a)
    bh, bl = _split2(b)
    d = functools.partial(lax.dot_general, dimension_numbers=dims, preferred_element_type=F32)
    return d(ah, bh) + (d(al, bh) + d(ah, bl))


def _mm_sel_r(x, e, dims=_NN):
    h, m, l = _split3(x)
    d = functools.partial(lax.dot_general, dimension_numbers=dims, preferred_element_type=F32)
    return d(h, e) + (d(m, e) + d(l, e))


def _mm_sel_l(e, x, dims=_NN):
    h, m, l = _split3(x)
    d = functools.partial(lax.dot_general, dimension_numbers=dims, preferred_element_type=F32)
    return d(e, h) + (d(e, m) + d(e, l))


def _rmsnorm(x, g):
    return x * lax.rsqrt(jnp.mean(x * x, axis=-1, keepdims=True) + NORM_EPS) * g


def _sigmoid(x):
    return 1.0 / (1.0 + jnp.exp(-x))


def _silu(x):
    return x * _sigmoid(x)


def _softplus(x):
    return jnp.maximum(x, 0.0) + jnp.log(1.0 + jnp.exp(-jnp.abs(x)))


def _iota(shape, dim):
    return lax.broadcasted_iota(jnp.int32, shape, dim)


def _shifted(u, carry8, d):
    tl = u.shape[0]
    s = pltpu.roll(u, d, 0)
    c = pltpu.roll(carry8, d, 0)
    row = _iota((SUBLANES, u.shape[1]), 0)
    head = jnp.where(row < d, c, s[:SUBLANES])
    if tl == SUBLANES:
        return head
    return jnp.concatenate([head, s[SUBLANES:]], axis=0)


def _in_proj_kernel(*refs, splits, has_t):
    x_ref, g_ref, w_ref = refs[:3]
    pos = 3
    if has_t:
        wt_ref = refs[pos]
        pos += 1
    outs = refs[pos:pos + len(splits)]
    nb = _rmsnorm(x_ref[...], g_ref[...]).astype(BF16)
    off = 0
    for o_ref, width in zip(outs, splits):
        o_ref[...] = jnp.dot(nb, w_ref[:, off:off + width], preferred_element_type=F32)
        off += width
    if has_t:
        refs[pos + len(splits)][...] = lax.dot_general(wt_ref[...], nb, _NT, preferred_element_type=F32)


def _in_proj(x, g, w, splits, wt=None):
    m = x.shape[0]
    tm = min(512, m)
    n = sum(splits)
    in_specs = [pl.BlockSpec((tm, D_MODEL), lambda i: (i, 0)),
                pl.BlockSpec((1, D_MODEL), lambda i: (0, 0)),
                pl.BlockSpec((D_MODEL, n), lambda i: (0, 0))]
    args = [x, g, w]
    out_shape = [jax.ShapeDtypeStruct((m, s), F32) for s in splits]
    out_specs = [pl.BlockSpec((tm, s), lambda i: (i, 0)) for s in splits]
    if wt is not None:
        in_specs.append(pl.BlockSpec((16, D_MODEL), lambda i: (0, 0)))
        args.append(wt)
        out_shape.append(jax.ShapeDtypeStruct((16, m), F32))
        out_specs.append(pl.BlockSpec((16, tm), lambda i: (0, i)))
    return pl.pallas_call(
        functools.partial(_in_proj_kernel, splits=tuple(splits), has_t=wt is not None),
        grid=(m // tm,), in_specs=in_specs, out_specs=out_specs, out_shape=out_shape,
        compiler_params=_cparams(("parallel",)), name="in_proj")(*args)


def _ffn_kernel(x_ref, g_ref, wg_ref, wu_ref, wd_ref, gf_ref, o_ref, xn_ref, acc_ref, *, final_norm):
    j = pl.program_id(1)

    @pl.when(j == 0)
    def _():
        x = x_ref[...]
        xn_ref[...] = _rmsnorm(x, g_ref[...]).astype(BF16)
        acc_ref[...] = x

    xn = xn_ref[...]
    gate = jnp.dot(xn, wg_ref[...], preferred_element_type=F32)
    up = jnp.dot(xn, wu_ref[...], preferred_element_type=F32)
    act = (_silu(gate) * up).astype(BF16)
    acc_ref[...] += jnp.dot(act, wd_ref[...], preferred_element_type=F32)

    @pl.when(j == pl.num_programs(1) - 1)
    def _():
        y = acc_ref[...]
        if final_norm:
            y = _rmsnorm(y, gf_ref[...])
        o_ref[...] = y


def _ffn(x, g, wg, wu, wd, gfin, final_norm):
    m = x.shape[0]
    tm = min(512, m)
    th = FFN_HIDDEN // 2
    return pl.pallas_call(
        functools.partial(_ffn_kernel, final_norm=final_norm),
        grid=(m // tm, FFN_HIDDEN // th),
        in_specs=[pl.BlockSpec((tm, D_MODEL), lambda i, j: (i, 0)),
                  pl.BlockSpec((1, D_MODEL), lambda i, j: (0, 0)),
                  pl.BlockSpec((D_MODEL, th), lambda i, j: (0, j)),
                  pl.BlockSpec((D_MODEL, th), lambda i, j: (0, j)),
                  pl.BlockSpec((th, D_MODEL), lambda i, j: (j, 0)),
                  pl.BlockSpec((1, D_MODEL), lambda i, j: (0, 0))],
        out_specs=pl.BlockSpec((tm, D_MODEL), lambda i, j: (i, 0)),
        out_shape=jax.ShapeDtypeStruct((m, D_MODEL), F32),
        scratch_shapes=[pltpu.VMEM((tm, D_MODEL), BF16), pltpu.VMEM((tm, D_MODEL), F32)],
        compiler_params=_cparams(("parallel", "arbitrary")), name="ffn")(x, g, wg, wu, wd, gfin)


def _sconv_kernel(z_ref, buf_ref, w_ref, y_ref, last_ref, carry_ref):
    @pl.when(pl.program_id(1) == 0)
    def _():
        carry_ref[...] = buf_ref[...]

    z = z_ref[...]
    tl = z.shape[0]
    gb, gc, h = z[:, :SC_DIM], z[:, SC_DIM:2 * SC_DIM], z[:, 2 * SC_DIM:]
    u = gc * h
    carry = carry_ref[...]
    w = w_ref[...]
    y = w[0:1] * _shifted(u, carry, 2) + w[1:2] * _shifted(u, carry, 1) + w[2:3] * u
    y_ref[...] = gb * y
    tail = u[tl - SUBLANES:]
    carry_ref[...] = tail
    last_ref[...] = tail


def _sconv(z, buf8, w8, tl):
    b, l, _ = z.shape
    return pl.pallas_call(
        _sconv_kernel, grid=(b, l // tl),
        in_specs=[pl.BlockSpec((None, tl, 3 * SC_DIM), lambda i, j: (i, j, 0)),
                  pl.BlockSpec((None, SUBLANES, SC_DIM), lambda i, j: (i, 0, 0)),
                  pl.BlockSpec((SUBLANES, SC_DIM), lambda i, j: (0, 0))],
        out_specs=[pl.BlockSpec((None, tl, SC_DIM), lambda i, j: (i, j, 0)),
                   pl.BlockSpec((None, SUBLANES, SC_DIM), lambda i, j: (i, 0, 0))],
        out_shape=[jax.ShapeDtypeStruct((b, l, SC_DIM), F32),
                   jax.ShapeDtypeStruct((b, SUBLANES, SC_DIM), F32)],
        scratch_shapes=[pltpu.VMEM((SUBLANES, SC_DIM), F32)],
        compiler_params=_cparams(("parallel", "arbitrary")), name="sconv")(z, buf8, w8)


def _rwkv_pre_kernel(z_ref, sh_ref, mu_ref, w0_ref, w2_ref, a0_ref, a2_ref, g2_ref, kk_ref, ka_ref, ones_ref,
                     r_o, lw_o, kp_o, v_o, kn_o, kb_o, g_o, carry_ref):
    @pl.when(pl.program_id(1) == 0)
    def _():
        carry_ref[...] = sh_ref[...]

    z = z_ref[...]
    tl = z.shape[0]
    zs = _shifted(z, carry_ref[...], 1)
    carry_ref[...] = z[tl - SUBLANES:]
    zx = z + mu_ref[...] * (zs - z)
    r = zx[:, :RW_DIM]
    k = zx[:, RW_DIM:2 * RW_DIM]
    v = zx[:, 2 * RW_DIM:3 * RW_DIM]
    wa = zx[:, 3 * RW_DIM:3 * RW_DIM + 128]
    gd = zx[:, 3 * RW_DIM + 128:]
    w_log = -_softplus(-(w0_ref[...] + _mm(jnp.tanh(wa), w2_ref[...]))) - 0.5
    lw_o[...] = -jnp.exp(w_log)
    alpha = _sigmoid(a0_ref[...] + _mm(wa, a2_ref[...]))
    g_o[...] = _mm(_sigmoid(gd), g2_ref[...])
    kk = k * kk_ref[...]
    n2 = _mm_sel_r(kk * kk, ones_ref[...])
    kk = kk / jnp.maximum(jnp.sqrt(n2), 1e-12)
    r_o[...] = r
    kp_o[...] = k * (1.0 + (alpha - 1.0) * ka_ref[...])
    v_o[...] = v
    kn_o[...] = kk
    kb_o[...] = kk * alpha


def _rwkv_pre(z, sh8, mu, w0, w2p, a0, a2p, g2, k_k, k_a, ones_bd, tl):
    b, l, _ = z.shape
    full = lambda s: pl.BlockSpec(s, lambda i, j: (0,) * len(s))
    out = jax.ShapeDtypeStruct((b, l, RW_DIM), F32)
    ospec = pl.BlockSpec((None, tl, RW_DIM), lambda i, j: (i, j, 0))
    return pl.pallas_call(
        _rwkv_pre_kernel, grid=(b, l // tl),
        in_specs=[pl.BlockSpec((None, tl, RW_SHIFT_W), lambda i, j: (i, j, 0)),
                  pl.BlockSpec((None, SUBLANES, RW_SHIFT_W), lambda i, j: (i, 0, 0)),
                  full((1, RW_SHIFT_W)), full((1, RW_DIM)), full((128, RW_DIM)), full((1, RW_DIM)),
                  full((128, RW_DIM)), full((128, RW_DIM)), full((1, RW_DIM)), full((1, RW_DIM)),
                  full((RW_DIM, RW_DIM))],
        out_specs=[ospec] * 7, out_shape=[out] * 7,
        scratch_shapes=[pltpu.VMEM((SUBLANES, RW_SHIFT_W), F32)],
        compiler_params=_cparams(("parallel", "arbitrary")), name="rwkv_pre")(
            z, sh8, mu, w0, w2p, a0, a2p, g2, k_k, k_a, ones_bd)


def _unit_lower_inverse(ns, row, col, t):
    eye = (row == col).astype(F32)
    lower = row > col
    same8 = (row // 8) == (col // 8)
    n8 = [jnp.where(same8 & lower, n, 0.0) for n in ns]
    p2 = [_mm3(a, a) for a in n8]
    p4 = [_mm3(a, a) for a in p2]
    xs = [_mm3(eye + a, eye + b) for a, b in zip(n8, p2)]
    xs = [_mm3(x, eye + b) for x, b in zip(xs, p4)]
    s = 16
    while s <= t:
        level = ((row // s) == (col // s)) & ((row // (s // 2)) != (col // (s // 2))) & lower
        xc = [_mm3(x, jnp.where(level, n, 0.0)) for x, n in zip(xs, ns)]
        xs = [x + _mm3(y, x) for x, y in zip(xs, xc)]
        s *= 2
    return xs


def _rwkv_chunk_math(r, lw, k, v, kn, kb, h0s, tri):
    t = r.shape[0]
    pairs = range(len(h0s))
    lane_a = _iota((t, LANES), 1) < HEAD_DIM
    row2 = _iota((2 * t, 2 * t), 0)
    col2 = _iota((2 * t, 2 * t), 1)
    rowh = _iota((LANES, LANES), 0)
    colh = _iota((LANES, LANES), 1)
    cum = _mm_sel_l(tri, lw)
    cum_t = cum[t - 1:t]
    g_inv = jnp.exp(-cum)
    g_end = jnp.exp(cum_t - cum)
    a_t = -kn * jnp.exp(cum - lw)
    r_t = r * jnp.exp(cum)
    b_t = kb * g_inv
    k_t = k * g_inv
    b_h = kb * g_end
    k_h = k * g_end
    g_t = jnp.exp(cum_t)

    def pair(x, p):
        return x[:, p * LANES:(p + 1) * LANES]

    def stack(x):
        return [jnp.concatenate([jnp.where(lane_a, pair(x, p), 0.0), jnp.where(lane_a, 0.0, pair(x, p))], axis=0)
                for p in pairs]

    la, lr, lb, lk, vbd = stack(a_t), stack(r_t), stack(b_t), stack(k_t), stack(v)
    same_head = (row2 // t) == (col2 // t)
    strict = same_head & (row2 > col2)
    incl = same_head & (row2 >= col2)
    n = [jnp.where(strict, _mm3(la[p], lb[p], _NT), 0.0) for p in pairs]
    aak = [jnp.where(strict, _mm3(la[p], lk[p], _NT), 0.0) for p in pairs]
    arb = [jnp.where(incl, _mm(lr[p], lb[p], _NT), 0.0) for p in pairs]
    ark = [jnp.where(incl, _mm(lr[p], lk[p], _NT), 0.0) for p in pairs]
    rhs = [_mm3(la[p], h0s[p]) + _mm3(aak[p], vbd[p]) for p in pairs]
    inv = _unit_lower_inverse(n, row2, col2, t)
    u = [_mm3(inv[p], rhs[p]) for p in pairs]
    o = [_mm(lr[p], h0s[p]) + _mm(arb[p], u[p]) + _mm(ark[p], vbd[p]) for p in pairs]
    upd = [_mm3(jnp.concatenate([pair(b_h, p), pair(k_h, p)], axis=0),
                jnp.concatenate([u[p][:t] + u[p][t:], pair(v, p)], axis=0), _TN) for p in pairs]
    same_h = (rowh // HEAD_DIM) == (colh // HEAD_DIM)
    h_new = [jnp.broadcast_to(pair(g_t, p), (LANES, LANES)).T * h0s[p] + jnp.where(same_h, upd[p], 0.0)
             for p in pairs]
    o_wide = jnp.concatenate([o[p][:t] + o[p][t:] for p in pairs], axis=1)
    return o_wide, h_new


def _rwkv_chunk_kernel(r_ref, lw_ref, k_ref, v_ref, kn_ref, kb_ref, h0_ref, tri_ref, o_ref, hout_ref, h_ref):
    @pl.when(pl.program_id(1) == 0)
    def _():
        h_ref[...] = h0_ref[...]

    bb = r_ref.shape[0]
    npair = RW_DIM // LANES
    wide = lambda ref: jnp.concatenate([ref[b] for b in range(bb)], axis=1)
    h0s = [h_ref[b, p] for b in range(bb) for p in range(npair)]
    o_wide, h_new = _rwkv_chunk_math(wide(r_ref), wide(lw_ref), wide(k_ref), wide(v_ref), wide(kn_ref),
                                     wide(kb_ref), h0s, tri_ref[...])
    for b in range(bb):
        o_ref[b] = o_wide[:, b * RW_DIM:(b + 1) * RW_DIM]
        for p in range(npair):
            h_ref[b, p] = h_new[b * npair + p]
            hout_ref[b, p] = h_new[b * npair + p]


def _rwkv_chunk(r, lw, kp, v, kn, kb, h0, t):
    b, l, _ = r.shape
    bb = 2
    tri = (jnp.arange(t)[:, None] >= jnp.arange(t)[None, :]).astype(BF16)
    xspec = pl.BlockSpec((bb, t, RW_DIM), lambda i, j: (i, j, 0))
    hspec = pl.BlockSpec((bb, 4, LANES, LANES), lambda i, j: (i, 0, 0, 0))
    return pl.pallas_call(
        _rwkv_chunk_kernel, grid=(b // bb, l // t),
        in_specs=[xspec] * 6 + [hspec, pl.BlockSpec((t, t), lambda i, j: (0, 0))],
        out_specs=[xspec, hspec],
        out_shape=[jax.ShapeDtypeStruct((b, l, RW_DIM), F32), jax.ShapeDtypeStruct((b, 4, LANES, LANES), F32)],
        scratch_shapes=[pltpu.VMEM((bb, 4, LANES, LANES), F32)],
        compiler_params=_cparams(("parallel", "arbitrary")), name="rwkv_chunk")(r, lw, kp, v, kn, kb, h0, tri)


def _out0_kernel(x_ref, ya_ref, o_ref, r_ref, kp_ref, v_ref, g_ref, lnw_ref, lnb_ref, rk_ref, ones_ref,
                 wa_ref, wb_ref, out_ref):
    ones = ones_ref[...]
    o = o_ref[...]
    mean = _mm_sel_r(o, ones) * (1.0 / HEAD_DIM)
    d = o - mean
    var = _mm_sel_r(d * d, ones) * (1.0 / HEAD_DIM)
    on = d * lax.rsqrt(var + RW_GN_EPS) * lnw_ref[...] + lnb_ref[...]
    bonus = _mm_sel_r(r_ref[...] * kp_ref[...] * rk_ref[...], ones) * v_ref[...]
    yb = (on + bonus) * g_ref[...]
    out_ref[...] = x_ref[...] + _mm(ya_ref[...], wa_ref[...]) + _mm(yb, wb_ref[...])


def _out0(x, ya, o, r, kp, v, g, lnw, lnb, rk, ones_bd, wa, wb):
    m = x.shape[0]
    tm = min(512, m)
    row = lambda n: pl.BlockSpec((tm, n), lambda i: (i, 0))
    full = lambda s: pl.BlockSpec(s, lambda i: (0, 0))
    return pl.pallas_call(
        _out0_kernel, grid=(m // tm,),
        in_specs=[row(D_MODEL)] + [row(RW_DIM)] * 6 + [full((1, RW_DIM))] * 3 + [full((RW_DIM, RW_DIM))]
                 + [full((SC_DIM, D_MODEL)), full((RW_DIM, D_MODEL))],
        out_specs=row(D_MODEL), out_shape=jax.ShapeDtypeStruct((m, D_MODEL), F32),
        compiler_params=_cparams(("parallel",)), name="out0")(x, ya, o, r, kp, v, g, lnw, lnb, rk, ones_bd, wa, wb)


def _out1_kernel(x_ref, a_ref, b_ref, wa_ref, wb_ref, out_ref):
    out_ref[...] = x_ref[...] + _mm(a_ref[...], wa_ref[...]) + _mm(b_ref[...], wb_ref[...])


def _out1(x, a, b, wa, wb):
    m = x.shape[0]
    tm = min(512, m)
    row = lambda n: pl.BlockSpec((tm, n), lambda i: (i, 0))
    full = lambda s: pl.BlockSpec(s, lambda i: (0, 0))
    return pl.pallas_call(
        _out1_kernel, grid=(m // tm,),
        in_specs=[row(D_MODEL), row(FOX_DIM), row(SSM_DIM), full((FOX_DIM, D_MODEL)), full((SSM_DIM, D_MODEL))],
        out_specs=row(D_MODEL), out_shape=jax.ShapeDtypeStruct((m, D_MODEL), F32),
        compiler_params=_cparams(("parallel",)), name="out1")(x, a, b, wa, wb)


def _logf_kernel(fc_ref, fr_ref, bc_ref, br_ref, tl_ref, tu_ref, lf_o, cc_o, cr_o, carry_c, carry_r):
    @pl.when(pl.program_id(1) == 0)
    def _():
        carry_c[...] = jnp.zeros_like(carry_c)
        carry_r[...] = jnp.zeros_like(carry_r)

    tl = fc_ref.shape[0]
    lf_c = -_softplus(-(fc_ref[...] + bc_ref[...]))
    lf_o[...] = lf_c
    cc = _mm_sel_l(tl_ref[...], lf_c) + carry_c[0:1]
    cc_o[...] = cc
    carry_c[...] = jnp.broadcast_to(cc[tl - 1:tl], carry_c.shape)
    rep = tl // LANES
    lf_r = -_softplus(-(fr_ref[...] + jnp.tile(br_ref[...], (1, rep))))
    cr = _mm_sel_r(lf_r, tu_ref[...]) + jnp.tile(carry_r[...], (1, rep))
    cr_o[...] = cr
    carry_r[...] = jnp.broadcast_to(cr[:, tl - 1:tl], carry_r.shape)


def _logf(fc, fr, bias_c, bias_r, tri_l, tri_u, tl):
    b, l, _ = fc.shape
    cspec = pl.BlockSpec((None, tl, LANES), lambda i, j: (i, j, 0))
    rspec = pl.BlockSpec((None, 16, tl), lambda i, j: (i, 0, j))
    full = lambda s: pl.BlockSpec(s, lambda i, j: (0, 0))
    return pl.pallas_call(
        _logf_kernel, grid=(b, l // tl),
        in_specs=[cspec, rspec, full((1, LANES)), full((16, LANES)), full((tl, tl)), full((tl, tl))],
        out_specs=[cspec, cspec, rspec],
        out_shape=[jax.ShapeDtypeStruct((b, l, LANES), F32), jax.ShapeDtypeStruct((b, l, LANES), F32),
                   jax.ShapeDtypeStruct((b, 16, l), F32)],
        scratch_shapes=[pltpu.VMEM((SUBLANES, LANES), F32), pltpu.VMEM((16, LANES), F32)],
        compiler_params=_cparams(("parallel", "arbitrary")), name="logf")(fc, fr, bias_c, bias_r, tri_l, tri_u)


def _fox_aug_kernel(q_ref, k_ref, v_ref, cc_ref, wq_ref, wk_ref, oq_ref, ok_ref, qa_o, ka_o, vb_o):
    c_hi, c_mid, c_lo = _split3(cc_ref[...] * LOG2E)
    q = q_ref[...] * (LOG2E * HEAD_DIM ** -0.5)
    k = k_ref[...]
    for p in range(FOX_DIM // LANES):
        sl = slice(p * LANES, (p + 1) * LANES)
        xq = jnp.concatenate([q[:, sl].astype(BF16), c_hi, c_mid, c_lo], axis=1)
        xk = jnp.concatenate([k[:, sl].astype(BF16), c_hi, c_mid, c_lo], axis=1)
        for h in range(2):
            hh = 2 * p + h
            qa_o[hh] = (jnp.dot(xq, wq_ref[hh], preferred_element_type=F32) + oq_ref[...]).astype(BF16)
            ka_o[hh] = (jnp.dot(xk, wk_ref[hh], preferred_element_type=F32) + ok_ref[...]).astype(BF16)
    vb_o[...] = v_ref[...].astype(BF16)


def _fox_aug(q, k, v, cc, tl):
    b, l, _ = q.shape
    head = jnp.arange(FOX_HEADS)
    src = jnp.arange(4 * LANES)
    dst = jnp.arange(LANES)
    vec = (src[None, :, None] < LANES) & (src[None, :, None] - (head[:, None, None] % 2) * HEAD_DIM == dst[None, None, :]) \
        & (dst[None, None, :] < HEAD_DIM)
    part = (src[None, :, None] - LANES) // LANES
    is_c = (src[None, :, None] >= LANES) & ((src[None, :, None] % LANES) == head[:, None, None])
    wq = (vec | (is_c & (dst[None, None, :] == HEAD_DIM + part))).astype(BF16)
    wk = vec.astype(BF16) - (is_c & (dst[None, None, :] == HEAD_DIM + 3 + part)).astype(BF16)
    ones_q = ((dst >= HEAD_DIM + 3) & (dst < HEAD_DIM + 6)).astype(F32)[None]
    ones_k = ((dst >= HEAD_DIM) & (dst < HEAD_DIM + 3)).astype(F32)[None]
    xspec = pl.BlockSpec((None, tl, FOX_DIM), lambda i, j: (i, j, 0))
    aspec = pl.BlockSpec((None, FOX_HEADS, tl, LANES), lambda i, j: (i, 0, j, 0))
    full = lambda s: pl.BlockSpec(s, lambda i, j: (0,) * len(s))
    aug = jax.ShapeDtypeStruct((b, FOX_HEADS, l, LANES), BF16)
    return pl.pallas_call(
        _fox_aug_kernel, grid=(b, l // tl),
        in_specs=[xspec, xspec, xspec, pl.BlockSpec((None, tl, LANES), lambda i, j: (i, j, 0)),
                  full((FOX_HEADS, 4 * LANES, LANES)), full((FOX_HEADS, 4 * LANES, LANES)),
                  full((1, LANES)), full((1, LANES))],
        out_specs=[aspec, aspec, xspec],
        out_shape=[aug, aug, jax.ShapeDtypeStruct((b, l, FOX_DIM), BF16)],
        compiler_params=_cparams(("parallel", "parallel")), name="fox_aug")(q, k, v, cc, wq, wk, ones_q, ones_k)


def _fox_prompt_kernel(qi_ref, ki_ref, qa_ref, ka_ref, v_ref, o_ref, m_ref, l_ref, acc_ref):
    step = pl.program_id(2)
    qi = qi_ref[step]
    ki = ki_ref[step]
    tq = qa_ref.shape[1]
    tk = ka_ref.shape[1]
    rep = tk // LANES

    @pl.when(ki == 0)
    def _():
        m_ref[...] = jnp.full_like(m_ref, NEG_BIG)
        l_ref[...] = jnp.zeros_like(l_ref)
        acc_ref[...] = jnp.zeros_like(acc_ref)

    def update(diagonal):
        vb = v_ref[...]
        for h in range(2):
            s = lax.dot_general(qa_ref[h], ka_ref[h], _NT, preferred_element_type=F32)
            if diagonal:
                s = jnp.where(_iota((tq, tk), 0) >= _iota((tq, tk), 1), s, NEG_BIG)
            m_prev = m_ref[h]
            m_new = jnp.maximum(m_prev, jnp.max(s, axis=1, keepdims=True))
            pr = jnp.exp2(s - jnp.tile(m_new, (1, rep)))
            corr = jnp.exp2(m_prev - m_new)
            l_ref[h] = corr * l_ref[h] + jnp.sum(pr, axis=1, keepdims=True)
            acc_ref[h] = corr * acc_ref[h] + jnp.dot(pr.astype(BF16), vb, preferred_element_type=F32)
            m_ref[h] = m_new

    @pl.when(ki < qi)
    def _():
        update(False)

    @pl.when(ki == qi)
    def _():
        update(True)
        lane = _iota((tq, LANES), 1)
        o_ref[...] = jnp.where(lane < HEAD_DIM, acc_ref[0] / l_ref[0], acc_ref[1] / l_ref[1])


def _fox_prompt(qa, ka, vb, tq):
    b, _, l, _ = qa.shape
    nq = l // tq
    pairs = [(qi, ki) for qi in range(nq) for ki in range(qi + 1)]
    qi_of = jnp.array([p[0] for p in pairs], jnp.int32)
    ki_of = jnp.array([p[1] for p in pairs], jnp.int32)
    grid_spec = pltpu.PrefetchScalarGridSpec(
        num_scalar_prefetch=2, grid=(b, FOX_DIM // LANES, len(pairs)),
        in_specs=[pl.BlockSpec((None, 2, tq, LANES), lambda i, p, s, qi, ki: (i, p, qi[s], 0)),
                  pl.BlockSpec((None, 2, tq, LANES), lambda i, p, s, qi, ki: (i, p, ki[s], 0)),
                  pl.BlockSpec((None, tq, LANES), lambda i, p, s, qi, ki: (i, ki[s], p))],
        out_specs=pl.BlockSpec((None, tq, LANES), lambda i, p, s, qi, ki: (i, qi[s], p)),
        scratch_shapes=[pltpu.VMEM((2, tq, LANES), F32), pltpu.VMEM((2, tq, LANES), F32),
                        pltpu.VMEM((2, tq, LANES), F32)])
    return pl.pallas_call(
        _fox_prompt_kernel, grid_spec=grid_spec,
        out_shape=jax.ShapeDtypeStruct((b, l, FOX_DIM), F32),
        compiler_params=_cparams(("parallel", "parallel", "arbitrary")), name="fox_prompt")(qi_of, ki_of, qa, ka, vb)


def _fox_sample_kernel(pt_ref, qbd_ref, cq_ref, kn_ref, vn_ref, cn_ref, *rest, group):
    del pt_ref
    kp_refs, vp_refs, lf_refs = rest[:group], rest[group:2 * group], rest[2 * group:3 * group]
    o_ref, m_ref, l_ref, acc_ref, suf_ref = rest[3 * group:]
    s_id = pl.program_id(1)
    nq = qbd_ref.shape[0]
    rep = nq // FOX_HEADS
    row = _iota((nq, PAGE), 0)
    col = _iota((nq, PAGE), 1)

    def attend(scores, values):
        m_prev = m_ref[...]
        m_new = m_prev
        for s in scores:
            m_new = jnp.maximum(m_new, jnp.max(s, axis=1, keepdims=True))
        corr = jnp.exp(m_prev - m_new)
        l_new = corr * l_ref[...]
        acc = corr[:, 0:1] * acc_ref[...]
        for s, pv in zip(scores, values):
            pr = jnp.exp(s - m_new)
            l_new = l_new + jnp.sum(pr, axis=1, keepdims=True)
            acc = acc + pv(pr.astype(BF16))
        l_ref[...] = l_new
        acc_ref[...] = acc
        m_ref[...] = m_new

    @pl.when(s_id == 0)
    def _():
        m_ref[...] = jnp.full_like(m_ref, NEG_BIG)
        l_ref[...] = jnp.zeros_like(l_ref)
        acc_ref[...] = jnp.zeros_like(acc_ref)
        suf_ref[...] = jnp.zeros_like(suf_ref)
        s = lax.dot_general(qbd_ref[...], kn_ref[...].astype(BF16), _NT, preferred_element_type=F32)
        s = s + (cq_ref[...] - jnp.tile(cn_ref[...], (rep, 1)))
        s = jnp.where(col <= row // FOX_HEADS, s, NEG_BIG)
        vn = vn_ref[...].astype(BF16)
        attend([s], [lambda pr: jnp.dot(pr, vn, preferred_element_type=F32)])

    @pl.when(s_id > 0)
    def _():
        later = (_iota((PAGE, PAGE), 0) > _iota((PAGE, PAGE), 1)).astype(BF16)
        carry = suf_ref[...]
        qbd = qbd_ref[...]
        cq = cq_ref[...]
        scores, values = [], []
        lfs = [lf_refs[i][...] for i in range(group)]
        local = [_mm_sel_r(lf, later) for lf in lfs]
        totals = [jnp.broadcast_to(loc[:, 0:1] + lf[:, 0:1], carry.shape) for loc, lf in zip(local, lfs)]
        for i in range(group):
            suffix = local[i] + carry
            carry = carry + totals[i]
            s = jnp.dot(qbd, kp_refs[i][...].astype(BF16), preferred_element_type=F32)
            scores.append(s + (cq + jnp.tile(suffix, (rep, 1))))
            values.append(lambda pr, i=i: lax.dot_general(pr, vp_refs[i][...].astype(BF16), _NT,
                                                          preferred_element_type=F32))
        suf_ref[...] = carry
        attend(scores, values)

    @pl.when(s_id == pl.num_programs(1) - 1)
    def _():
        lane_head = _iota((nq, FOX_DIM), 1) // HEAD_DIM
        row_head = _iota((nq, FOX_DIM), 0) % FOX_HEADS
        own = jnp.where(lane_head == row_head, acc_ref[...], 0.0)
        fold = (_iota((FOX_DIM, HEAD_DIM), 0) % HEAD_DIM == _iota((FOX_DIM, HEAD_DIM), 1)).astype(BF16)
        o_ref[...] = _mm_sel_r(own, fold) / l_ref[:, 0:HEAD_DIM]


def _fox_sample(page_table, qbd, cq, k_new, v_new, c_new, pool_k, pool_v, pool_lf):
    b, nq, _ = qbd.shape
    n_pages = page_table.shape[1]
    group = min(FOX_PAGE_GROUP, n_pages)
    assert n_pages % group == 0

    def page(slot):
        return lambda i, s, pt: (pt[i, n_pages - 1 - (jnp.maximum(s - 1, 0) * group + slot)], 0, 0)

    per_b = lambda shape: pl.BlockSpec((None,) + shape, lambda i, s, pt: (i, 0, 0))
    slots = range(group)
    grid_spec = pltpu.PrefetchScalarGridSpec(
        num_scalar_prefetch=1, grid=(b, n_pages // group + 1),
        in_specs=[per_b((nq, FOX_DIM)), per_b((nq, LANES)), per_b((PAGE, FOX_DIM)), per_b((PAGE, FOX_DIM)),
                  per_b((FOX_HEADS, LANES))]
                 + [pl.BlockSpec((None, FOX_DIM, PAGE), page(i)) for i in slots]
                 + [pl.BlockSpec((None, FOX_DIM, PAGE), page(i)) for i in slots]
                 + [pl.BlockSpec((None, FOX_HEADS, PAGE), page(i)) for i in slots],
        out_specs=per_b((nq, HEAD_DIM)),
        scratch_shapes=[pltpu.VMEM((nq, LANES), F32), pltpu.VMEM((nq, LANES), F32),
                        pltpu.VMEM((nq, FOX_DIM), F32), pltpu.VMEM((FOX_HEADS, LANES), F32)])
    return pl.pallas_call(
        functools.partial(_fox_sample_kernel, group=group), grid_spec=grid_spec,
        out_shape=jax.ShapeDtypeStruct((b, nq, HEAD_DIM), F32),
        compiler_params=_cparams(("parallel", "arbitrary")), name="fox_sample")(
            page_table, qbd, cq, k_new, v_new, c_new, *([pool_k] * group), *([pool_v] * group),
            *([pool_lf] * group))


def _ssm_kernel(z_ref, xbc_ref, fc_ref, fr_ref, buf_ref, s0_ref, cw_ref, cb_ref, dtb_c_ref, dtb_r_ref,
                a_c_ref, a_r_ref, d_ref, nw_ref, tl_ref, tu_ref, y_ref, sout_ref, carry_ref, s_ref, *, valid):
    @pl.when(pl.program_id(1) == 0)
    def _():
        carry_ref[...] = buf_ref[...]
        s_ref[...] = s0_ref[...]

    raw = xbc_ref[...]
    cs = raw.shape[0]
    carry = carry_ref[...]
    cw = cw_ref[...]
    conv = (cw[0:1] * _shifted(raw, carry, 3) + cw[1:2] * _shifted(raw, carry, 2)
            + cw[2:3] * _shifted(raw, carry, 1) + cw[3:4] * raw)
    carry_ref[...] = raw[cs - SUBLANES:]
    xbc = _silu(conv + cb_ref[...])
    x = xbc[:, :SSM_DIM]

    dt_c = _softplus(fc_ref[...] + dtb_c_ref[...])
    dt_r = _softplus(fr_ref[...] + dtb_r_ref[...])
    if valid < cs:
        dt_c = jnp.where(_iota((cs, LANES), 0) < valid, dt_c, 0.0)
        dt_r = jnp.where(_iota((16, cs), 1) < valid, dt_r, 0.0)
    acum_c = _mm_sel_l(tl_ref[...], dt_c * a_c_ref[...])
    acum_r = _mm_sel_r(dt_r * a_r_ref[...], tu_ref[...])
    w_c = jnp.exp(acum_c[cs - 1:cs] - acum_c) * dt_c

    sel_row = _iota((LANES, LANES), 0)
    sel_lane = _iota((LANES, LANES), 1)
    lane = _iota((cs, LANES), 1)
    causal = _iota((cs, cs), 0) >= _iota((cs, cs), 1)
    y_pairs = []
    for p in range(SSM_HEADS // 2):
        g = (2 * p) // (SSM_HEADS // SSM_GROUPS)
        bm = xbc[:, SSM_DIM + g * SSM_STATE:SSM_DIM + (g + 1) * SSM_STATE]
        cm = xbc[:, SSM_DIM + (SSM_GROUPS + g) * SSM_STATE:SSM_DIM + (SSM_GROUPS + g + 1) * SSM_STATE]
        cb = _mm(cm, bm, _NT)
        xp = x[:, p * LANES:(p + 1) * LANES]
        y_intra = []
        dec_rows = []
        for h in range(2):
            hh = 8 + 2 * p + h
            a_col = _mm_sel_r(acum_c, (sel_row == hh).astype(BF16))
            seg = a_col[:, :cs] - acum_r[hh:hh + 1, :]
            lmat = jnp.exp(jnp.where(causal, seg, NEG_BIG))
            gmat = cb * lmat * dt_r[hh:hh + 1, :]
            y_intra.append(_mm(gmat, xp))
            dec_rows.append(jnp.broadcast_to(jnp.exp(a_col[cs - 1:cs]), (HEAD_DIM, LANES)))
        pair_sel = (sel_row == 8 + 2 * p + sel_lane // HEAD_DIM).astype(BF16)
        ea_pair = jnp.exp(_mm_sel_r(acum_c, pair_sel))
        w_pair = _mm_sel_r(w_c, pair_sel)
        s_pair = s_ref[p]
        y_inter = _mm(cm, s_pair, _NT) * ea_pair
        y_pairs.append(jnp.where(lane < HEAD_DIM, y_intra[0], y_intra[1]) + y_inter)
        s_ref[p] = s_pair * jnp.concatenate(dec_rows, axis=0) + _mm(xp * w_pair, bm, _TN)

    y = jnp.concatenate(y_pairs, axis=1) + d_ref[...] * x
    y = y * _silu(z_ref[...])
    y_ref[...] = _rmsnorm(y, nw_ref[...])

    @pl.when(pl.program_id(1) == pl.num_programs(1) - 1)
    def _():
        sout_ref[...] = s_ref[...]


def _ssm(z, xbc, fc, fr, buf8, s0, cw8, cb, dtb_c, dtb_r, a_c, a_r, d_exp, nw, tri_l, tri_u, valid):
    b, l, _ = z.shape
    cs = LANES
    full = lambda s: pl.BlockSpec(s, lambda i, j: (0,) * len(s))
    sspec = pl.BlockSpec((None, 4, LANES, LANES), lambda i, j: (i, 0, 0, 0))
    return pl.pallas_call(
        functools.partial(_ssm_kernel, valid=valid), grid=(b, l // cs),
        in_specs=[pl.BlockSpec((None, cs, SSM_DIM), lambda i, j: (i, j, 0)),
                  pl.BlockSpec((None, cs, SSM_CONV_DIM), lambda i, j: (i, j, 0)),
                  pl.BlockSpec((None, cs, LANES), lambda i, j: (i, j, 0)),
                  pl.BlockSpec((None, 16, cs), lambda i, j: (i, 0, j)),
                  pl.BlockSpec((None, SUBLANES, SSM_CONV_DIM), lambda i, j: (i, 0, 0)),
                  sspec,
                  full((SUBLANES, SSM_CONV_DIM)), full((1, SSM_CONV_DIM)), full((1, LANES)), full((16, LANES)),
                  full((1, LANES)), full((16, LANES)), full((1, SSM_DIM)), full((1, SSM_DIM)),
                  full((cs, cs)), full((cs, cs))],
        out_specs=[pl.BlockSpec((None, cs, SSM_DIM), lambda i, j: (i, j, 0)), sspec],
        out_shape=[jax.ShapeDtypeStruct((b, l, SSM_DIM), F32), jax.ShapeDtypeStruct((b, 4, LANES, LANES), F32)],
        scratch_shapes=[pltpu.VMEM((SUBLANES, SSM_CONV_DIM), F32), pltpu.VMEM((4, LANES, LANES), F32)],
        compiler_params=_cparams(("parallel", "arbitrary")), name="ssm")(
            z, xbc, fc, fr, buf8, s0, cw8, cb, dtb_c, dtb_r, a_c, a_r, d_exp, nw, tri_l, tri_u)


def _pad_rows(a, rows):
    return jnp.pad(a, ((0, 0), (rows - a.shape[1], 0), (0, 0)))


def _tri(n, seg):
    i = jnp.arange(n)
    same = (i[:, None] // seg) == (i[None, :] // seg)
    lower = (same & (i[:, None] >= i[None, :])).astype(BF16)
    return lower, lower.T


def _lane_vec(v, offset):
    n = v.shape[0]
    col = jnp.zeros((1, LANES), F32).at[0, offset:offset + n].set(v)
    row = jnp.zeros((16, LANES), F32).at[offset:offset + n, :].set(jnp.broadcast_to(v[:, None], (n, LANES)))
    return col, row


def _layer0(h, bsz, l, sc_buf, shift_prev, wkv0, p):
    m = bsz * l
    tl = min(512, l)
    z_sc, z_rw = _in_proj(h, p['g_mix'], p['w_in'], (3 * SC_DIM, RW_SHIFT_W))
    z_sc = z_sc.reshape(bsz, l, 3 * SC_DIM)
    z_rw = z_rw.reshape(bsz, l, RW_SHIFT_W)
    ya, sc_last = _sconv(z_sc, _pad_rows(sc_buf, SUBLANES), p['sc_w8'], tl)
    r, lw, kp, v, kn, kb, g = _rwkv_pre(z_rw, _pad_rows(shift_prev[:, None], SUBLANES), p['mu'], p['w0'], p['w2p'],
                                        p['a0'], p['a2p'], p['g2'], p['k_k'], p['k_a'], p['ones_bd'], tl)
    ht = jnp.swapaxes(wkv0, 2, 3).reshape(bsz, 4, 2, HEAD_DIM, HEAD_DIM)
    eye2 = jnp.eye(2, dtype=F32)
    h_bd = jnp.einsum('bpgji,gk->bpgjki', ht, eye2).reshape(bsz, 4, LANES, LANES)
    lpad = -(-l // RW_CHUNK) * RW_CHUNK
    padl = lambda t: jnp.pad(t, ((0, 0), (0, lpad - l), (0, 0)))
    o, h_fin = _rwkv_chunk(padl(r), padl(lw), padl(kp), padl(v), padl(kn), padl(kb), h_bd, RW_CHUNK)
    o = o[:, :l]
    h_fin = h_fin.reshape(bsz, 4, 2, HEAD_DIM, 2, HEAD_DIM)
    wkv_new = jnp.stack([h_fin[:, :, 0, :, 0], h_fin[:, :, 1, :, 1]], axis=2).reshape(bsz, RW_HEADS, HEAD_DIM, HEAD_DIM)
    wkv_new = jnp.swapaxes(wkv_new, 2, 3)
    flat = lambda t: t.reshape(m, t.shape[-1])
    h = _out0(h, flat(ya), flat(o), flat(r), flat(kp), flat(v), flat(g), p['ln_w'], p['ln_b'], p['r_k'],
              p['ones_bd'], p['wo_a'], p['wo_b'])
    h = _ffn(h, p['g_ffn'], p['w_gate'], p['w_up'], p['w_down'], p['g_ffn'], False)
    return h, sc_last[:, SUBLANES - 2:], z_rw[:, -1], wkv_new


def _layer1(h, bsz, l, past, conv_buf, ssm0, p, g_final):
    m = bsz * l
    q, k, v, zg, xbc, fdt, fdt_t = _in_proj(h, p['g_mix'], p['w_in'],
                                            (FOX_DIM, FOX_DIM, FOX_DIM, SSM_DIM, SSM_CONV_DIM, LANES), p['w_t'])
    fr = jnp.swapaxes(fdt_t.reshape(16, bsz, l), 0, 1)
    if past is None:
        tl = min(512, l)
        tri_l, tri_u = _tri(tl, tl)
        lf, cc, cr = _logf(fdt.reshape(bsz, l, LANES), fr, p['fb_c'], p['fb_r'], tri_l, tri_u, tl)
        qa, ka, vb = _fox_aug(q.reshape(bsz, l, FOX_DIM), k.reshape(bsz, l, FOX_DIM), v.reshape(bsz, l, FOX_DIM),
                              cc, tl)
        yf = _fox_prompt(qa, ka, vb, tl)
        logf_new = lf[..., :FOX_HEADS]
    else:
        page_table, pool_k, pool_v, pool_lf = past
        tri_l, tri_u = _tri(m, l)
        fr_flat = fdt_t.reshape(1, 16, m)
        lf, cc, cr = _logf(fdt.reshape(1, m, LANES), fr_flat, p['fb_c'], p['fb_r'], tri_l, tri_u, m)
        logf_new = lf.reshape(bsz, l, LANES)[..., :FOX_HEADS]
        q4 = q.reshape(bsz, l, FOX_HEADS, HEAD_DIM) * (HEAD_DIM ** -0.5)
        qbd = jnp.einsum('bqhd,hg->bqhgd', q4, jnp.eye(FOX_HEADS, dtype=F32))
        qbd = qbd.reshape(bsz, l * FOX_HEADS, FOX_DIM).astype(BF16)
        c_q = cc.reshape(bsz, l, LANES)[..., :FOX_HEADS].reshape(bsz, l * FOX_HEADS, 1)
        c_q = jnp.broadcast_to(c_q, (bsz, l * FOX_HEADS, LANES))
        c_new = jnp.pad(jnp.swapaxes(cr.reshape(16, bsz, l), 0, 1)[:, :FOX_HEADS], ((0, 0), (0, 0), (0, PAGE - l)))
        pad_new = lambda t: jnp.pad(t.reshape(bsz, l, FOX_DIM), ((0, 0), (0, PAGE - l), (0, 0)))
        yf = _fox_sample(page_table, qbd, c_q, pad_new(k), pad_new(v), c_new, pool_k, pool_v, pool_lf)
        yf = yf.reshape(bsz, l, FOX_DIM)
    ctl, ctu = _tri(SSD_CHUNK, SSD_CHUNK)
    xbc3 = xbc.reshape(bsz, l, SSM_CONV_DIM)
    if l % SSD_CHUNK == 0:
        lpad, valid = l, SSD_CHUNK
    else:
        assert l < SSD_CHUNK
        lpad, valid = SSD_CHUNK, l
    padl = lambda t: jnp.pad(t, ((0, 0), (0, lpad - l), (0, 0)))
    ym, s_fin = _ssm(padl(zg.reshape(bsz, l, SSM_DIM)), padl(xbc3), padl(fdt.reshape(bsz, l, LANES)),
                     jnp.pad(fr, ((0, 0), (0, 0), (0, lpad - l))),
                     _pad_rows(conv_buf, SUBLANES), ssm0.reshape(bsz, 4, LANES, LANES), p['cw8'], p['cb'],
                     p['dtb_c'], p['dtb_r'], p['a_c'], p['a_r'], p['d_exp'], p['nw'], ctl, ctu, valid)
    ym = ym[:, :l]
    h = _out1(h, yf.reshape(m, FOX_DIM), ym.reshape(m, SSM_DIM), p['wo_a'], p['wo_b'])
    h = _ffn(h, p['g_ffn'], p['w_gate'], p['w_up'], p['w_down'], g_final, True)
    shape5 = (bsz, l, FOX_HEADS, HEAD_DIM)
    return (h, k.reshape(shape5), v.reshape(shape5), logf_new, xbc3[:, l - 3:],
            s_fin.reshape(bsz, SSM_HEADS, HEAD_DIM, SSM_STATE))


def kernel(x_prompt, x_sample, state_sc, state_shift, state_wkv, cache_k, cache_v, cache_logf, state_ssm_conv, state_ssm, page_table, norm_mix, norm_ffn, norm_final, w_in0, sc_conv_w, rw_mu, rw_w0, rw_w2, rw_a0, rw_a2, rw_g2, rw_k_k, rw_k_a, rw_r_k, rw_ln_w, rw_ln_b, w_out0, w_in1, fox_f_bias, ssm_conv_w, ssm_conv_b, ssm_dt_bias, ssm_a_log, ssm_d, ssm_norm_w, w_out1, w_gate, w_up, w_down):
    bp, lp, _ = x_prompt.shape
    bs, ls, _ = x_sample.shape
    row = lambda v: v.reshape(1, -1).astype(F32)
    head_ids = jnp.arange(RW_DIM) // HEAD_DIM
    ones_bd = (head_ids[:, None] == head_ids[None, :]).astype(BF16)
    zpad = jnp.zeros((64, RW_DIM), F32)

    p0 = dict(
        g_mix=row(norm_mix[0]), g_ffn=row(norm_ffn[0]), w_in=w_in0[0].astype(BF16),
        sc_w8=jnp.pad(sc_conv_w[0], ((0, SUBLANES - sc_conv_w.shape[1]), (0, 0))),
        mu=row(rw_mu[0]), w0=row(rw_w0[0]), a0=row(rw_a0[0]),
        w2p=jnp.concatenate([rw_w2[0], zpad], axis=0).astype(BF16),
        a2p=jnp.concatenate([zpad, rw_a2[0]], axis=0).astype(BF16),
        g2=rw_g2[0].astype(BF16), k_k=row(rw_k_k[0]), k_a=row(rw_k_a[0]), r_k=row(rw_r_k[0]),
        ln_w=row(rw_ln_w[0]), ln_b=row(rw_ln_b[0]), ones_bd=ones_bd,
        wo_a=w_out0[0, :SC_DIM].astype(BF16), wo_b=w_out0[0, SC_DIM:].astype(BF16),
        w_gate=w_gate[0].astype(BF16), w_up=w_up[0].astype(BF16), w_down=w_down[0].astype(BF16))

    w1 = w_in1[0]
    c0 = 3 * FOX_DIM
    f_cols = w1[:, c0:c0 + FOX_HEADS]
    z_cols = w1[:, c0 + FOX_HEADS:c0 + FOX_HEADS + SSM_DIM]
    xbc_cols = w1[:, c0 + FOX_HEADS + SSM_DIM:c0 + FOX_HEADS + SSM_DIM + SSM_CONV_DIM]
    dt_cols = w1[:, c0 + FOX_HEADS + SSM_DIM + SSM_CONV_DIM:]
    fdt_cols = jnp.concatenate([f_cols, dt_cols, jnp.zeros((D_MODEL, LANES - 16), F32)], axis=1)
    w1r = jnp.concatenate([w1[:, :c0], z_cols, xbc_cols, fdt_cols], axis=1).astype(BF16)
    fb_c, fb_r = _lane_vec(fox_f_bias[0], 0)
    dtb_c, dtb_r = _lane_vec(ssm_dt_bias[0], 8)
    a_c, a_r = _lane_vec(-jnp.exp(ssm_a_log[0].astype(F32)), 8)
    p1 = dict(
        g_mix=row(norm_mix[1]), g_ffn=row(norm_ffn[1]), w_in=w1r, w_t=fdt_cols[:, :16].T.astype(BF16),
        fb_c=fb_c, fb_r=fb_r, dtb_c=dtb_c, dtb_r=dtb_r, a_c=a_c, a_r=a_r,
        cw8=jnp.pad(ssm_conv_w[0], ((0, SUBLANES - ssm_conv_w.shape[1]), (0, 0))), cb=row(ssm_conv_b[0]),
        d_exp=row(jnp.repeat(ssm_d[0], HEAD_DIM)), nw=row(ssm_norm_w[0]),
        wo_a=w_out1[0, :FOX_DIM].astype(BF16), wo_b=w_out1[0, FOX_DIM:].astype(BF16),
        w_gate=w_gate[1].astype(BF16), w_up=w_up[1].astype(BF16), w_down=w_down[1].astype(BF16))
    g_final = row(norm_final)

    n_pool = cache_k.shape[1]
    page_t = lambda c: jnp.transpose(c[0], (0, 2, 3, 1)).reshape(n_pool, FOX_DIM, PAGE)
    past = (page_table, page_t(cache_k), page_t(cache_v), jnp.swapaxes(cache_logf[0], 1, 2))

    hp = x_prompt.reshape(bp * lp, D_MODEL)
    hs = x_sample.reshape(bs * ls, D_MODEL)
    hp, sc_p, sh_p, wkv_p = _layer0(hp, bp, lp, jnp.zeros((bp, 2, SC_DIM), F32), jnp.zeros((bp, RW_SHIFT_W), F32),
                                    jnp.zeros((bp, RW_HEADS, HEAD_DIM, HEAD_DIM), F32), p0)
    hs, sc_s, sh_s, wkv_s = _layer0(hs, bs, ls, state_sc[0], state_shift[0], state_wkv[0], p0)
    yp, k_p, v_p, lf_p, cv_p, ssm_p = _layer1(hp, bp, lp, None, jnp.zeros((bp, 3, SSM_CONV_DIM), F32),
                                              jnp.zeros((bp, SSM_HEADS, HEAD_DIM, SSM_STATE), F32), p1, g_final)
    ys, k_s, v_s, lf_s, cv_s, ssm_s = _layer1(hs, bs, ls, past, state_ssm_conv[0], state_ssm[0], p1, g_final)
    one = lambda t: t[None]
    return (yp.reshape(bp, lp, D_MODEL), ys.reshape(bs, ls, D_MODEL),
            one(sc_p), one(sc_s), one(sh_p), one(sh_s), one(wkv_p), one(wkv_s),
            one(k_p), one(k_s), one(v_p), one(v_s), one(lf_p), one(lf_s), one(cv_p), one(cv_s),
            one(ssm_p), one(ssm_s))
```

```python
import functools

import jax
import jax.numpy as jnp
from jax import lax
from jax.experimental import pallas as pl
from jax.experimental.pallas import tpu as pltpu

F32 = jnp.float32
BF16 = jnp.bfloat16

D_MODEL = 1024
HEAD_DIM = 64
NORM_EPS = 1e-6
SC_DIM = 512
RW_DIM = 512
RW_HEADS = 8
RW_SHIFT_W = 3 * RW_DIM + 64 + 64 + 128
RW_GN_EPS = 64e-5
FOX_DIM = 512
FOX_HEADS = 8
SSM_DIM = 512
SSM_HEADS = 8
SSM_GROUPS = 2
SSM_STATE = 128
SSM_CONV_DIM = SSM_DIM + 2 * SSM_GROUPS * SSM_STATE
FFN_HIDDEN = 2816
PAGE = 128
SSD_CHUNK = 128
FOX_PAGE_GROUP = 8
FOX_TQ = 1024
FOX_ROW_SPLIT = 2
RW_UNITS = 16
RW_CHUNK = 64

LANES = 128
SUBLANES = 8
VMEM_LIMIT = 56 * 1024 * 1024

NEG_BIG = -1e30
LOG2E = 1.4426950408889634


def _cparams(sem):
    return pltpu.CompilerParams(dimension_semantics=sem, vmem_limit_bytes=VMEM_LIMIT)


_NN = (((1,), (0,)), ((), ()))
_NT = (((1,), (1,)), ((), ()))
_TN = (((0,), (0,)), ((), ()))


def _mm(a, b, dims=_NN):
    return lax.dot_general(a.astype(BF16), b.astype(BF16), dims, preferred_element_type=F32)


def _split2(x):
    hi = x.astype(BF16)
    lo = (x - hi.astype(F32)).astype(BF16)
    return hi, lo


def _split3(x):
    hi = x.astype(BF16)
    r1 = x - hi.astype(F32)
    mid = r1.astype(BF16)
    lo = (r1 - mid.astype(F32)).astype(BF16)
    return hi, mid, lo


def _mm3(a, b, dims=_NN):
    ah, al = _split2(a)
    bh, bl = _split2(b)
    d = functools.partial(lax.dot_general, dimension_numbers=dims, preferred_element_type=F32)
    return d(ah, bh) + (d(al, bh) + d(ah, bl))


def _mm_sel_r(x, e, dims=_NN):
    h, m, l = _split3(x)
    d = functools.partial(lax.dot_general, dimension_numbers=dims, preferred_element_type=F32)
    return d(h, e) + (d(m, e) + d(l, e))


def _mm_sel_l(e, x, dims=_NN):
    h, m, l = _split3(x)
    d = functools.partial(lax.dot_general, dimension_numbers=dims, preferred_element_type=F32)
    return d(e, h) + (d(e, m) + d(e, l))


def _rmsnorm(x, g):
    return x * lax.rsqrt(jnp.mean(x * x, axis=-1, keepdims=True) + NORM_EPS) * g


def _sigmoid(x):
    return 1.0 / (1.0 + jnp.exp(-x))


def _silu(x):
    return x * _sigmoid(x)


def _softplus(x):
    return jnp.maximum(x, 0.0) + jnp.log(1.0 + jnp.exp(-jnp.abs(x)))


def _iota(shape, dim):
    return lax.broadcasted_iota(jnp.int32, shape, dim)


def _shifted(u, carry8, d):
    tl = u.shape[0]
    s = pltpu.roll(u, d, 0)
    c = pltpu.roll(carry8, d, 0)
    row = _iota((SUBLANES, u.shape[1]), 0)
    head = jnp.where(row < d, c, s[:SUBLANES])
    if tl == SUBLANES:
        return head
    return jnp.concatenate([head, s[SUBLANES:]], axis=0)


def _in_proj_kernel(*refs, splits, has_t):
    x_ref, g_ref, w_ref = refs[:3]
    pos = 3
    if has_t:
        wt_ref = refs[pos]
        pos += 1
    outs = refs[pos:pos + len(splits)]
    nb = _rmsnorm(x_ref[...], g_ref[...]).astype(BF16)
    off = 0
    for o_ref, width in zip(outs, splits):
        o_ref[...] = jnp.dot(nb, w_ref[:, off:off + width], preferred_element_type=F32)
        off += width
    if has_t:
        refs[pos + len(splits)][...] = lax.dot_general(wt_ref[...], nb, _NT, preferred_element_type=F32)


def _in_proj(x, g, w, splits, wt=None, seqs=None):
    m = x.shape[0]
    tm = min(512, m)
    n = sum(splits)
    in_specs = [pl.BlockSpec((tm, D_MODEL), lambda i: (i, 0)),
                pl.BlockSpec((1, D_MODEL), lambda i: (0, 0)),
                pl.BlockSpec((D_MODEL, n), lambda i: (0, 0))]
    args = [x, g, w]
    out_shape = [jax.ShapeDtypeStruct((m, s), F32) for s in splits]
    out_specs = [pl.BlockSpec((tm, s), lambda i: (i, 0)) for s in splits]
    if wt is not None:
        rows = wt.shape[0]
        in_specs.append(pl.BlockSpec((rows, D_MODEL), lambda i: (0, 0)))
        args.append(wt)
        if seqs is None:
            out_shape.append(jax.ShapeDtypeStruct((rows, m), F32))
            out_specs.append(pl.BlockSpec((rows, tm), lambda i: (0, i)))
        else:
            bsz, l = seqs
            per_seq = l // tm
            out_shape.append(jax.ShapeDtypeStruct((bsz, rows, l), F32))
            out_specs.append(pl.BlockSpec((None, rows, tm), lambda i: (i // per_seq, 0, i % per_seq)))
    return pl.pallas_call(
        functools.partial(_in_proj_kernel, splits=tuple(splits), has_t=wt is not None),
        grid=(m // tm,), in_specs=in_specs, out_specs=out_specs, out_shape=out_shape,
        compiler_params=_cparams(("parallel",)), name="in_proj")(*args)


def _ffn_kernel(x_ref, g_ref, wg_ref, wu_ref, wd_ref, gf_ref, o_ref, xn_ref, acc_ref, *, final_norm):
    j = pl.program_id(1)

    @pl.when(j == 0)
    def _():
        x = x_ref[...]
        xn_ref[...] = _rmsnorm(x, g_ref[...]).astype(BF16)
        acc_ref[...] = x

    xn = xn_ref[...]
    gate = jnp.dot(xn, wg_ref[...], preferred_element_type=F32)
    up = jnp.dot(xn, wu_ref[...], preferred_element_type=F32)
    act = (_silu(gate) * up).astype(BF16)
    acc_ref[...] += jnp.dot(act, wd_ref[...], preferred_element_type=F32)

    @pl.when(j == pl.num_programs(1) - 1)
    def _():
        y = acc_ref[...]
        if final_norm:
            y = _rmsnorm(y, gf_ref[...])
        o_ref[...] = y


def _ffn(x, g, wg, wu, wd, gfin, final_norm):
    m = x.shape[0]
    tm = min(512, m)
    th = FFN_HIDDEN // 2
    return pl.pallas_call(
        functools.partial(_ffn_kernel, final_norm=final_norm),
        grid=(m // tm, FFN_HIDDEN // th),
        in_specs=[pl.BlockSpec((tm, D_MODEL), lambda i, j: (i, 0)),
                  pl.BlockSpec((1, D_MODEL), lambda i, j: (0, 0)),
                  pl.BlockSpec((D_MODEL, th), lambda i, j: (0, j)),
                  pl.BlockSpec((D_MODEL, th), lambda i, j: (0, j)),
                  pl.BlockSpec((th, D_MODEL), lambda i, j: (j, 0)),
                  pl.BlockSpec((1, D_MODEL), lambda i, j: (0, 0))],
        out_specs=pl.BlockSpec((tm, D_MODEL), lambda i, j: (i, 0)),
        out_shape=jax.ShapeDtypeStruct((m, D_MODEL), F32),
        scratch_shapes=[pltpu.VMEM((tm, D_MODEL), BF16), pltpu.VMEM((tm, D_MODEL), F32)],
        compiler_params=_cparams(("parallel", "arbitrary")), name="ffn")(x, g, wg, wu, wd, gfin)


def _sconv_kernel(z_ref, buf_ref, w_ref, y_ref, last_ref, carry_ref):
    @pl.when(pl.program_id(1) == 0)
    def _():
        carry_ref[...] = buf_ref[...]

    z = z_ref[...]
    tl = z.shape[0]
    gb, gc, h = z[:, :SC_DIM], z[:, SC_DIM:2 * SC_DIM], z[:, 2 * SC_DIM:]
    u = gc * h
    carry = carry_ref[...]
    w = w_ref[...]
    y = w[0:1] * _shifted(u, carry, 2) + w[1:2] * _shifted(u, carry, 1) + w[2:3] * u
    y_ref[...] = gb * y
    tail = u[tl - SUBLANES:]
    carry_ref[...] = tail
    last_ref[...] = tail


def _sconv(z, buf8, w8, tl):
    b, l, _ = z.shape
    return pl.pallas_call(
        _sconv_kernel, grid=(b, l // tl),
        in_specs=[pl.BlockSpec((None, tl, 3 * SC_DIM), lambda i, j: (i, j, 0)),
                  pl.BlockSpec((None, SUBLANES, SC_DIM), lambda i, j: (i, 0, 0)),
                  pl.BlockSpec((SUBLANES, SC_DIM), lambda i, j: (0, 0))],
        out_specs=[pl.BlockSpec((None, tl, SC_DIM), lambda i, j: (i, j, 0)),
                   pl.BlockSpec((None, SUBLANES, SC_DIM), lambda i, j: (i, 0, 0))],
        out_shape=[jax.ShapeDtypeStruct((b, l, SC_DIM), F32),
                   jax.ShapeDtypeStruct((b, SUBLANES, SC_DIM), F32)],
        scratch_shapes=[pltpu.VMEM((SUBLANES, SC_DIM), F32)],
        compiler_params=_cparams(("parallel", "arbitrary")), name="sconv")(z, buf8, w8)


def _rwkv_pre_kernel(z_ref, sh_ref, mu_ref, w0_ref, w2_ref, a0_ref, a2_ref, g2_ref, kk_ref, ka_ref, ones_ref,
                     r_o, lw_o, kp_o, v_o, kn_o, kb_o, g_o, carry_ref):
    @pl.when(pl.program_id(1) == 0)
    def _():
        carry_ref[...] = sh_ref[...]

    z = z_ref[...]
    tl = z.shape[0]
    zs = _shifted(z, carry_ref[...], 1)
    carry_ref[...] = z[tl - SUBLANES:]
    zx = z + mu_ref[...] * (zs - z)
    r = zx[:, :RW_DIM]
    k = zx[:, RW_DIM:2 * RW_DIM]
    v = zx[:, 2 * RW_DIM:3 * RW_DIM]
    wa = zx[:, 3 * RW_DIM:3 * RW_DIM + 128]
    gd = zx[:, 3 * RW_DIM + 128:]
    w_log = -_softplus(-(w0_ref[...] + _mm(jnp.tanh(wa), w2_ref[...]))) - 0.5
    lw_o[...] = -jnp.exp(w_log)
    alpha = _sigmoid(a0_ref[...] + _mm(wa, a2_ref[...]))
    g_o[...] = _mm(_sigmoid(gd), g2_ref[...])
    kk = k * kk_ref[...]
    n2 = _mm_sel_r(kk * kk, ones_ref[...])
    kk = kk / jnp.maximum(jnp.sqrt(n2), 1e-12)
    r_o[...] = r
    kp_o[...] = k * (1.0 + (alpha - 1.0) * ka_ref[...])
    v_o[...] = v
    kn_o[...] = kk
    kb_o[...] = kk * alpha


def _rwkv_pre(z, sh8, mu, w0, w2p, a0, a2p, g2, k_k, k_a, ones_bd, tl):
    b, l, _ = z.shape
    full = lambda s: pl.BlockSpec(s, lambda i, j: (0,) * len(s))
    out = jax.ShapeDtypeStruct((b, l, RW_DIM), F32)
    ospec = pl.BlockSpec((None, tl, RW_DIM), lambda i, j: (i, j, 0))
    return pl.pallas_call(
        _rwkv_pre_kernel, grid=(b, l // tl),
        in_specs=[pl.BlockSpec((None, tl, RW_SHIFT_W), lambda i, j: (i, j, 0)),
                  pl.BlockSpec((None, SUBLANES, RW_SHIFT_W), lambda i, j: (i, 0, 0)),
                  full((1, RW_SHIFT_W)), full((1, RW_DIM)), full((128, RW_DIM)), full((1, RW_DIM)),
                  full((128, RW_DIM)), full((128, RW_DIM)), full((1, RW_DIM)), full((1, RW_DIM)),
                  full((RW_DIM, RW_DIM))],
        out_specs=[ospec] * 7, out_shape=[out] * 7,
        scratch_shapes=[pltpu.VMEM((SUBLANES, RW_SHIFT_W), F32)],
        compiler_params=_cparams(("parallel", "arbitrary")), name="rwkv_pre")(
            z, sh8, mu, w0, w2p, a0, a2p, g2, k_k, k_a, ones_bd)


def _unit_lower_inverse(ns, row, col, t):
    eye = (row == col).astype(F32)
    lower = row > col
    same8 = (row // 8) == (col // 8)
    n8 = [jnp.where(same8 & lower, n, 0.0) for n in ns]
    p2 = [_mm(a, a) for a in n8]
    p4 = [_mm(a, a) for a in p2]
    xs = [_mm(eye + a, eye + b) for a, b in zip(n8, p2)]
    xs = [_mm(x, eye + b) for x, b in zip(xs, p4)]
    s = 16
    while s <= t:
        level = ((row // s) == (col // s)) & ((row // (s // 2)) != (col // (s // 2))) & lower
        xc = [_mm(x, jnp.where(level, n, 0.0)) for x, n in zip(xs, ns)]
        xs = [x + _mm(y, x) for x, y in zip(xs, xc)]
        s *= 2
    return xs


def _rwkv_chunk_math(r, lw, k, v, kn, kb, h0s, tri, t):
    chunks = range(r.shape[0] // t)
    pairs = range(len(h0s))
    units = [(c, p) for c in chunks for p in pairs]
    lane_a = _iota((t, LANES), 1) < HEAD_DIM
    row2 = _iota((2 * t, 2 * t), 0)
    col2 = _iota((2 * t, 2 * t), 1)
    rowh = _iota((LANES, LANES), 0)
    colh = _iota((LANES, LANES), 1)
    cum = _mm_sel_l(tri, lw)
    last = [cum[(c + 1) * t - 1:(c + 1) * t] for c in chunks]
    cum_end = jnp.concatenate([jnp.broadcast_to(x, (t, x.shape[1])) for x in last], axis=0)
    g_inv = jnp.exp(-cum)
    g_end = jnp.exp(cum_end - cum)
    a_t = -kn * jnp.exp(cum - lw)
    r_t = r * jnp.exp(cum)
    b_t = kb * g_inv
    k_t = k * g_inv
    b_h = kb * g_end
    k_h = k * g_end

    def part(x, c, p):
        return x[c * t:(c + 1) * t, p * LANES:(p + 1) * LANES]

    def stack(x):
        return [jnp.concatenate([jnp.where(lane_a, part(x, c, p), 0.0), jnp.where(lane_a, 0.0, part(x, c, p))],
                                axis=0) for c, p in units]

    la, lr, lb, lk, vbd = stack(a_t), stack(r_t), stack(b_t), stack(k_t), stack(v)
    every = range(len(units))
    same_head = (row2 // t) == (col2 // t)
    strict = same_head & (row2 > col2)
    incl = same_head & (row2 >= col2)
    n = [jnp.where(strict, _mm3(la[i], lb[i], _NT), 0.0) for i in every]
    aak = [jnp.where(strict, _mm3(la[i], lk[i], _NT), 0.0) for i in every]
    arb = [jnp.where(incl, _mm(lr[i], lb[i], _NT), 0.0) for i in every]
    ark = [jnp.where(incl, _mm(lr[i], lk[i], _NT), 0.0) for i in every]
    aakv = [_mm3(aak[i], vbd[i]) for i in every]
    arkv = [_mm(ark[i], vbd[i]) for i in every]
    inv = _unit_lower_inverse(n, row2, col2, t)
    same_h = (rowh // HEAD_DIM) == (colh // HEAD_DIM)
    hs = list(h0s)
    o_rows = []
    for c in chunks:
        idx = [c * len(pairs) + p for p in pairs]
        rhs = [_mm3(la[i], hs[p]) + aakv[i] for p, i in zip(pairs, idx)]
        u0 = [_mm(inv[i], x) for i, x in zip(idx, rhs)]
        res = [x - y + _mm3(n[i], y) for i, x, y in zip(idx, rhs, u0)]
        u = [y + _mm(inv[i], z) for i, y, z in zip(idx, u0, res)]
        o = [_mm(lr[i], hs[p]) + _mm(arb[i], u[p]) + arkv[i] for p, i in zip(pairs, idx)]
        upd = [_mm3(jnp.concatenate([part(b_h, c, p), part(k_h, c, p)], axis=0),
                    jnp.concatenate([u[p][:t] + u[p][t:], part(v, c, p)], axis=0), _TN) for p in pairs]
        g_t = jnp.exp(last[c])
        hs = [jnp.broadcast_to(g_t[:, p * LANES:(p + 1) * LANES], (LANES, LANES)).T * hs[p]
              + jnp.where(same_h, upd[p], 0.0) for p in pairs]
        o_rows.append(jnp.concatenate([o[p][:t] + o[p][t:] for p in pairs], axis=1))
    return jnp.concatenate(o_rows, axis=0), hs


def _rwkv_chunk_kernel(r_ref, lw_ref, k_ref, v_ref, kn_ref, kb_ref, h0_ref, tri_ref, o_ref, hout_ref, h_ref):
    bb = r_ref.shape[0]
    npair = RW_DIM // LANES

    @pl.when(pl.program_id(1) == 0)
    def _():
        eye = (_iota((HEAD_DIM, HEAD_DIM), 0) == _iota((HEAD_DIM, HEAD_DIM), 1)).astype(BF16)
        lane_a = _iota((HEAD_DIM, LANES), 1) < HEAD_DIM
        for b in range(bb):
            for p in range(npair):
                tr = _mm_sel_l(eye, h0_ref[b, p], _NT)
                h_ref[b, p] = jnp.concatenate([jnp.where(lane_a, tr, 0.0), jnp.where(lane_a, 0.0, tr)], axis=0)

    wide = lambda ref: jnp.concatenate([ref[b] for b in range(bb)], axis=1)
    h0s = [h_ref[b, p] for b in range(bb) for p in range(npair)]
    o_wide, h_new = _rwkv_chunk_math(wide(r_ref), wide(lw_ref), wide(k_ref), wide(v_ref), wide(kn_ref),
                                     wide(kb_ref), h0s, tri_ref[...], RW_CHUNK)
    for b in range(bb):
        o_ref[b] = o_wide[:, b * RW_DIM:(b + 1) * RW_DIM]
        for p in range(npair):
            hn = h_new[b * npair + p]
            h_ref[b, p] = hn
            hout_ref[b, p] = hn[:HEAD_DIM] + hn[HEAD_DIM:]


def _rwkv_step_shape(bsz, l):
    pairs = RW_DIM // LANES
    chunks = 2 if l % (2 * RW_CHUNK) == 0 else 1
    bb = max(1, RW_UNITS // (pairs * chunks))
    while bsz % bb:
        bb //= 2
    return bb, chunks * RW_CHUNK


def _rwkv_chunk(r, lw, kp, v, kn, kb, h0, bb, t):
    b, l, _ = r.shape
    assert b % bb == 0 and l % t == 0 and t % RW_CHUNK == 0
    tri = _tri(t, RW_CHUNK)[0]
    xspec = pl.BlockSpec((bb, t, RW_DIM), lambda i, j: (i, j, 0))
    return pl.pallas_call(
        _rwkv_chunk_kernel, grid=(b // bb, l // t),
        in_specs=[xspec] * 6 + [pl.BlockSpec((bb, 4, LANES, HEAD_DIM), lambda i, j: (i, 0, 0, 0)),
                                pl.BlockSpec((t, t), lambda i, j: (0, 0))],
        out_specs=[xspec, pl.BlockSpec((bb, 4, HEAD_DIM, LANES), lambda i, j: (i, 0, 0, 0))],
        out_shape=[jax.ShapeDtypeStruct((b, l, RW_DIM), F32), jax.ShapeDtypeStruct((b, 4, HEAD_DIM, LANES), F32)],
        scratch_shapes=[pltpu.VMEM((bb, 4, LANES, LANES), F32)],
        compiler_params=_cparams(("parallel", "arbitrary")), name="rwkv_chunk")(r, lw, kp, v, kn, kb, h0, tri)


def _out0_kernel(x_ref, ya_ref, o_ref, r_ref, kp_ref, v_ref, g_ref, lnw_ref, lnb_ref, rk_ref, ones_ref,
                 wa_ref, wb_ref, out_ref):
    ones = ones_ref[...]
    o = o_ref[...]
    mean = _mm_sel_r(o, ones) * (1.0 / HEAD_DIM)
    d = o - mean
    var = _mm_sel_r(d * d, ones) * (1.0 / HEAD_DIM)
    on = d * lax.rsqrt(var + RW_GN_EPS) * lnw_ref[...] + lnb_ref[...]
    bonus = _mm_sel_r(r_ref[...] * kp_ref[...] * rk_ref[...], ones) * v_ref[...]
    yb = (on + bonus) * g_ref[...]
    out_ref[...] = x_ref[...] + _mm(ya_ref[...], wa_ref[...]) + _mm(yb, wb_ref[...])


def _out0(x, ya, o, r, kp, v, g, lnw, lnb, rk, ones_bd, wa, wb):
    m = x.shape[0]
    tm = min(512, m)
    row = lambda n: pl.BlockSpec((tm, n), lambda i: (i, 0))
    full = lambda s: pl.BlockSpec(s, lambda i: (0, 0))
    return pl.pallas_call(
        _out0_kernel, grid=(m // tm,),
        in_specs=[row(D_MODEL)] + [row(RW_DIM)] * 6 + [full((1, RW_DIM))] * 3 + [full((RW_DIM, RW_DIM))]
                 + [full((SC_DIM, D_MODEL)), full((RW_DIM, D_MODEL))],
        out_specs=row(D_MODEL), out_shape=jax.ShapeDtypeStruct((m, D_MODEL), F32),
        compiler_params=_cparams(("parallel",)), name="out0")(x, ya, o, r, kp, v, g, lnw, lnb, rk, ones_bd, wa, wb)


def _out1_kernel(x_ref, a_ref, b_ref, wa_ref, wb_ref, out_ref):
    out_ref[...] = x_ref[...] + _mm(a_ref[...], wa_ref[...]) + _mm(b_ref[...], wb_ref[...])


def _out1(x, a, b, wa, wb):
    m = x.shape[0]
    tm = min(512, m)
    row = lambda n: pl.BlockSpec((tm, n), lambda i: (i, 0))
    full = lambda s: pl.BlockSpec(s, lambda i: (0, 0))
    return pl.pallas_call(
        _out1_kernel, grid=(m // tm,),
        in_specs=[row(D_MODEL), row(FOX_DIM), row(SSM_DIM), full((FOX_DIM, D_MODEL)), full((SSM_DIM, D_MODEL))],
        out_specs=row(D_MODEL), out_shape=jax.ShapeDtypeStruct((m, D_MODEL), F32),
        compiler_params=_cparams(("parallel",)), name="out1")(x, a, b, wa, wb)


def _logf_kernel(fc_ref, fr_ref, bc_ref, br_ref, tl_ref, tu_ref, lf_o, cc_o, cr_o, carry_c, carry_r):
    @pl.when(pl.program_id(1) == 0)
    def _():
        carry_c[...] = jnp.zeros_like(carry_c)
        carry_r[...] = jnp.zeros_like(carry_r)

    tl = fc_ref.shape[0]
    lf_c = -_softplus(-(fc_ref[...] + bc_ref[...]))
    lf_o[...] = lf_c
    cc = _mm_sel_l(tl_ref[...], lf_c) + carry_c[0:1]
    cc_o[...] = cc
    carry_c[...] = jnp.broadcast_to(cc[tl - 1:tl], carry_c.shape)
    rep = tl // LANES
    lf_r = -_softplus(-(fr_ref[...] + jnp.tile(br_ref[...], (1, rep))))
    cr = _mm_sel_r(lf_r, tu_ref[...]) + jnp.tile(carry_r[...], (1, rep))
    cr_o[...] = cr
    carry_r[...] = jnp.broadcast_to(cr[:, tl - 1:tl], carry_r.shape)


def _logf(fc, fr, bias_c, bias_r, tri_l, tri_u, tl):
    b, l, _ = fc.shape
    cspec = pl.BlockSpec((None, tl, LANES), lambda i, j: (i, j, 0))
    rspec = pl.BlockSpec((None, 16, tl), lambda i, j: (i, 0, j))
    full = lambda s: pl.BlockSpec(s, lambda i, j: (0, 0))
    return pl.pallas_call(
        _logf_kernel, grid=(b, l // tl),
        in_specs=[cspec, rspec, full((1, LANES)), full((16, LANES)), full((tl, tl)), full((tl, tl))],
        out_specs=[cspec, cspec, rspec],
        out_shape=[jax.ShapeDtypeStruct((b, l, LANES), F32), jax.ShapeDtypeStruct((b, l, LANES), F32),
                   jax.ShapeDtypeStruct((b, 16, l), F32)],
        scratch_shapes=[pltpu.VMEM((SUBLANES, LANES), F32), pltpu.VMEM((16, LANES), F32)],
        compiler_params=_cparams(("parallel", "arbitrary")), name="logf")(fc, fr, bias_c, bias_r, tri_l, tri_u)


def _fox_aug_kernel(q_ref, k_ref, v_ref, cc_ref, wq_ref, wk_ref, oq_ref, ok_ref, qa_o, ka_o, vb_o):
    c_hi, c_mid, c_lo = _split3(cc_ref[...] * LOG2E)
    q = q_ref[...] * (LOG2E * HEAD_DIM ** -0.5)
    k = k_ref[...]
    for p in range(FOX_DIM // LANES):
        sl = slice(p * LANES, (p + 1) * LANES)
        xq = jnp.concatenate([q[:, sl].astype(BF16), c_hi, c_mid, c_lo], axis=1)
        xk = jnp.concatenate([k[:, sl].astype(BF16), c_hi, c_mid, c_lo], axis=1)
        for h in range(2):
            hh = 2 * p + h
            qa_o[hh] = (jnp.dot(xq, wq_ref[hh], preferred_element_type=F32) + oq_ref[...]).astype(BF16)
            ka_o[hh] = (jnp.dot(xk, wk_ref[hh], preferred_element_type=F32) + ok_ref[...]).astype(BF16)
    vb_o[...] = v_ref[...].astype(BF16)


def _fox_aug(q, k, v, cc, tl):
    b, l, _ = q.shape
    head = jnp.arange(FOX_HEADS)
    src = jnp.arange(4 * LANES)
    dst = jnp.arange(LANES)
    vec = (src[None, :, None] < LANES) & (src[None, :, None] - (head[:, None, None] % 2) * HEAD_DIM == dst[None, None, :]) \
        & (dst[None, None, :] < HEAD_DIM)
    part = (src[None, :, None] - LANES) // LANES
    is_c = (src[None, :, None] >= LANES) & ((src[None, :, None] % LANES) == head[:, None, None])
    wq = (vec | (is_c & (dst[None, None, :] == HEAD_DIM + part))).astype(BF16)
    wk = vec.astype(BF16) - (is_c & (dst[None, None, :] == HEAD_DIM + 3 + part)).astype(BF16)
    ones_q = ((dst >= HEAD_DIM + 3) & (dst < HEAD_DIM + 6)).astype(F32)[None]
    ones_k = ((dst >= HEAD_DIM) & (dst < HEAD_DIM + 3)).astype(F32)[None]
    xspec = pl.BlockSpec((None, tl, FOX_DIM), lambda i, j: (i, j, 0))
    aspec = pl.BlockSpec((None, FOX_HEADS, tl, LANES), lambda i, j: (i, 0, j, 0))
    full = lambda s: pl.BlockSpec(s, lambda i, j: (0,) * len(s))
    aug = jax.ShapeDtypeStruct((b, FOX_HEADS, l, LANES), BF16)
    return pl.pallas_call(
        _fox_aug_kernel, grid=(b, l // tl),
        in_specs=[xspec, xspec, xspec, pl.BlockSpec((None, tl, LANES), lambda i, j: (i, j, 0)),
                  full((FOX_HEADS, 4 * LANES, LANES)), full((FOX_HEADS, 4 * LANES, LANES)),
                  full((1, LANES)), full((1, LANES))],
        out_specs=[aspec, aspec, xspec],
        out_shape=[aug, aug, jax.ShapeDtypeStruct((b, l, FOX_DIM), BF16)],
        compiler_params=_cparams(("parallel", "parallel")), name="fox_aug")(q, k, v, cc, wq, wk, ones_q, ones_k)


def _fox_prompt_kernel(qi_ref, ki_ref, qa_ref, ka_ref, v_ref, o_ref, m_ref, l_ref, acc_ref):
    step = pl.program_id(2)
    qi = qi_ref[step]
    ki = ki_ref[step]
    tq = qa_ref.shape[1]
    tk = ka_ref.shape[1]
    rep = tk // LANES

    first_q = qi * tq
    first_k = ki * tk

    @pl.when(ki == 0)
    def _():
        m_ref[...] = jnp.full_like(m_ref, NEG_BIG)
        l_ref[...] = jnp.zeros_like(l_ref)
        acc_ref[...] = jnp.zeros_like(acc_ref)

    def update(diagonal):
        vb = v_ref[...]
        half = tq // FOX_ROW_SPLIT
        units = [(h, r * half) for h in range(2) for r in range(FOX_ROW_SPLIT)]
        s = [lax.dot_general(qa_ref[h, r0:r0 + half], ka_ref[h], _NT, preferred_element_type=F32)
             for h, r0 in units]
        if diagonal:
            ahead = first_q - first_k
            s = [jnp.where(ahead + r0 + _iota((half, tk), 0) >= _iota((half, tk), 1), x, NEG_BIG)
                 for x, (h, r0) in zip(s, units)]
        m_prev = [m_ref[h, r0:r0 + half] for h, r0 in units]
        m_new = [jnp.maximum(mp, jnp.max(x, axis=1, keepdims=True)) for mp, x in zip(m_prev, s)]
        pr = [jnp.exp2(x - jnp.tile(mn, (1, rep))) for x, mn in zip(s, m_new)]
        pv = [jnp.dot(x.astype(BF16), vb, preferred_element_type=F32) for x in pr]
        for (h, r0), mp, mn, x, y in zip(units, m_prev, m_new, pr, pv):
            corr = jnp.exp2(mp - mn)
            rows = slice(r0, r0 + half)
            l_ref[h, rows] = corr * l_ref[h, rows] + jnp.sum(x, axis=1, keepdims=True)
            acc_ref[h, rows] = corr * acc_ref[h, rows] + y
            m_ref[h, rows] = mn

    @pl.when(first_k + tk <= first_q)
    def _():
        update(False)

    @pl.when(first_k + tk > first_q)
    def _():
        update(True)

    @pl.when(first_k + tk == first_q + tq)
    def _():
        lane = _iota((tq, LANES), 1)
        o_ref[...] = jnp.where(lane < HEAD_DIM, acc_ref[0] / l_ref[0], acc_ref[1] / l_ref[1])


def _fox_prompt(qa, ka, vb, tq, tk):
    b, _, l, _ = qa.shape
    assert tq % tk == 0 and l % tq == 0
    pairs = [(qi, ki) for qi in range(l // tq) for ki in range((qi + 1) * tq // tk)]
    qi_of = jnp.array([p[0] for p in pairs], jnp.int32)
    ki_of = jnp.array([p[1] for p in pairs], jnp.int32)
    grid_spec = pltpu.PrefetchScalarGridSpec(
        num_scalar_prefetch=2, grid=(b, FOX_DIM // LANES, len(pairs)),
        in_specs=[pl.BlockSpec((None, 2, tq, LANES), lambda i, p, s, qi, ki: (i, p, qi[s], 0)),
                  pl.BlockSpec((None, 2, tk, LANES), lambda i, p, s, qi, ki: (i, p, ki[s], 0)),
                  pl.BlockSpec((None, tk, LANES), lambda i, p, s, qi, ki: (i, ki[s], p))],
        out_specs=pl.BlockSpec((None, tq, LANES), lambda i, p, s, qi, ki: (i, qi[s], p)),
        scratch_shapes=[pltpu.VMEM((2, tq, LANES), F32), pltpu.VMEM((2, tq, LANES), F32),
                        pltpu.VMEM((2, tq, LANES), F32)])
    return pl.pallas_call(
        _fox_prompt_kernel, grid_spec=grid_spec,
        out_shape=jax.ShapeDtypeStruct((b, l, FOX_DIM), F32),
        compiler_params=_cparams(("parallel", "parallel", "arbitrary")), name="fox_prompt")(qi_of, ki_of, qa, ka, vb)


def _fox_sample_kernel(pt_ref, qbd_ref, cq_ref, kn_ref, vn_ref, cn_ref, *rest, group):
    del pt_ref
    kp_refs, vp_refs, lf_refs = rest[:group], rest[group:2 * group], rest[2 * group:3 * group]
    o_ref, m_ref, l_ref, acc_ref, suf_ref = rest[3 * group:]
    s_id = pl.program_id(1)
    nq = qbd_ref.shape[0]
    rep = nq // FOX_HEADS
    row = _iota((nq, PAGE), 0)
    col = _iota((nq, PAGE), 1)

    def attend(scores, values):
        m_prev = m_ref[...]
        m_new = m_prev
        for s in scores:
            m_new = jnp.maximum(m_new, jnp.max(s, axis=1, keepdims=True))
        corr = jnp.exp(m_prev - m_new)
        l_new = corr * l_ref[...]
        acc = corr[:, 0:1] * acc_ref[...]
        for s, pv in zip(scores, values):
            pr = jnp.exp(s - m_new)
            l_new = l_new + jnp.sum(pr, axis=1, keepdims=True)
            acc = acc + pv(pr.astype(BF16))
        l_ref[...] = l_new
        acc_ref[...] = acc
        m_ref[...] = m_new

    @pl.when(s_id == 0)
    def _():
        m_ref[...] = jnp.full_like(m_ref, NEG_BIG)
        l_ref[...] = jnp.zeros_like(l_ref)
        acc_ref[...] = jnp.zeros_like(acc_ref)
        suf_ref[...] = jnp.zeros_like(suf_ref)
        s = lax.dot_general(qbd_ref[...], kn_ref[...].astype(BF16), _NT, preferred_element_type=F32)
        s = s + (cq_ref[...] - jnp.tile(cn_ref[...], (rep, 1)))
        s = jnp.where(col <= row // FOX_HEADS, s, NEG_BIG)
        vn = vn_ref[...].astype(BF16)
        attend([s], [lambda pr: jnp.dot(pr, vn, preferred_element_type=F32)])

    @pl.when(s_id > 0)
    def _():
        later = (_iota((PAGE, PAGE), 0) > _iota((PAGE, PAGE), 1)).astype(BF16)
        carry = suf_ref[...]
        qbd = qbd_ref[...]
        cq = cq_ref[...]
        scores, values = [], []
        lfs = [lf_refs[i][...] for i in range(group)]
        local = [_mm_sel_r(lf, later) for lf in lfs]
        totals = [jnp.broadcast_to(loc[:, 0:1] + lf[:, 0:1], carry.shape) for loc, lf in zip(local, lfs)]
        for i in range(group):
            suffix = local[i] + carry
            carry = carry + totals[i]
            s = jnp.dot(qbd, kp_refs[i][...].astype(BF16), preferred_element_type=F32)
            scores.append(s + (cq + jnp.tile(suffix, (rep, 1))))
            values.append(lambda pr, i=i: lax.dot_general(pr, vp_refs[i][...].astype(BF16), _NT,
                                                          preferred_element_type=F32))
        suf_ref[...] = carry
        attend(scores, values)

    @pl.when(s_id == pl.num_programs(1) - 1)
    def _():
        lane_head = _iota((nq, FOX_DIM), 1) // HEAD_DIM
        row_head = _iota((nq, FOX_DIM), 0) % FOX_HEADS
        own = jnp.where(lane_head == row_head, acc_ref[...], 0.0)
        fold = (_iota((FOX_DIM, HEAD_DIM), 0) % HEAD_DIM == _iota((FOX_DIM, HEAD_DIM), 1)).astype(BF16)
        o_ref[...] = _mm_sel_r(own, fold) / l_ref[:, 0:HEAD_DIM]


def _fox_sample(page_table, qbd, cq, k_new, v_new, c_new, pool_k, pool_v, pool_lf):
    b, nq, _ = qbd.shape
    n_pages = page_table.shape[1]
    group = min(FOX_PAGE_GROUP, n_pages)
    assert n_pages % group == 0

    def page(slot):
        return lambda i, s, pt: (pt[i, n_pages - 1 - (jnp.maximum(s - 1, 0) * group + slot)], 0, 0)

    per_b = lambda shape: pl.BlockSpec((None,) + shape, lambda i, s, pt: (i, 0, 0))
    slots = range(group)
    grid_spec = pltpu.PrefetchScalarGridSpec(
        num_scalar_prefetch=1, grid=(b, n_pages // group + 1),
        in_specs=[per_b((nq, FOX_DIM)), per_b((nq, LANES)), per_b((PAGE, FOX_DIM)), per_b((PAGE, FOX_DIM)),
                  per_b((FOX_HEADS, LANES))]
                 + [pl.BlockSpec((None, FOX_DIM, PAGE), page(i)) for i in slots]
                 + [pl.BlockSpec((None, FOX_DIM, PAGE), page(i)) for i in slots]
                 + [pl.BlockSpec((None, FOX_HEADS, PAGE), page(i)) for i in slots],
        out_specs=per_b((nq, HEAD_DIM)),
        scratch_shapes=[pltpu.VMEM((nq, LANES), F32), pltpu.VMEM((nq, LANES), F32),
                        pltpu.VMEM((nq, FOX_DIM), F32), pltpu.VMEM((FOX_HEADS, LANES), F32)])
    return pl.pallas_call(
        functools.partial(_fox_sample_kernel, group=group), grid_spec=grid_spec,
        out_shape=jax.ShapeDtypeStruct((b, nq, HEAD_DIM), F32),
        compiler_params=_cparams(("parallel", "arbitrary")), name="fox_sample")(
            page_table, qbd, cq, k_new, v_new, c_new, *([pool_k] * group), *([pool_v] * group),
            *([pool_lf] * group))


def _ssm_kernel(z_ref, xbc_ref, fc_ref, fr_ref, buf_ref, s0_ref, cw_ref, cb_ref, dtb_c_ref, dtb_r_ref,
                a_c_ref, a_r_ref, d_ref, nw_ref, tl_ref, tu_ref, eall_ref, eexp_ref, y_ref, sout_ref, carry_ref, s_ref,
                *, valid):
    @pl.when(pl.program_id(1) == 0)
    def _():
        carry_ref[...] = buf_ref[...]
        s_ref[...] = s0_ref[...]

    raw = xbc_ref[...]
    cs = raw.shape[0]
    carry = carry_ref[...]
    cw = cw_ref[...]
    conv = (cw[0:1] * _shifted(raw, carry, 3) + cw[1:2] * _shifted(raw, carry, 2)
            + cw[2:3] * _shifted(raw, carry, 1) + cw[3:4] * raw)
    carry_ref[...] = raw[cs - SUBLANES:]
    xbc = _silu(conv + cb_ref[...])
    x = xbc[:, :SSM_DIM]

    dt_c = _softplus(fc_ref[...] + dtb_c_ref[...])
    dt_r = _softplus(fr_ref[...] + dtb_r_ref[...])
    if valid < cs:
        dt_c = jnp.where(_iota((cs, LANES), 0) < valid, dt_c, 0.0)
        dt_r = jnp.where(_iota((16, cs), 1) < valid, dt_r, 0.0)
    acum_c = _mm_sel_l(tl_ref[...], dt_c * a_c_ref[...])
    acum_r = _mm_sel_r(dt_r * a_r_ref[...], tu_ref[...])
    w_c = jnp.exp(acum_c[cs - 1:cs] - acum_c) * dt_c

    heads = range(SSM_HEADS)
    pairs = range(SSM_HEADS // 2)
    group_of = lambda p: (2 * p) // (SSM_HEADS // SSM_GROUPS)
    lane = _iota((cs, LANES), 1)
    causal = _iota((cs, cs), 0) >= _iota((cs, cs), 1)
    a_all = _mm_sel_r(acum_c, eall_ref[...])
    ea_wide = jnp.exp(_mm_sel_r(acum_c, eexp_ref[...]))
    w_wide = _mm_sel_r(w_c, eexp_ref[...])
    bm = [xbc[:, SSM_DIM + g * SSM_STATE:SSM_DIM + (g + 1) * SSM_STATE] for g in range(SSM_GROUPS)]
    cm = [xbc[:, SSM_DIM + (SSM_GROUPS + g) * SSM_STATE:SSM_DIM + (SSM_GROUPS + g + 1) * SSM_STATE]
          for g in range(SSM_GROUPS)]
    cb = [_mm(cm[g], bm[g], _NT) for g in range(SSM_GROUPS)]
    xp = [x[:, p * LANES:(p + 1) * LANES] for p in pairs]
    s_old = [s_ref[p] for p in pairs]
    a_col = [a_all[:, h * LANES:(h + 1) * LANES] for h in heads]
    gmat = [cb[group_of(h // 2)] * jnp.exp(jnp.where(causal, a_col[h] - acum_r[8 + h:9 + h, :], NEG_BIG))
            * dt_r[8 + h:9 + h, :] for h in heads]
    y_intra = [_mm(gmat[h], xp[h // 2]) for h in heads]
    y_inter = [_mm(cm[group_of(p)], s_old[p], _NT) for p in pairs]
    s_upd = [_mm(xp[p] * w_wide[:, p * LANES:(p + 1) * LANES], bm[group_of(p)], _TN) for p in pairs]
    y_pairs = []
    for p in pairs:
        dec = jnp.concatenate([jnp.broadcast_to(jnp.exp(a_col[2 * p + h][cs - 1:cs]), (HEAD_DIM, LANES))
                               for h in range(2)], axis=0)
        s_ref[p] = s_old[p] * dec + s_upd[p]
        y_pairs.append(jnp.where(lane < HEAD_DIM, y_intra[2 * p], y_intra[2 * p + 1])
                       + y_inter[p] * ea_wide[:, p * LANES:(p + 1) * LANES])

    y = jnp.concatenate(y_pairs, axis=1) + d_ref[...] * x
    y = y * _silu(z_ref[...])
    y_ref[...] = _rmsnorm(y, nw_ref[...])

    @pl.when(pl.program_id(1) == pl.num_programs(1) - 1)
    def _():
        sout_ref[...] = s_ref[...]


def _ssm(z, xbc, fc, fr, buf8, s0, cw8, cb, dtb_c, dtb_r, a_c, a_r, d_exp, nw, tri_l, tri_u, valid):
    b, l, _ = z.shape
    cs = LANES
    src = jnp.arange(LANES)[:, None]
    e_all = (src == 8 + jnp.arange(SSM_HEADS * LANES)[None, :] // LANES).astype(BF16)
    e_exp = (src == 8 + jnp.arange(SSM_DIM)[None, :] // HEAD_DIM).astype(BF16)
    full = lambda s: pl.BlockSpec(s, lambda i, j: (0,) * len(s))
    sspec = pl.BlockSpec((None, 4, LANES, LANES), lambda i, j: (i, 0, 0, 0))
    return pl.pallas_call(
        functools.partial(_ssm_kernel, valid=valid), grid=(b, l // cs),
        in_specs=[pl.BlockSpec((None, cs, SSM_DIM), lambda i, j: (i, j, 0)),
                  pl.BlockSpec((None, cs, SSM_CONV_DIM), lambda i, j: (i, j, 0)),
                  pl.BlockSpec((None, cs, LANES), lambda i, j: (i, j, 0)),
                  pl.BlockSpec((None, 16, cs), lambda i, j: (i, 0, j)),
                  pl.BlockSpec((None, SUBLANES, SSM_CONV_DIM), lambda i, j: (i, 0, 0)),
                  sspec,
                  full((SUBLANES, SSM_CONV_DIM)), full((1, SSM_CONV_DIM)), full((1, LANES)), full((16, LANES)),
                  full((1, LANES)), full((16, LANES)), full((1, SSM_DIM)), full((1, SSM_DIM)),
                  full((cs, cs)), full((cs, cs)), full((LANES, SSM_HEADS * LANES)), full((LANES, SSM_DIM))],
        out_specs=[pl.BlockSpec((None, cs, SSM_DIM), lambda i, j: (i, j, 0)), sspec],
        out_shape=[jax.ShapeDtypeStruct((b, l, SSM_DIM), F32), jax.ShapeDtypeStruct((b, 4, LANES, LANES), F32)],
        scratch_shapes=[pltpu.VMEM((SUBLANES, SSM_CONV_DIM), F32), pltpu.VMEM((4, LANES, LANES), F32)],
        compiler_params=_cparams(("parallel", "arbitrary")), name="ssm")(
            z, xbc, fc, fr, buf8, s0, cw8, cb, dtb_c, dtb_r, a_c, a_r, d_exp, nw, tri_l, tri_u, e_all, e_exp)


def _pad_rows(a, rows):
    return jnp.pad(a, ((0, 0), (rows - a.shape[1], 0), (0, 0)))


def _tri(n, seg):
    i = jnp.arange(n)
    same = (i[:, None] // seg) == (i[None, :] // seg)
    lower = (same & (i[:, None] >= i[None, :])).astype(BF16)
    return lower, lower.T


def _lane_vec(v, offset):
    n = v.shape[0]
    col = jnp.zeros((1, LANES), F32).at[0, offset:offset + n].set(v)
    row = jnp.zeros((16, LANES), F32).at[offset:offset + n, :].set(jnp.broadcast_to(v[:, None], (n, LANES)))
    return col, row


def _layer0(h, bsz, l, sc_buf, shift_prev, wkv0, p):
    m = bsz * l
    tl = min(512, l)
    z_sc, z_rw = _in_proj(h, p['g_mix'], p['w_in'], (3 * SC_DIM, RW_SHIFT_W))
    z_sc = z_sc.reshape(bsz, l, 3 * SC_DIM)
    z_rw = z_rw.reshape(bsz, l, RW_SHIFT_W)
    ya, sc_last = _sconv(z_sc, _pad_rows(sc_buf, SUBLANES), p['sc_w8'], tl)
    r, lw, kp, v, kn, kb, g = _rwkv_pre(z_rw, _pad_rows(shift_prev[:, None], SUBLANES), p['mu'], p['w0'], p['w2p'],
                                        p['a0'], p['a2p'], p['g2'], p['k_k'], p['k_a'], p['ones_bd'], tl)
    h_in = wkv0.reshape(bsz, 4, LANES, HEAD_DIM)
    lpad = -(-l // RW_CHUNK) * RW_CHUNK
    padl = lambda t: jnp.pad(t, ((0, 0), (0, lpad - l), (0, 0)))
    o, h_fin = _rwkv_chunk(padl(r), padl(lw), padl(kp), padl(v), padl(kn), padl(kb), h_in, *_rwkv_step_shape(bsz, lpad))
    o = o[:, :l]
    wkv_new = jnp.transpose(h_fin.reshape(bsz, 4, HEAD_DIM, 2, HEAD_DIM), (0, 1, 3, 4, 2))
    wkv_new = wkv_new.reshape(bsz, RW_HEADS, HEAD_DIM, HEAD_DIM)
    flat = lambda t: t.reshape(m, t.shape[-1])
    h = _out0(h, flat(ya), flat(o), flat(r), flat(kp), flat(v), flat(g), p['ln_w'], p['ln_b'], p['r_k'],
              p['ones_bd'], p['wo_a'], p['wo_b'])
    h = _ffn(h, p['g_ffn'], p['w_gate'], p['w_up'], p['w_down'], p['g_ffn'], False)
    return h, sc_last[:, SUBLANES - 2:], z_rw[:, -1], wkv_new


def _layer1(h, bsz, l, past, conv_buf, ssm0, p, g_final):
    m = bsz * l
    splits = (FOX_DIM, FOX_DIM, FOX_DIM, SSM_DIM, SSM_CONV_DIM, LANES)
    shape5 = (bsz, l, FOX_HEADS, HEAD_DIM)
    if past is None:
        q, k, v, zg, xbc, fdt, t_out = _in_proj(h, p['g_mix'], p['w_in'], splits, p['w_tkv'], (bsz, l))
        fr = t_out[:, 2 * FOX_DIM:]
        rows_t = lambda t: jnp.transpose(t.reshape(bsz, FOX_HEADS, HEAD_DIM, l), (0, 3, 1, 2))
        k_new, v_new = rows_t(t_out[:, :FOX_DIM]), rows_t(t_out[:, FOX_DIM:2 * FOX_DIM])
    else:
        q, k, v, zg, xbc, fdt, fdt_t = _in_proj(h, p['g_mix'], p['w_in'], splits, p['w_t'])
        fr = jnp.swapaxes(fdt_t.reshape(16, bsz, l), 0, 1)
        k_new, v_new = k.reshape(shape5), v.reshape(shape5)
    if past is None:
        tl = min(512, l)
        tri_l, tri_u = _tri(tl, tl)
        lf, cc, cr = _logf(fdt.reshape(bsz, l, LANES), fr, p['fb_c'], p['fb_r'], tri_l, tri_u, tl)
        qa, ka, vb = _fox_aug(q.reshape(bsz, l, FOX_DIM), k.reshape(bsz, l, FOX_DIM), v.reshape(bsz, l, FOX_DIM),
                              cc, tl)
        yf = _fox_prompt(qa, ka, vb, min(FOX_TQ, l), tl)
        logf_new = lf[..., :FOX_HEADS]
    else:
        page_table, pool_k, pool_v, pool_lf = past
        tri_l, tri_u = _tri(m, l)
        fr_flat = fdt_t.reshape(1, 16, m)
        lf, cc, cr = _logf(fdt.reshape(1, m, LANES), fr_flat, p['fb_c'], p['fb_r'], tri_l, tri_u, m)
        logf_new = lf.reshape(bsz, l, LANES)[..., :FOX_HEADS]
        q4 = q.reshape(bsz, l, FOX_HEADS, HEAD_DIM) * (HEAD_DIM ** -0.5)
        qbd = jnp.einsum('bqhd,hg->bqhgd', q4, jnp.eye(FOX_HEADS, dtype=F32))
        qbd = qbd.reshape(bsz, l * FOX_HEADS, FOX_DIM).astype(BF16)
        c_q = cc.reshape(bsz, l, LANES)[..., :FOX_HEADS].reshape(bsz, l * FOX_HEADS, 1)
        c_q = jnp.broadcast_to(c_q, (bsz, l * FOX_HEADS, LANES))
        c_new = jnp.pad(jnp.swapaxes(cr.reshape(16, bsz, l), 0, 1)[:, :FOX_HEADS], ((0, 0), (0, 0), (0, PAGE - l)))
        pad_new = lambda t: jnp.pad(t.reshape(bsz, l, FOX_DIM), ((0, 0), (0, PAGE - l), (0, 0)))
        yf = _fox_sample(page_table, qbd, c_q, pad_new(k), pad_new(v), c_new, pool_k, pool_v, pool_lf)
        yf = yf.reshape(bsz, l, FOX_DIM)
    ctl, ctu = _tri(SSD_CHUNK, SSD_CHUNK)
    xbc3 = xbc.reshape(bsz, l, SSM_CONV_DIM)
    if l % SSD_CHUNK == 0:
        lpad, valid = l, SSD_CHUNK
    else:
        assert l < SSD_CHUNK
        lpad, valid = SSD_CHUNK, l
    padl = lambda t: jnp.pad(t, ((0, 0), (0, lpad - l), (0, 0)))
    ym, s_fin = _ssm(padl(zg.reshape(bsz, l, SSM_DIM)), padl(xbc3), padl(fdt.reshape(bsz, l, LANES)),
                     jnp.pad(fr, ((0, 0), (0, 0), (0, lpad - l))),
                     _pad_rows(conv_buf, SUBLANES), ssm0.reshape(bsz, 4, LANES, LANES), p['cw8'], p['cb'],
                     p['dtb_c'], p['dtb_r'], p['a_c'], p['a_r'], p['d_exp'], p['nw'], ctl, ctu, valid)
    ym = ym[:, :l]
    h = _out1(h, yf.reshape(m, FOX_DIM), ym.reshape(m, SSM_DIM), p['wo_a'], p['wo_b'])
    h = _ffn(h, p['g_ffn'], p['w_gate'], p['w_up'], p['w_down'], g_final, True)
    return (h, k_new, v_new, logf_new, xbc3[:, l - 3:], s_fin.reshape(bsz, SSM_HEADS, HEAD_DIM, SSM_STATE))


def kernel(x_prompt, x_sample, state_sc, state_shift, state_wkv, cache_k, cache_v, cache_logf, state_ssm_conv, state_ssm, page_table, norm_mix, norm_ffn, norm_final, w_in0, sc_conv_w, rw_mu, rw_w0, rw_w2, rw_a0, rw_a2, rw_g2, rw_k_k, rw_k_a, rw_r_k, rw_ln_w, rw_ln_b, w_out0, w_in1, fox_f_bias, ssm_conv_w, ssm_conv_b, ssm_dt_bias, ssm_a_log, ssm_d, ssm_norm_w, w_out1, w_gate, w_up, w_down):
    bp, lp, _ = x_prompt.shape
    bs, ls, _ = x_sample.shape
    row = lambda v: v.reshape(1, -1).astype(F32)
    head_ids = jnp.arange(RW_DIM) // HEAD_DIM
    ones_bd = (head_ids[:, None] == head_ids[None, :]).astype(BF16)
    zpad = jnp.zeros((64, RW_DIM), F32)

    p0 = dict(
        g_mix=row(norm_mix[0]), g_ffn=row(norm_ffn[0]), w_in=w_in0[0].astype(BF16),
        sc_w8=jnp.pad(sc_conv_w[0], ((0, SUBLANES - sc_conv_w.shape[1]), (0, 0))),
        mu=row(rw_mu[0]), w0=row(rw_w0[0]), a0=row(rw_a0[0]),
        w2p=jnp.concatenate([rw_w2[0], zpad], axis=0).astype(BF16),
        a2p=jnp.concatenate([zpad, rw_a2[0]], axis=0).astype(BF16),
        g2=rw_g2[0].astype(BF16), k_k=row(rw_k_k[0]), k_a=row(rw_k_a[0]), r_k=row(rw_r_k[0]),
        ln_w=row(rw_ln_w[0]), ln_b=row(rw_ln_b[0]), ones_bd=ones_bd,
        wo_a=w_out0[0, :SC_DIM].astype(BF16), wo_b=w_out0[0, SC_DIM:].astype(BF16),
        w_gate=w_gate[0].astype(BF16), w_up=w_up[0].astype(BF16), w_down=w_down[0].astype(BF16))

    w1 = w_in1[0]
    c0 = 3 * FOX_DIM
    f_cols = w1[:, c0:c0 + FOX_HEADS]
    z_cols = w1[:, c0 + FOX_HEADS:c0 + FOX_HEADS + SSM_DIM]
    xbc_cols = w1[:, c0 + FOX_HEADS + SSM_DIM:c0 + FOX_HEADS + SSM_DIM + SSM_CONV_DIM]
    dt_cols = w1[:, c0 + FOX_HEADS + SSM_DIM + SSM_CONV_DIM:]
    fdt_cols = jnp.concatenate([f_cols, dt_cols, jnp.zeros((D_MODEL, LANES - 16), F32)], axis=1)
    w1r = jnp.concatenate([w1[:, :c0], z_cols, xbc_cols, fdt_cols], axis=1).astype(BF16)
    fb_c, fb_r = _lane_vec(fox_f_bias[0], 0)
    dtb_c, dtb_r = _lane_vec(ssm_dt_bias[0], 8)
    a_c, a_r = _lane_vec(-jnp.exp(ssm_a_log[0].astype(F32)), 8)
    p1 = dict(
        g_mix=row(norm_mix[1]), g_ffn=row(norm_ffn[1]), w_in=w1r, w_t=fdt_cols[:, :16].T.astype(BF16),
        w_tkv=jnp.concatenate([w1[:, FOX_DIM:3 * FOX_DIM], fdt_cols[:, :16]], axis=1).T.astype(BF16),
        fb_c=fb_c, fb_r=fb_r, dtb_c=dtb_c, dtb_r=dtb_r, a_c=a_c, a_r=a_r,
        cw8=jnp.pad(ssm_conv_w[0], ((0, SUBLANES - ssm_conv_w.shape[1]), (0, 0))), cb=row(ssm_conv_b[0]),
        d_exp=row(jnp.repeat(ssm_d[0], HEAD_DIM)), nw=row(ssm_norm_w[0]),
        wo_a=w_out1[0, :FOX_DIM].astype(BF16), wo_b=w_out1[0, FOX_DIM:].astype(BF16),
        w_gate=w_gate[1].astype(BF16), w_up=w_up[1].astype(BF16), w_down=w_down[1].astype(BF16))
    g_final = row(norm_final)

    n_pool = cache_k.shape[1]
    page_t = lambda c: jnp.transpose(c[0], (0, 2, 3, 1)).reshape(n_pool, FOX_DIM, PAGE)
    past = (page_table, page_t(cache_k), page_t(cache_v), jnp.swapaxes(cache_logf[0], 1, 2))

    hp = x_prompt.reshape(bp * lp, D_MODEL)
    hs = x_sample.reshape(bs * ls, D_MODEL)
    hp, sc_p, sh_p, wkv_p = _layer0(hp, bp, lp, jnp.zeros((bp, 2, SC_DIM), F32), jnp.zeros((bp, RW_SHIFT_W), F32),
                                    jnp.zeros((bp, RW_HEADS, HEAD_DIM, HEAD_DIM), F32), p0)
    hs, sc_s, sh_s, wkv_s = _layer0(hs, bs, ls, state_sc[0], state_shift[0], state_wkv[0], p0)
    yp, k_p, v_p, lf_p, cv_p, ssm_p = _layer1(hp, bp, lp, None, jnp.zeros((bp, 3, SSM_CONV_DIM), F32),
                                              jnp.zeros((bp, SSM_HEADS, HEAD_DIM, SSM_STATE), F32), p1, g_final)
    ys, k_s, v_s, lf_s, cv_s, ssm_s = _layer1(hs, bs, ls, past, state_ssm_conv[0], state_ssm[0], p1, g_final)
    one = lambda t: t[None]
    return (yp.reshape(bp, lp, D_MODEL), ys.reshape(bs, ls, D_MODEL),
            one(sc_p), one(sc_s), one(sh_p), one(sh_s), one(wkv_p), one(wkv_s),
            one(k_p), one(k_s), one(v_p), one(v_s), one(lf_p), one(lf_s), one(cv_p), one(cv_s),
            one(ssm_p), one(ssm_s))
```

```python
import functools

import jax
import jax.numpy as jnp
from jax import lax
from jax.experimental import pallas as pl
from jax.experimental.pallas import tpu as pltpu

F32 = jnp.float32
BF16 = jnp.bfloat16

D_MODEL = 1024
HEAD_DIM = 64
NORM_EPS = 1e-6
SC_DIM = 512
RW_DIM = 512
RW_HEADS = 8
RW_SHIFT_W = 3 * RW_DIM + 64 + 64 + 128
RW_GN_EPS = 64e-5
FOX_DIM = 512
FOX_HEADS = 8
SSM_DIM = 512
SSM_HEADS = 8
SSM_GROUPS = 2
SSM_STATE = 128
SSM_CONV_DIM = SSM_DIM + 2 * SSM_GROUPS * SSM_STATE
FFN_HIDDEN = 2816
PAGE = 128
SSD_CHUNK = 128
FOX_PAGE_GROUP = 8
FOX_RING_SLOTS = 3
FOX_RING_AHEAD = FOX_RING_SLOTS - 1
FOX_TQ = 1024
FOX_ROW_SPLIT = 2
RW_UNITS = 16
RW_CHUNK = 64

LANES = 128
SUBLANES = 8
VMEM_LIMIT = 56 * 1024 * 1024

NEG_BIG = -1e30
LOG2E = 1.4426950408889634


def _cparams(sem):
    return pltpu.CompilerParams(dimension_semantics=sem, vmem_limit_bytes=VMEM_LIMIT)


_NN = (((1,), (0,)), ((), ()))
_NT = (((1,), (1,)), ((), ()))
_TN = (((0,), (0,)), ((), ()))


def _mm(a, b, dims=_NN):
    return lax.dot_general(a.astype(BF16), b.astype(BF16), dims, preferred_element_type=F32)


def _split2(x):
    hi = x.astype(BF16)
    lo = (x - hi.astype(F32)).astype(BF16)
    return hi, lo


def _split3(x):
    hi = x.astype(BF16)
    r1 = x - hi.astype(F32)
    mid = r1.astype(BF16)
    lo = (r1 - mid.astype(F32)).astype(BF16)
    return hi, mid, lo


def _mm3(a, b, dims=_NN):
    ah, al = _split2(a)
    bh, bl = _split2(b)
    d = functools.partial(lax.dot_general, dimension_numbers=dims, preferred_element_type=F32)
    return d(ah, bh) + (d(al, bh) + d(ah, bl))


def _mm_sel_r(x, e, dims=_NN):
    h, m, l = _split3(x)
    d = functools.partial(lax.dot_general, dimension_numbers=dims, preferred_element_type=F32)
    return d(h, e) + (d(m, e) + d(l, e))


def _mm_sum_r(x, e, dims=_NN):
    h, l = _split2(x)
    d = functools.partial(lax.dot_general, dimension_numbers=dims, preferred_element_type=F32)
    return d(h, e) + d(l, e)


def _mm_sel_l(e, x, dims=_NN):
    h, m, l = _split3(x)
    d = functools.partial(lax.dot_general, dimension_numbers=dims, preferred_element_type=F32)
    return d(e, h) + (d(e, m) + d(e, l))


def _rmsnorm(x, g):
    return x * lax.rsqrt(jnp.mean(x * x, axis=-1, keepdims=True) + NORM_EPS) * g


def _sigmoid(x):
    return 1.0 / (1.0 + jnp.exp(-x))


def _silu(x):
    return x * _sigmoid(x)


def _softplus(x):
    return jnp.maximum(x, 0.0) + jnp.log(1.0 + jnp.exp(-jnp.abs(x)))


def _iota(shape, dim):
    return lax.broadcasted_iota(jnp.int32, shape, dim)


def _shifted(u, carry8, d):
    tl = u.shape[0]
    s = pltpu.roll(u, d, 0)
    c = pltpu.roll(carry8, d, 0)
    row = _iota((SUBLANES, u.shape[1]), 0)
    head = jnp.where(row < d, c, s[:SUBLANES])
    if tl == SUBLANES:
        return head
    return jnp.concatenate([head, s[SUBLANES:]], axis=0)


def _in_proj_kernel(*refs, splits, has_t):
    x_ref, g_ref, w_ref = refs[:3]
    pos = 3
    if has_t:
        wt_ref = refs[pos]
        pos += 1
    outs = refs[pos:pos + len(splits)]
    nb = _rmsnorm(x_ref[...], g_ref[...]).astype(BF16)
    off = 0
    for o_ref, width in zip(outs, splits):
        o_ref[...] = jnp.dot(nb, w_ref[:, off:off + width], preferred_element_type=F32)
        off += width
    if has_t:
        refs[pos + len(splits)][...] = lax.dot_general(wt_ref[...], nb, _NT, preferred_element_type=F32)


def _in_proj(x, g, w, splits, wt=None, seqs=None):
    m = x.shape[0]
    tm = min(512, m)
    n = sum(splits)
    in_specs = [pl.BlockSpec((tm, D_MODEL), lambda i: (i, 0)),
                pl.BlockSpec((1, D_MODEL), lambda i: (0, 0)),
                pl.BlockSpec((D_MODEL, n), lambda i: (0, 0))]
    args = [x, g, w]
    out_shape = [jax.ShapeDtypeStruct((m, s), F32) for s in splits]
    out_specs = [pl.BlockSpec((tm, s), lambda i: (i, 0)) for s in splits]
    if wt is not None:
        rows = wt.shape[0]
        in_specs.append(pl.BlockSpec((rows, D_MODEL), lambda i: (0, 0)))
        args.append(wt)
        if seqs is None:
            out_shape.append(jax.ShapeDtypeStruct((rows, m), F32))
            out_specs.append(pl.BlockSpec((rows, tm), lambda i: (0, i)))
        else:
            bsz, l = seqs
            per_seq = l // tm
            out_shape.append(jax.ShapeDtypeStruct((bsz, rows, l), F32))
            out_specs.append(pl.BlockSpec((None, rows, tm), lambda i: (i // per_seq, 0, i % per_seq)))
    return pl.pallas_call(
        functools.partial(_in_proj_kernel, splits=tuple(splits), has_t=wt is not None),
        grid=(m // tm,), in_specs=in_specs, out_specs=out_specs, out_shape=out_shape,
        compiler_params=_cparams(("parallel",)), name="in_proj")(*args)


def _ffn_kernel(x_ref, g_ref, wg_ref, wu_ref, wd_ref, gf_ref, o_ref, xn_ref, acc_ref, *, final_norm):
    j = pl.program_id(1)

    @pl.when(j == 0)
    def _():
        x = x_ref[...]
        xn_ref[...] = _rmsnorm(x, g_ref[...]).astype(BF16)
        acc_ref[...] = x

    xn = xn_ref[...]
    gate = jnp.dot(xn, wg_ref[...], preferred_element_type=F32)
    up = jnp.dot(xn, wu_ref[...], preferred_element_type=F32)
    act = (_silu(gate) * up).astype(BF16)
    acc_ref[...] += jnp.dot(act, wd_ref[...], preferred_element_type=F32)

    @pl.when(j == pl.num_programs(1) - 1)
    def _():
        y = acc_ref[...]
        if final_norm:
            y = _rmsnorm(y, gf_ref[...])
        o_ref[...] = y


def _ffn(x, g, wg, wu, wd, gfin, final_norm):
    m = x.shape[0]
    tm = min(512, m)
    th = FFN_HIDDEN // 2
    return pl.pallas_call(
        functools.partial(_ffn_kernel, final_norm=final_norm),
        grid=(m // tm, FFN_HIDDEN // th),
        in_specs=[pl.BlockSpec((tm, D_MODEL), lambda i, j: (i, 0)),
                  pl.BlockSpec((1, D_MODEL), lambda i, j: (0, 0)),
                  pl.BlockSpec((D_MODEL, th), lambda i, j: (0, j)),
                  pl.BlockSpec((D_MODEL, th), lambda i, j: (0, j)),
                  pl.BlockSpec((th, D_MODEL), lambda i, j: (j, 0)),
                  pl.BlockSpec((1, D_MODEL), lambda i, j: (0, 0))],
        out_specs=pl.BlockSpec((tm, D_MODEL), lambda i, j: (i, 0)),
        out_shape=jax.ShapeDtypeStruct((m, D_MODEL), F32),
        scratch_shapes=[pltpu.VMEM((tm, D_MODEL), BF16), pltpu.VMEM((tm, D_MODEL), F32)],
        compiler_params=_cparams(("parallel", "arbitrary")), name="ffn")(x, g, wg, wu, wd, gfin)


def _sconv_kernel(z_ref, buf_ref, w_ref, y_ref, last_ref, carry_ref):
    @pl.when(pl.program_id(1) == 0)
    def _():
        carry_ref[...] = buf_ref[...]

    z = z_ref[...]
    tl = z.shape[0]
    gb, gc, h = z[:, :SC_DIM], z[:, SC_DIM:2 * SC_DIM], z[:, 2 * SC_DIM:]
    u = gc * h
    carry = carry_ref[...]
    w = w_ref[...]
    y = w[0:1] * _shifted(u, carry, 2) + w[1:2] * _shifted(u, carry, 1) + w[2:3] * u
    y_ref[...] = gb * y
    tail = u[tl - SUBLANES:]
    carry_ref[...] = tail
    last_ref[...] = tail


def _sconv(z, buf8, w8, tl):
    b, l, _ = z.shape
    return pl.pallas_call(
        _sconv_kernel, grid=(b, l // tl),
        in_specs=[pl.BlockSpec((None, tl, 3 * SC_DIM), lambda i, j: (i, j, 0)),
                  pl.BlockSpec((None, SUBLANES, SC_DIM), lambda i, j: (i, 0, 0)),
                  pl.BlockSpec((SUBLANES, SC_DIM), lambda i, j: (0, 0))],
        out_specs=[pl.BlockSpec((None, tl, SC_DIM), lambda i, j: (i, j, 0)),
                   pl.BlockSpec((None, SUBLANES, SC_DIM), lambda i, j: (i, 0, 0))],
        out_shape=[jax.ShapeDtypeStruct((b, l, SC_DIM), F32),
                   jax.ShapeDtypeStruct((b, SUBLANES, SC_DIM), F32)],
        scratch_shapes=[pltpu.VMEM((SUBLANES, SC_DIM), F32)],
        compiler_params=_cparams(("parallel", "arbitrary")), name="sconv")(z, buf8, w8)


def _rwkv_pre_kernel(z_ref, sh_ref, mu_ref, w0_ref, w2_ref, a0_ref, a2_ref, g2_ref, kk_ref, ka_ref, ones_ref,
                     r_o, lw_o, kp_o, v_o, kn_o, kb_o, g_o, carry_ref):
    @pl.when(pl.program_id(1) == 0)
    def _():
        carry_ref[...] = sh_ref[...]

    z = z_ref[...]
    tl = z.shape[0]
    zs = _shifted(z, carry_ref[...], 1)
    carry_ref[...] = z[tl - SUBLANES:]
    zx = z + mu_ref[...] * (zs - z)
    r = zx[:, :RW_DIM]
    k = zx[:, RW_DIM:2 * RW_DIM]
    v = zx[:, 2 * RW_DIM:3 * RW_DIM]
    wa = zx[:, 3 * RW_DIM:3 * RW_DIM + 128]
    gd = zx[:, 3 * RW_DIM + 128:]
    w_log = -_softplus(-(w0_ref[...] + _mm(jnp.tanh(wa), w2_ref[...]))) - 0.5
    lw_o[...] = -jnp.exp(w_log)
    alpha = _sigmoid(a0_ref[...] + _mm(wa, a2_ref[...]))
    g_o[...] = _mm(_sigmoid(gd), g2_ref[...])
    kk = k * kk_ref[...]
    n2 = _mm_sum_r(kk * kk, ones_ref[...])
    kk = kk / jnp.maximum(jnp.sqrt(n2), 1e-12)
    r_o[...] = r
    kp_o[...] = k * (1.0 + (alpha - 1.0) * ka_ref[...])
    v_o[...] = v
    kn_o[...] = kk
    kb_o[...] = kk * alpha


def _rwkv_pre(z, sh8, mu, w0, w2p, a0, a2p, g2, k_k, k_a, ones_bd, tl):
    b, l, _ = z.shape
    full = lambda s: pl.BlockSpec(s, lambda i, j: (0,) * len(s))
    out = jax.ShapeDtypeStruct((b, l, RW_DIM), F32)
    ospec = pl.BlockSpec((None, tl, RW_DIM), lambda i, j: (i, j, 0))
    return pl.pallas_call(
        _rwkv_pre_kernel, grid=(b, l // tl),
        in_specs=[pl.BlockSpec((None, tl, RW_SHIFT_W), lambda i, j: (i, j, 0)),
                  pl.BlockSpec((None, SUBLANES, RW_SHIFT_W), lambda i, j: (i, 0, 0)),
                  full((1, RW_SHIFT_W)), full((1, RW_DIM)), full((128, RW_DIM)), full((1, RW_DIM)),
                  full((128, RW_DIM)), full((128, RW_DIM)), full((1, RW_DIM)), full((1, RW_DIM)),
                  full((RW_DIM, RW_DIM))],
        out_specs=[ospec] * 7, out_shape=[out] * 7,
        scratch_shapes=[pltpu.VMEM((SUBLANES, RW_SHIFT_W), F32)],
        compiler_params=_cparams(("parallel", "arbitrary")), name="rwkv_pre")(
            z, sh8, mu, w0, w2p, a0, a2p, g2, k_k, k_a, ones_bd)


def _unit_lower_inverse(ns, row, col, t):
    eye = (row == col).astype(F32)
    lower = row > col
    same8 = (row // 8) == (col // 8)
    n8 = [jnp.where(same8 & lower, n, 0.0) for n in ns]
    p2 = [_mm(a, a) for a in n8]
    p4 = [_mm(a, a) for a in p2]
    xs = [_mm(eye + a, eye + b) for a, b in zip(n8, p2)]
    xs = [_mm(x, eye + b) for x, b in zip(xs, p4)]
    s = 16
    while s <= t:
        level = ((row // s) == (col // s)) & ((row // (s // 2)) != (col // (s // 2))) & lower
        xc = [_mm(x, jnp.where(level, n, 0.0)) for x, n in zip(xs, ns)]
        xs = [x + _mm(y, x) for x, y in zip(xs, xc)]
        s *= 2
    return xs


def _rwkv_chunk_math(r, lw, k, v, kn, kb, h0s, tri, t):
    chunks = range(r.shape[0] // t)
    pairs = range(len(h0s))
    units = [(c, p) for c in chunks for p in pairs]
    lane_a = _iota((t, LANES), 1) < HEAD_DIM
    row2 = _iota((2 * t, 2 * t), 0)
    col2 = _iota((2 * t, 2 * t), 1)
    rowh = _iota((LANES, LANES), 0)
    colh = _iota((LANES, LANES), 1)
    cum = _mm_sel_l(tri, lw)
    last = [cum[(c + 1) * t - 1:(c + 1) * t] for c in chunks]
    cum_end = jnp.concatenate([jnp.broadcast_to(x, (t, x.shape[1])) for x in last], axis=0)
    g_inv = jnp.exp(-cum)
    g_end = jnp.exp(cum_end - cum)
    a_t = -kn * jnp.exp(cum - lw)
    r_t = r * jnp.exp(cum)
    b_t = kb * g_inv
    k_t = k * g_inv
    b_h = kb * g_end
    k_h = k * g_end

    def part(x, c, p):
        return x[c * t:(c + 1) * t, p * LANES:(p + 1) * LANES]

    def stack(x):
        return [jnp.concatenate([jnp.where(lane_a, part(x, c, p), 0.0), jnp.where(lane_a, 0.0, part(x, c, p))],
                                axis=0) for c, p in units]

    la, lr, lb, lk, vbd = stack(a_t), stack(r_t), stack(b_t), stack(k_t), stack(v)
    every = range(len(units))
    same_head = (row2 // t) == (col2 // t)
    strict = same_head & (row2 > col2)
    incl = same_head & (row2 >= col2)
    n = [jnp.where(strict, _mm(la[i], lb[i], _NT), 0.0) for i in every]
    aak = [jnp.where(strict, _mm(la[i], lk[i], _NT), 0.0) for i in every]
    arb = [jnp.where(incl, _mm(lr[i], lb[i], _NT), 0.0) for i in every]
    ark = [jnp.where(incl, _mm(lr[i], lk[i], _NT), 0.0) for i in every]
    aakv = [_mm(aak[i], vbd[i]) for i in every]
    arkv = [_mm(ark[i], vbd[i]) for i in every]
    inv = _unit_lower_inverse(n, row2, col2, t)
    same_h = (rowh // HEAD_DIM) == (colh // HEAD_DIM)
    hs = list(h0s)
    o_rows = []
    for c in chunks:
        idx = [c * len(pairs) + p for p in pairs]
        rhs = [_mm3(la[i], hs[p]) + aakv[i] for p, i in zip(pairs, idx)]
        u0 = [_mm(inv[i], x) for i, x in zip(idx, rhs)]
        res = [x - y + _mm3(n[i], y) for i, x, y in zip(idx, rhs, u0)]
        u = [y + _mm(inv[i], z) for i, y, z in zip(idx, u0, res)]
        o = [_mm(lr[i], hs[p]) + _mm(arb[i], u[p]) + arkv[i] for p, i in zip(pairs, idx)]
        upd = [_mm(jnp.concatenate([part(b_h, c, p), part(k_h, c, p)], axis=0),
                    jnp.concatenate([u[p][:t] + u[p][t:], part(v, c, p)], axis=0), _TN) for p in pairs]
        g_t = jnp.exp(last[c])
        hs = [jnp.broadcast_to(g_t[:, p * LANES:(p + 1) * LANES], (LANES, LANES)).T * hs[p]
              + jnp.where(same_h, upd[p], 0.0) for p in pairs]
        o_rows.append(jnp.concatenate([o[p][:t] + o[p][t:] for p in pairs], axis=1))
    return jnp.concatenate(o_rows, axis=0), hs


def _rwkv_chunk_kernel(r_ref, lw_ref, k_ref, v_ref, kn_ref, kb_ref, h0_ref, tri_ref, o_ref, hout_ref, h_ref):
    bb = r_ref.shape[0]
    npair = RW_DIM // LANES

    @pl.when(pl.program_id(1) == 0)
    def _():
        eye = (_iota((HEAD_DIM, HEAD_DIM), 0) == _iota((HEAD_DIM, HEAD_DIM), 1)).astype(BF16)
        lane_a = _iota((HEAD_DIM, LANES), 1) < HEAD_DIM
        for b in range(bb):
            for p in range(npair):
                tr = _mm_sel_l(eye, h0_ref[b, p], _NT)
                h_ref[b, p] = jnp.concatenate([jnp.where(lane_a, tr, 0.0), jnp.where(lane_a, 0.0, tr)], axis=0)

    wide = lambda ref: jnp.concatenate([ref[b] for b in range(bb)], axis=1)
    h0s = [h_ref[b, p] for b in range(bb) for p in range(npair)]
    o_wide, h_new = _rwkv_chunk_math(wide(r_ref), wide(lw_ref), wide(k_ref), wide(v_ref), wide(kn_ref),
                                     wide(kb_ref), h0s, tri_ref[...], RW_CHUNK)
    for b in range(bb):
        o_ref[b] = o_wide[:, b * RW_DIM:(b + 1) * RW_DIM]
        for p in range(npair):
            hn = h_new[b * npair + p]
            h_ref[b, p] = hn
            hout_ref[b, p] = hn[:HEAD_DIM] + hn[HEAD_DIM:]


def _rwkv_step_shape(bsz, l):
    pairs = RW_DIM // LANES
    chunks = 2 if l % (2 * RW_CHUNK) == 0 else 1
    bb = max(1, RW_UNITS // (pairs * chunks))
    while bsz % bb:
        bb //= 2
    return bb, chunks * RW_CHUNK


def _rwkv_chunk(r, lw, kp, v, kn, kb, h0, bb, t):
    b, l, _ = r.shape
    assert b % bb == 0 and l % t == 0 and t % RW_CHUNK == 0
    tri = _tri(t, RW_CHUNK)[0]
    xspec = pl.BlockSpec((bb, t, RW_DIM), lambda i, j: (i, j, 0))
    return pl.pallas_call(
        _rwkv_chunk_kernel, grid=(b // bb, l // t),
        in_specs=[xspec] * 6 + [pl.BlockSpec((bb, 4, LANES, HEAD_DIM), lambda i, j: (i, 0, 0, 0)),
                                pl.BlockSpec((t, t), lambda i, j: (0, 0))],
        out_specs=[xspec, pl.BlockSpec((bb, 4, HEAD_DIM, LANES), lambda i, j: (i, 0, 0, 0))],
        out_shape=[jax.ShapeDtypeStruct((b, l, RW_DIM), F32), jax.ShapeDtypeStruct((b, 4, HEAD_DIM, LANES), F32)],
        scratch_shapes=[pltpu.VMEM((bb, 4, LANES, LANES), F32)],
        compiler_params=_cparams(("parallel", "arbitrary")), name="rwkv_chunk")(r, lw, kp, v, kn, kb, h0, tri)


def _out0_kernel(x_ref, ya_ref, o_ref, r_ref, kp_ref, v_ref, g_ref, lnw_ref, lnb_ref, rk_ref, ones_ref,
                 wa_ref, wb_ref, out_ref):
    ones = ones_ref[...]
    o = o_ref[...]
    mean = _mm_sum_r(o, ones) * (1.0 / HEAD_DIM)
    d = o - mean
    var = _mm_sum_r(d * d, ones) * (1.0 / HEAD_DIM)
    on = d * lax.rsqrt(var + RW_GN_EPS) * lnw_ref[...] + lnb_ref[...]
    bonus = _mm_sum_r(r_ref[...] * kp_ref[...] * rk_ref[...], ones) * v_ref[...]
    yb = (on + bonus) * g_ref[...]
    out_ref[...] = x_ref[...] + _mm(ya_ref[...], wa_ref[...]) + _mm(yb, wb_ref[...])


def _out0(x, ya, o, r, kp, v, g, lnw, lnb, rk, ones_bd, wa, wb):
    m = x.shape[0]
    tm = min(512, m)
    row = lambda n: pl.BlockSpec((tm, n), lambda i: (i, 0))
    full = lambda s: pl.BlockSpec(s, lambda i: (0, 0))
    return pl.pallas_call(
        _out0_kernel, grid=(m // tm,),
        in_specs=[row(D_MODEL)] + [row(RW_DIM)] * 6 + [full((1, RW_DIM))] * 3 + [full((RW_DIM, RW_DIM))]
                 + [full((SC_DIM, D_MODEL)), full((RW_DIM, D_MODEL))],
        out_specs=row(D_MODEL), out_shape=jax.ShapeDtypeStruct((m, D_MODEL), F32),
        compiler_params=_cparams(("parallel",)), name="out0")(x, ya, o, r, kp, v, g, lnw, lnb, rk, ones_bd, wa, wb)


def _out1_kernel(x_ref, a_ref, b_ref, wa_ref, wb_ref, out_ref):
    out_ref[...] = x_ref[...] + _mm(a_ref[...], wa_ref[...]) + _mm(b_ref[...], wb_ref[...])


def _out1(x, a, b, wa, wb):
    m = x.shape[0]
    tm = min(512, m)
    row = lambda n: pl.BlockSpec((tm, n), lambda i: (i, 0))
    full = lambda s: pl.BlockSpec(s, lambda i: (0, 0))
    return pl.pallas_call(
        _out1_kernel, grid=(m // tm,),
        in_specs=[row(D_MODEL), row(FOX_DIM), row(SSM_DIM), full((FOX_DIM, D_MODEL)), full((SSM_DIM, D_MODEL))],
        out_specs=row(D_MODEL), out_shape=jax.ShapeDtypeStruct((m, D_MODEL), F32),
        compiler_params=_cparams(("parallel",)), name="out1")(x, a, b, wa, wb)


def _logf_kernel(fc_ref, fr_ref, bc_ref, br_ref, tl_ref, tu_ref, lf_o, cc_o, cr_o, carry_c, carry_r):
    @pl.when(pl.program_id(1) == 0)
    def _():
        carry_c[...] = jnp.zeros_like(carry_c)
        carry_r[...] = jnp.zeros_like(carry_r)

    tl = fc_ref.shape[0]
    lf_c = -_softplus(-(fc_ref[...] + bc_ref[...]))
    lf_o[...] = lf_c
    cc = _mm_sel_l(tl_ref[...], lf_c) + carry_c[0:1]
    cc_o[...] = cc
    carry_c[...] = jnp.broadcast_to(cc[tl - 1:tl], carry_c.shape)
    rep = tl // LANES
    lf_r = -_softplus(-(fr_ref[...] + jnp.tile(br_ref[...], (1, rep))))
    cr = _mm_sel_r(lf_r, tu_ref[...]) + jnp.tile(carry_r[...], (1, rep))
    cr_o[...] = cr
    carry_r[...] = jnp.broadcast_to(cr[:, tl - 1:tl], carry_r.shape)


def _logf(fc, fr, bias_c, bias_r, tri_l, tri_u, tl):
    b, l, _ = fc.shape
    cspec = pl.BlockSpec((None, tl, LANES), lambda i, j: (i, j, 0))
    rspec = pl.BlockSpec((None, 16, tl), lambda i, j: (i, 0, j))
    full = lambda s: pl.BlockSpec(s, lambda i, j: (0, 0))
    return pl.pallas_call(
        _logf_kernel, grid=(b, l // tl),
        in_specs=[cspec, rspec, full((1, LANES)), full((16, LANES)), full((tl, tl)), full((tl, tl))],
        out_specs=[cspec, cspec, rspec],
        out_shape=[jax.ShapeDtypeStruct((b, l, LANES), F32), jax.ShapeDtypeStruct((b, l, LANES), F32),
                   jax.ShapeDtypeStruct((b, 16, l), F32)],
        scratch_shapes=[pltpu.VMEM((SUBLANES, LANES), F32), pltpu.VMEM((16, LANES), F32)],
        compiler_params=_cparams(("parallel", "arbitrary")), name="logf")(fc, fr, bias_c, bias_r, tri_l, tri_u)


def _fox_aug_kernel(q_ref, k_ref, v_ref, cc_ref, wq_ref, wk_ref, oq_ref, ok_ref, qa_o, ka_o, vb_o):
    c_hi, c_mid, c_lo = _split3(cc_ref[...] * LOG2E)
    q = q_ref[...] * (LOG2E * HEAD_DIM ** -0.5)
    k = k_ref[...]
    for p in range(FOX_DIM // LANES):
        sl = slice(p * LANES, (p + 1) * LANES)
        xq = jnp.concatenate([q[:, sl].astype(BF16), c_hi, c_mid, c_lo], axis=1)
        xk = jnp.concatenate([k[:, sl].astype(BF16), c_hi, c_mid, c_lo], axis=1)
        for h in range(2):
            hh = 2 * p + h
            qa_o[hh] = (jnp.dot(xq, wq_ref[hh], preferred_element_type=F32) + oq_ref[...]).astype(BF16)
            ka_o[hh] = (jnp.dot(xk, wk_ref[hh], preferred_element_type=F32) + ok_ref[...]).astype(BF16)
    vb_o[...] = v_ref[...].astype(BF16)


def _fox_aug(q, k, v, cc, tl):
    b, l, _ = q.shape
    head = jnp.arange(FOX_HEADS)
    src = jnp.arange(4 * LANES)
    dst = jnp.arange(LANES)
    vec = (src[None, :, None] < LANES) & (src[None, :, None] - (head[:, None, None] % 2) * HEAD_DIM == dst[None, None, :]) \
        & (dst[None, None, :] < HEAD_DIM)
    part = (src[None, :, None] - LANES) // LANES
    is_c = (src[None, :, None] >= LANES) & ((src[None, :, None] % LANES) == head[:, None, None])
    wq = (vec | (is_c & (dst[None, None, :] == HEAD_DIM + part))).astype(BF16)
    wk = vec.astype(BF16) - (is_c & (dst[None, None, :] == HEAD_DIM + 3 + part)).astype(BF16)
    ones_q = ((dst >= HEAD_DIM + 3) & (dst < HEAD_DIM + 6)).astype(F32)[None]
    ones_k = ((dst >= HEAD_DIM) & (dst < HEAD_DIM + 3)).astype(F32)[None]
    xspec = pl.BlockSpec((None, tl, FOX_DIM), lambda i, j: (i, j, 0))
    aspec = pl.BlockSpec((None, FOX_HEADS, tl, LANES), lambda i, j: (i, 0, j, 0))
    full = lambda s: pl.BlockSpec(s, lambda i, j: (0,) * len(s))
    aug = jax.ShapeDtypeStruct((b, FOX_HEADS, l, LANES), BF16)
    return pl.pallas_call(
        _fox_aug_kernel, grid=(b, l // tl),
        in_specs=[xspec, xspec, xspec, pl.BlockSpec((None, tl, LANES), lambda i, j: (i, j, 0)),
                  full((FOX_HEADS, 4 * LANES, LANES)), full((FOX_HEADS, 4 * LANES, LANES)),
                  full((1, LANES)), full((1, LANES))],
        out_specs=[aspec, aspec, xspec],
        out_shape=[aug, aug, jax.ShapeDtypeStruct((b, l, FOX_DIM), BF16)],
        compiler_params=_cparams(("parallel", "parallel")), name="fox_aug")(q, k, v, cc, wq, wk, ones_q, ones_k)


def _fox_prompt_kernel(qi_ref, ki_ref, qa_ref, ka_ref, v_ref, o_ref, m_ref, l_ref, acc_ref):
    step = pl.program_id(2)
    qi = qi_ref[step]
    ki = ki_ref[step]
    tq = qa_ref.shape[1]
    tk = ka_ref.shape[1]
    rep = tk // LANES

    first_q = qi * tq
    first_k = ki * tk

    @pl.when(ki == 0)
    def _():
        m_ref[...] = jnp.full_like(m_ref, NEG_BIG)
        l_ref[...] = jnp.zeros_like(l_ref)
        acc_ref[...] = jnp.zeros_like(acc_ref)

    def update(diagonal):
        vb = v_ref[...]
        half = tq // FOX_ROW_SPLIT
        units = [(h, r * half) for h in range(2) for r in range(FOX_ROW_SPLIT)]
        s = [lax.dot_general(qa_ref[h, r0:r0 + half], ka_ref[h], _NT, preferred_element_type=F32)
             for h, r0 in units]
        if diagonal:
            ahead = first_q - first_k
            s = [jnp.where(ahead + r0 + _iota((half, tk), 0) >= _iota((half, tk), 1), x, NEG_BIG)
                 for x, (h, r0) in zip(s, units)]
        m_prev = [m_ref[h, r0:r0 + half] for h, r0 in units]
        m_new = [jnp.maximum(mp, jnp.max(x, axis=1, keepdims=True)) for mp, x in zip(m_prev, s)]
        pr = [jnp.exp2(x - jnp.tile(mn, (1, rep))) for x, mn in zip(s, m_new)]
        pv = [jnp.dot(x.astype(BF16), vb, preferred_element_type=F32) for x in pr]
        for (h, r0), mp, mn, x, y in zip(units, m_prev, m_new, pr, pv):
            corr = jnp.exp2(mp - mn)
            rows = slice(r0, r0 + half)
            l_ref[h, rows] = corr * l_ref[h, rows] + jnp.sum(x, axis=1, keepdims=True)
            acc_ref[h, rows] = corr * acc_ref[h, rows] + y
            m_ref[h, rows] = mn

    @pl.when(first_k + tk <= first_q)
    def _():
        update(False)

    @pl.when(first_k + tk > first_q)
    def _():
        update(True)

    @pl.when(first_k + tk == first_q + tq)
    def _():
        lane = _iota((tq, LANES), 1)
        o_ref[...] = jnp.where(lane < HEAD_DIM, acc_ref[0] / l_ref[0], acc_ref[1] / l_ref[1])


def _fox_prompt(qa, ka, vb, tq, tk):
    b, _, l, _ = qa.shape
    assert tq % tk == 0 and l % tq == 0
    pairs = [(qi, ki) for qi in range(l // tq) for ki in range((qi + 1) * tq // tk)]
    qi_of = jnp.array([p[0] for p in pairs], jnp.int32)
    ki_of = jnp.array([p[1] for p in pairs], jnp.int32)
    grid_spec = pltpu.PrefetchScalarGridSpec(
        num_scalar_prefetch=2, grid=(b, FOX_DIM // LANES, len(pairs)),
        in_specs=[pl.BlockSpec((None, 2, tq, LANES), lambda i, p, s, qi, ki: (i, p, qi[s], 0)),
                  pl.BlockSpec((None, 2, tk, LANES), lambda i, p, s, qi, ki: (i, p, ki[s], 0)),
                  pl.BlockSpec((None, tk, LANES), lambda i, p, s, qi, ki: (i, ki[s], p))],
        out_specs=pl.BlockSpec((None, tq, LANES), lambda i, p, s, qi, ki: (i, qi[s], p)),
        scratch_shapes=[pltpu.VMEM((2, tq, LANES), F32), pltpu.VMEM((2, tq, LANES), F32),
                        pltpu.VMEM((2, tq, LANES), F32)])
    return pl.pallas_call(
        _fox_prompt_kernel, grid_spec=grid_spec,
        out_shape=jax.ShapeDtypeStruct((b, l, FOX_DIM), F32),
        compiler_params=_cparams(("parallel", "parallel", "arbitrary")), name="fox_prompt")(qi_of, ki_of, qa, ka, vb)


def _fox_sample_kernel(pt_ref, qbd_ref, cq_ref, kn_ref, vn_ref, cn_ref, pool_k, pool_v, pool_lf, o_ref,
                       kbuf, vbuf, lfbuf, sem, m_ref, l_ref, acc_ref, suf_ref, *, group, n_pages):
    b_id = pl.program_id(0)
    s_id = pl.program_id(1)
    nq = qbd_ref.shape[0]
    rep = nq // FOX_HEADS
    row = _iota((nq, PAGE), 0)
    col = _iota((nq, PAGE), 1)
    groups_per_seq = n_pages // group
    total_groups = pl.num_programs(0) * groups_per_seq

    def group_copies(t, slot):
        seq = t // groups_per_seq
        first = (t % groups_per_seq) * group
        out = []
        for i in range(group):
            pg = pt_ref[seq, n_pages - 1 - (first + i)]
            out.append(pltpu.make_async_copy(pool_k.at[pg], kbuf.at[slot, i], sem.at[slot]))
            out.append(pltpu.make_async_copy(pool_v.at[pg], vbuf.at[slot, i], sem.at[slot]))
            out.append(pltpu.make_async_copy(pool_lf.at[pg], lfbuf.at[slot, i], sem.at[slot]))
        return out

    def start_group(t):
        for c in group_copies(t, t % FOX_RING_SLOTS):
            c.start()

    @pl.when((b_id == 0) & (s_id == 0))
    def _():
        for t in range(FOX_RING_AHEAD):
            start_group(t)

    def attend(scores, values):
        m_prev = m_ref[...]
        m_new = m_prev
        for s in scores:
            m_new = jnp.maximum(m_new, jnp.max(s, axis=1, keepdims=True))
        corr = jnp.exp(m_prev - m_new)
        l_new = corr * l_ref[...]
        acc = corr[:, 0:1] * acc_ref[...]
        for s, pv in zip(scores, values):
            pr = jnp.exp(s - m_new)
            l_new = l_new + jnp.sum(pr, axis=1, keepdims=True)
            acc = acc + pv(pr.astype(BF16))
        l_ref[...] = l_new
        acc_ref[...] = acc
        m_ref[...] = m_new

    @pl.when(s_id == 0)
    def _():
        m_ref[...] = jnp.full_like(m_ref, NEG_BIG)
        l_ref[...] = jnp.zeros_like(l_ref)
        acc_ref[...] = jnp.zeros_like(acc_ref)
        suf_ref[...] = jnp.zeros_like(suf_ref)
        s = lax.dot_general(qbd_ref[...], kn_ref[...].astype(BF16), _NT, preferred_element_type=F32)
        s = s + (cq_ref[...] - jnp.tile(cn_ref[...], (rep, 1)))
        s = jnp.where(col <= row // FOX_HEADS, s, NEG_BIG)
        vn = vn_ref[...].astype(BF16)
        attend([s], [lambda pr: jnp.dot(pr, vn, preferred_element_type=F32)])

    @pl.when(s_id > 0)
    def _():
        t = b_id * groups_per_seq + (s_id - 1)
        slot = t % FOX_RING_SLOTS

        @pl.when(t + FOX_RING_AHEAD < total_groups)
        def _():
            start_group(t + FOX_RING_AHEAD)

        for c in group_copies(t, slot):
            c.wait()
        later = (_iota((PAGE, PAGE), 0) > _iota((PAGE, PAGE), 1)).astype(BF16)
        carry = suf_ref[...]
        qbd = qbd_ref[...]
        cq = cq_ref[...]
        scores, values = [], []
        lfs = [lfbuf[slot, i] for i in range(group)]
        local = [_mm_sel_r(lf, later) for lf in lfs]
        totals = [jnp.broadcast_to(loc[:, 0:1] + lf[:, 0:1], carry.shape) for loc, lf in zip(local, lfs)]
        for i in range(group):
            suffix = local[i] + carry
            carry = carry + totals[i]
            s = jnp.dot(qbd, kbuf[slot, i].astype(BF16), preferred_element_type=F32)
            scores.append(s + (cq + jnp.tile(suffix, (rep, 1))))
            values.append(lambda pr, i=i: lax.dot_general(pr, vbuf[slot, i].astype(BF16), _NT,
                                                          preferred_element_type=F32))
        suf_ref[...] = carry
        attend(scores, values)

    @pl.when(s_id == pl.num_programs(1) - 1)
    def _():
        lane_head = _iota((nq, FOX_DIM), 1) // HEAD_DIM
        row_head = _iota((nq, FOX_DIM), 0) % FOX_HEADS
        own = jnp.where(lane_head == row_head, acc_ref[...], 0.0)
        fold = (_iota((FOX_DIM, HEAD_DIM), 0) % HEAD_DIM == _iota((FOX_DIM, HEAD_DIM), 1)).astype(BF16)
        o_ref[...] = _mm_sel_r(own, fold) / l_ref[:, 0:HEAD_DIM]


def _fox_sample(page_table, qbd, cq, k_new, v_new, c_new, pool_k, pool_v, pool_lf):
    b, nq, _ = qbd.shape
    n_pages = page_table.shape[1]
    group = min(FOX_PAGE_GROUP, n_pages)
    assert n_pages % group == 0

    assert b * (n_pages // group) >= FOX_RING_AHEAD
    per_b = lambda shape: pl.BlockSpec((None,) + shape, lambda i, s, pt: (i, 0, 0))
    hbm = pl.BlockSpec(memory_space=pl.ANY)
    grid_spec = pltpu.PrefetchScalarGridSpec(
        num_scalar_prefetch=1, grid=(b, n_pages // group + 1),
        in_specs=[per_b((nq, FOX_DIM)), per_b((nq, LANES)), per_b((PAGE, FOX_DIM)), per_b((PAGE, FOX_DIM)),
                  per_b((FOX_HEADS, LANES)), hbm, hbm, hbm],
        out_specs=per_b((nq, HEAD_DIM)),
        scratch_shapes=[pltpu.VMEM((FOX_RING_SLOTS, group, FOX_DIM, PAGE), F32),
                        pltpu.VMEM((FOX_RING_SLOTS, group, FOX_DIM, PAGE), F32),
                        pltpu.VMEM((FOX_RING_SLOTS, group, FOX_HEADS, PAGE), F32),
                        pltpu.SemaphoreType.DMA((FOX_RING_SLOTS,)),
                        pltpu.VMEM((nq, LANES), F32), pltpu.VMEM((nq, LANES), F32),
                        pltpu.VMEM((nq, FOX_DIM), F32), pltpu.VMEM((FOX_HEADS, LANES), F32)])
    return pl.pallas_call(
        functools.partial(_fox_sample_kernel, group=group, n_pages=n_pages), grid_spec=grid_spec,
        out_shape=jax.ShapeDtypeStruct((b, nq, HEAD_DIM), F32),
        compiler_params=_cparams(("arbitrary", "arbitrary")), name="fox_sample")(
            page_table, qbd, cq, k_new, v_new, c_new, pool_k, pool_v, pool_lf)


def _ssm_kernel(z_ref, xbc_ref, fc_ref, fr_ref, buf_ref, s0_ref, cw_ref, cb_ref, dtb_c_ref, dtb_r_ref,
                a_c_ref, a_r_ref, d_ref, nw_ref, tl_ref, tu_ref, eall_ref, eexp_ref, y_ref, sout_ref, carry_ref, s_ref,
                *, valid):
    @pl.when(pl.program_id(1) == 0)
    def _():
        carry_ref[...] = buf_ref[...]
        s_ref[...] = s0_ref[...]

    raw = xbc_ref[...]
    cs = raw.shape[0]
    carry = carry_ref[...]
    cw = cw_ref[...]
    conv = (cw[0:1] * _shifted(raw, carry, 3) + cw[1:2] * _shifted(raw, carry, 2)
            + cw[2:3] * _shifted(raw, carry, 1) + cw[3:4] * raw)
    carry_ref[...] = raw[cs - SUBLANES:]
    xbc = _silu(conv + cb_ref[...])
    x = xbc[:, :SSM_DIM]

    dt_c = _softplus(fc_ref[...] + dtb_c_ref[...])
    dt_r = _softplus(fr_ref[...] + dtb_r_ref[...])
    if valid < cs:
        dt_c = jnp.where(_iota((cs, LANES), 0) < valid, dt_c, 0.0)
        dt_r = jnp.where(_iota((16, cs), 1) < valid, dt_r, 0.0)
    acum_c = _mm_sel_l(tl_ref[...], dt_c * a_c_ref[...])
    acum_r = _mm_sel_r(dt_r * a_r_ref[...], tu_ref[...])
    w_c = jnp.exp(acum_c[cs - 1:cs] - acum_c) * dt_c

    heads = range(SSM_HEADS)
    pairs = range(SSM_HEADS // 2)
    group_of = lambda p: (2 * p) // (SSM_HEADS // SSM_GROUPS)
    lane = _iota((cs, LANES), 1)
    causal = _iota((cs, cs), 0) >= _iota((cs, cs), 1)
    a_all = _mm_sel_r(acum_c, eall_ref[...])
    ea_wide = jnp.exp(_mm_sel_r(acum_c, eexp_ref[...]))
    w_wide = _mm_sel_r(w_c, eexp_ref[...])
    bm = [xbc[:, SSM_DIM + g * SSM_STATE:SSM_DIM + (g + 1) * SSM_STATE] for g in range(SSM_GROUPS)]
    cm = [xbc[:, SSM_DIM + (SSM_GROUPS + g) * SSM_STATE:SSM_DIM + (SSM_GROUPS + g + 1) * SSM_STATE]
          for g in range(SSM_GROUPS)]
    cb = [_mm(cm[g], bm[g], _NT) for g in range(SSM_GROUPS)]
    xp = [x[:, p * LANES:(p + 1) * LANES] for p in pairs]
    s_old = [s_ref[p] for p in pairs]
    a_col = [a_all[:, h * LANES:(h + 1) * LANES] for h in heads]
    gmat = [cb[group_of(h // 2)] * jnp.exp(jnp.where(causal, a_col[h] - acum_r[8 + h:9 + h, :], NEG_BIG))
            * dt_r[8 + h:9 + h, :] for h in heads]
    y_intra = [_mm(gmat[h], xp[h // 2]) for h in heads]
    y_inter = [_mm(cm[group_of(p)], s_old[p], _NT) for p in pairs]
    s_upd = [_mm(xp[p] * w_wide[:, p * LANES:(p + 1) * LANES], bm[group_of(p)], _TN) for p in pairs]
    y_pairs = []
    for p in pairs:
        dec = jnp.concatenate([jnp.broadcast_to(jnp.exp(a_col[2 * p + h][cs - 1:cs]), (HEAD_DIM, LANES))
                               for h in range(2)], axis=0)
        s_ref[p] = s_old[p] * dec + s_upd[p]
        y_pairs.append(jnp.where(lane < HEAD_DIM, y_intra[2 * p], y_intra[2 * p + 1])
                       + y_inter[p] * ea_wide[:, p * LANES:(p + 1) * LANES])

    y = jnp.concatenate(y_pairs, axis=1) + d_ref[...] * x
    y = y * _silu(z_ref[...])
    y_ref[...] = _rmsnorm(y, nw_ref[...])

    @pl.when(pl.program_id(1) == pl.num_programs(1) - 1)
    def _():
        sout_ref[...] = s_ref[...]


def _ssm(z, xbc, fc, fr, buf8, s0, cw8, cb, dtb_c, dtb_r, a_c, a_r, d_exp, nw, tri_l, tri_u, valid):
    b, l, _ = z.shape
    cs = LANES
    src = jnp.arange(LANES)[:, None]
    e_all = (src == 8 + jnp.arange(SSM_HEADS * LANES)[None, :] // LANES).astype(BF16)
    e_exp = (src == 8 + jnp.arange(SSM_DIM)[None, :] // HEAD_DIM).astype(BF16)
    full = lambda s: pl.BlockSpec(s, lambda i, j: (0,) * len(s))
    sspec = pl.BlockSpec((None, 4, LANES, LANES), lambda i, j: (i, 0, 0, 0))
    return pl.pallas_call(
        functools.partial(_ssm_kernel, valid=valid), grid=(b, l // cs),
        in_specs=[pl.BlockSpec((None, cs, SSM_DIM), lambda i, j: (i, j, 0)),
                  pl.BlockSpec((None, cs, SSM_CONV_DIM), lambda i, j: (i, j, 0)),
                  pl.BlockSpec((None, cs, LANES), lambda i, j: (i, j, 0)),
                  pl.BlockSpec((None, 16, cs), lambda i, j: (i, 0, j)),
                  pl.BlockSpec((None, SUBLANES, SSM_CONV_DIM), lambda i, j: (i, 0, 0)),
                  sspec,
                  full((SUBLANES, SSM_CONV_DIM)), full((1, SSM_CONV_DIM)), full((1, LANES)), full((16, LANES)),
                  full((1, LANES)), full((16, LANES)), full((1, SSM_DIM)), full((1, SSM_DIM)),
                  full((cs, cs)), full((cs, cs)), full((LANES, SSM_HEADS * LANES)), full((LANES, SSM_DIM))],
        out_specs=[pl.BlockSpec((None, cs, SSM_DIM), lambda i, j: (i, j, 0)), sspec],
        out_shape=[jax.ShapeDtypeStruct((b, l, SSM_DIM), F32), jax.ShapeDtypeStruct((b, 4, LANES, LANES), F32)],
        scratch_shapes=[pltpu.VMEM((SUBLANES, SSM_CONV_DIM), F32), pltpu.VMEM((4, LANES, LANES), F32)],
        compiler_params=_cparams(("parallel", "arbitrary")), name="ssm")(
            z, xbc, fc, fr, buf8, s0, cw8, cb, dtb_c, dtb_r, a_c, a_r, d_exp, nw, tri_l, tri_u, e_all, e_exp)


def _pad_rows(a, rows):
    return jnp.pad(a, ((0, 0), (rows - a.shape[1], 0), (0, 0)))


def _tri(n, seg):
    i = jnp.arange(n)
    same = (i[:, None] // seg) == (i[None, :] // seg)
    lower = (same & (i[:, None] >= i[None, :])).astype(BF16)
    return lower, lower.T


def _lane_vec(v, offset):
    n = v.shape[0]
    col = jnp.zeros((1, LANES), F32).at[0, offset:offset + n].set(v)
    row = jnp.zeros((16, LANES), F32).at[offset:offset + n, :].set(jnp.broadcast_to(v[:, None], (n, LANES)))
    return col, row


def _layer0(h, bsz, l, sc_buf, shift_prev, wkv0, p):
    m = bsz * l
    tl = min(512, l)
    z_sc, z_rw = _in_proj(h, p['g_mix'], p['w_in'], (3 * SC_DIM, RW_SHIFT_W))
    z_sc = z_sc.reshape(bsz, l, 3 * SC_DIM)
    z_rw = z_rw.reshape(bsz, l, RW_SHIFT_W)
    ya, sc_last = _sconv(z_sc, _pad_rows(sc_buf, SUBLANES), p['sc_w8'], tl)
    r, lw, kp, v, kn, kb, g = _rwkv_pre(z_rw, _pad_rows(shift_prev[:, None], SUBLANES), p['mu'], p['w0'], p['w2p'],
                                        p['a0'], p['a2p'], p['g2'], p['k_k'], p['k_a'], p['ones_bd'], tl)
    h_in = wkv0.reshape(bsz, 4, LANES, HEAD_DIM)
    lpad = -(-l // RW_CHUNK) * RW_CHUNK
    padl = lambda t: jnp.pad(t, ((0, 0), (0, lpad - l), (0, 0)))
    o, h_fin = _rwkv_chunk(padl(r), padl(lw), padl(kp), padl(v), padl(kn), padl(kb), h_in, *_rwkv_step_shape(bsz, lpad))
    o = o[:, :l]
    wkv_new = jnp.transpose(h_fin.reshape(bsz, 4, HEAD_DIM, 2, HEAD_DIM), (0, 1, 3, 4, 2))
    wkv_new = wkv_new.reshape(bsz, RW_HEADS, HEAD_DIM, HEAD_DIM)
    flat = lambda t: t.reshape(m, t.shape[-1])
    h = _out0(h, flat(ya), flat(o), flat(r), flat(kp), flat(v), flat(g), p['ln_w'], p['ln_b'], p['r_k'],
              p['ones_bd'], p['wo_a'], p['wo_b'])
    h = _ffn(h, p['g_ffn'], p['w_gate'], p['w_up'], p['w_down'], p['g_ffn'], False)
    return h, sc_last[:, SUBLANES - 2:], z_rw[:, -1], wkv_new


def _layer1(h, bsz, l, past, conv_buf, ssm0, p, g_final):
    m = bsz * l
    splits = (FOX_DIM, FOX_DIM, FOX_DIM, SSM_DIM, SSM_CONV_DIM, LANES)
    shape5 = (bsz, l, FOX_HEADS, HEAD_DIM)
    if past is None:
        q, k, v, zg, xbc, fdt, t_out = _in_proj(h, p['g_mix'], p['w_in'], splits, p['w_tkv'], (bsz, l))
        fr = t_out[:, 2 * FOX_DIM:]
        rows_t = lambda t: jnp.transpose(t.reshape(bsz, FOX_HEADS, HEAD_DIM, l), (0, 3, 1, 2))
        k_new, v_new = rows_t(t_out[:, :FOX_DIM]), rows_t(t_out[:, FOX_DIM:2 * FOX_DIM])
    else:
        q, k, v, zg, xbc, fdt, fdt_t = _in_proj(h, p['g_mix'], p['w_in'], splits, p['w_t'])
        fr = jnp.swapaxes(fdt_t.reshape(16, bsz, l), 0, 1)
        k_new, v_new = k.reshape(shape5), v.reshape(shape5)
    if past is None:
        tl = min(512, l)
        tri_l, tri_u = _tri(tl, tl)
        lf, cc, cr = _logf(fdt.reshape(bsz, l, LANES), fr, p['fb_c'], p['fb_r'], tri_l, tri_u, tl)
        qa, ka, vb = _fox_aug(q.reshape(bsz, l, FOX_DIM), k.reshape(bsz, l, FOX_DIM), v.reshape(bsz, l, FOX_DIM),
                              cc, tl)
        yf = _fox_prompt(qa, ka, vb, min(FOX_TQ, l), tl)
        logf_new = lf[..., :FOX_HEADS]
    else:
        page_table, pool_k, pool_v, pool_lf = past
        tri_l, tri_u = _tri(m, l)
        fr_flat = fdt_t.reshape(1, 16, m)
        lf, cc, cr = _logf(fdt.reshape(1, m, LANES), fr_flat, p['fb_c'], p['fb_r'], tri_l, tri_u, m)
        logf_new = lf.reshape(bsz, l, LANES)[..., :FOX_HEADS]
        q4 = q.reshape(bsz, l, FOX_HEADS, HEAD_DIM) * (HEAD_DIM ** -0.5)
        qbd = jnp.einsum('bqhd,hg->bqhgd', q4, jnp.eye(FOX_HEADS, dtype=F32))
        qbd = qbd.reshape(bsz, l * FOX_HEADS, FOX_DIM).astype(BF16)
        c_q = cc.reshape(bsz, l, LANES)[..., :FOX_HEADS].reshape(bsz, l * FOX_HEADS, 1)
        c_q = jnp.broadcast_to(c_q, (bsz, l * FOX_HEADS, LANES))
        c_new = jnp.pad(jnp.swapaxes(cr.reshape(16, bsz, l), 0, 1)[:, :FOX_HEADS], ((0, 0), (0, 0), (0, PAGE - l)))
        pad_new = lambda t: jnp.pad(t.reshape(bsz, l, FOX_DIM), ((0, 0), (0, PAGE - l), (0, 0)))
        yf = _fox_sample(page_table, qbd, c_q, pad_new(k), pad_new(v), c_new, pool_k, pool_v, pool_lf)
        yf = yf.reshape(bsz, l, FOX_DIM)
    ctl, ctu = _tri(SSD_CHUNK, SSD_CHUNK)
    xbc3 = xbc.reshape(bsz, l, SSM_CONV_DIM)
    if l % SSD_CHUNK == 0:
        lpad, valid = l, SSD_CHUNK
    else:
        assert l < SSD_CHUNK
        lpad, valid = SSD_CHUNK, l
    padl = lambda t: jnp.pad(t, ((0, 0), (0, lpad - l), (0, 0)))
    ym, s_fin = _ssm(padl(zg.reshape(bsz, l, SSM_DIM)), padl(xbc3), padl(fdt.reshape(bsz, l, LANES)),
                     jnp.pad(fr, ((0, 0), (0, 0), (0, lpad - l))),
                     _pad_rows(conv_buf, SUBLANES), ssm0.reshape(bsz, 4, LANES, LANES), p['cw8'], p['cb'],
                     p['dtb_c'], p['dtb_r'], p['a_c'], p['a_r'], p['d_exp'], p['nw'], ctl, ctu, valid)
    ym = ym[:, :l]
    h = _out1(h, yf.reshape(m, FOX_DIM), ym.reshape(m, SSM_DIM), p['wo_a'], p['wo_b'])
    h = _ffn(h, p['g_ffn'], p['w_gate'], p['w_up'], p['w_down'], g_final, True)
    return (h, k_new, v_new, logf_new, xbc3[:, l - 3:], s_fin.reshape(bsz, SSM_HEADS, HEAD_DIM, SSM_STATE))


def kernel(x_prompt, x_sample, state_sc, state_shift, state_wkv, cache_k, cache_v, cache_logf, state_ssm_conv, state_ssm, page_table, norm_mix, norm_ffn, norm_final, w_in0, sc_conv_w, rw_mu, rw_w0, rw_w2, rw_a0, rw_a2, rw_g2, rw_k_k, rw_k_a, rw_r_k, rw_ln_w, rw_ln_b, w_out0, w_in1, fox_f_bias, ssm_conv_w, ssm_conv_b, ssm_dt_bias, ssm_a_log, ssm_d, ssm_norm_w, w_out1, w_gate, w_up, w_down):
    bp, lp, _ = x_prompt.shape
    bs, ls, _ = x_sample.shape
    row = lambda v: v.reshape(1, -1).astype(F32)
    head_ids = jnp.arange(RW_DIM) // HEAD_DIM
    ones_bd = (head_ids[:, None] == head_ids[None, :]).astype(BF16)
    zpad = jnp.zeros((64, RW_DIM), F32)

    p0 = dict(
        g_mix=row(norm_mix[0]), g_ffn=row(norm_ffn[0]), w_in=w_in0[0].astype(BF16),
        sc_w8=jnp.pad(sc_conv_w[0], ((0, SUBLANES - sc_conv_w.shape[1]), (0, 0))),
        mu=row(rw_mu[0]), w0=row(rw_w0[0]), a0=row(rw_a0[0]),
        w2p=jnp.concatenate([rw_w2[0], zpad], axis=0).astype(BF16),
        a2p=jnp.concatenate([zpad, rw_a2[0]], axis=0).astype(BF16),
        g2=rw_g2[0].astype(BF16), k_k=row(rw_k_k[0]), k_a=row(rw_k_a[0]), r_k=row(rw_r_k[0]),
        ln_w=row(rw_ln_w[0]), ln_b=row(rw_ln_b[0]), ones_bd=ones_bd,
        wo_a=w_out0[0, :SC_DIM].astype(BF16), wo_b=w_out0[0, SC_DIM:].astype(BF16),
        w_gate=w_gate[0].astype(BF16), w_up=w_up[0].astype(BF16), w_down=w_down[0].astype(BF16))

    w1 = w_in1[0]
    c0 = 3 * FOX_DIM
    f_cols = w1[:, c0:c0 + FOX_HEADS]
    z_cols = w1[:, c0 + FOX_HEADS:c0 + FOX_HEADS + SSM_DIM]
    xbc_cols = w1[:, c0 + FOX_HEADS + SSM_DIM:c0 + FOX_HEADS + SSM_DIM + SSM_CONV_DIM]
    dt_cols = w1[:, c0 + FOX_HEADS + SSM_DIM + SSM_CONV_DIM:]
    fdt_cols = jnp.concatenate([f_cols, dt_cols, jnp.zeros((D_MODEL, LANES - 16), F32)], axis=1)
    w1r = jnp.concatenate([w1[:, :c0], z_cols, xbc_cols, fdt_cols], axis=1).astype(BF16)
    fb_c, fb_r = _lane_vec(fox_f_bias[0], 0)
    dtb_c, dtb_r = _lane_vec(ssm_dt_bias[0], 8)
    a_c, a_r = _lane_vec(-jnp.exp(ssm_a_log[0].astype(F32)), 8)
    p1 = dict(
        g_mix=row(norm_mix[1]), g_ffn=row(norm_ffn[1]), w_in=w1r, w_t=fdt_cols[:, :16].T.astype(BF16),
        w_tkv=jnp.concatenate([w1[:, FOX_DIM:3 * FOX_DIM], fdt_cols[:, :16]], axis=1).T.astype(BF16),
        fb_c=fb_c, fb_r=fb_r, dtb_c=dtb_c, dtb_r=dtb_r, a_c=a_c, a_r=a_r,
        cw8=jnp.pad(ssm_conv_w[0], ((0, SUBLANES - ssm_conv_w.shape[1]), (0, 0))), cb=row(ssm_conv_b[0]),
        d_exp=row(jnp.repeat(ssm_d[0], HEAD_DIM)), nw=row(ssm_norm_w[0]),
        wo_a=w_out1[0, :FOX_DIM].astype(BF16), wo_b=w_out1[0, FOX_DIM:].astype(BF16),
        w_gate=w_gate[1].astype(BF16), w_up=w_up[1].astype(BF16), w_down=w_down[1].astype(BF16))
    g_final = row(norm_final)

    n_pool = cache_k.shape[1]
    page_t = lambda c: jnp.transpose(c[0], (0, 2, 3, 1)).reshape(n_pool, FOX_DIM, PAGE)
    past = (page_table, page_t(cache_k), page_t(cache_v), jnp.swapaxes(cache_logf[0], 1, 2))

    hp = x_prompt.reshape(bp * lp, D_MODEL)
    hs = x_sample.reshape(bs * ls, D_MODEL)
    hp, sc_p, sh_p, wkv_p = _layer0(hp, bp, lp, jnp.zeros((bp, 2, SC_DIM), F32), jnp.zeros((bp, RW_SHIFT_W), F32),
                                    jnp.zeros((bp, RW_HEADS, HEAD_DIM, HEAD_DIM), F32), p0)
    hs, sc_s, sh_s, wkv_s = _layer0(hs, bs, ls, state_sc[0], state_shift[0], state_wkv[0], p0)
    yp, k_p, v_p, lf_p, cv_p, ssm_p = _layer1(hp, bp, lp, None, jnp.zeros((bp, 3, SSM_CONV_DIM), F32),
                                              jnp.zeros((bp, SSM_HEADS, HEAD_DIM, SSM_STATE), F32), p1, g_final)
    ys, k_s, v_s, lf_s, cv_s, ssm_s = _layer1(hs, bs, ls, past, state_ssm_conv[0], state_ssm[0], p1, g_final)
    one = lambda t: t[None]
    return (yp.reshape(bp, lp, D_MODEL), ys.reshape(bs, ls, D_MODEL),
            one(sc_p), one(sc_s), one(sh_p), one(sh_s), one(wkv_p), one(wkv_s),
            one(k_p), one(k_s), one(v_p), one(v_s), one(lf_p), one(lf_s), one(cv_p), one(cv_s),
            one(ssm_p), one(ssm_s))
```

```python
import functools

import jax
import jax.numpy as jnp
from jax import lax
from jax.experimental import pallas as pl
from jax.experimental.pallas import tpu as pltpu

F32 = jnp.float32
BF16 = jnp.bfloat16

D_MODEL = 1024
HEAD_DIM = 64
NORM_EPS = 1e-6
SC_DIM = 512
RW_DIM = 512
RW_HEADS = 8
RW_SHIFT_W = 3 * RW_DIM + 64 + 64 + 128
RW_GN_EPS = 64e-5
FOX_DIM = 512
FOX_HEADS = 8
SSM_DIM = 512
SSM_HEADS = 8
SSM_GROUPS = 2
SSM_STATE = 128
SSM_CONV_DIM = SSM_DIM + 2 * SSM_GROUPS * SSM_STATE
FFN_HIDDEN = 2816
PAGE = 128
SSD_CHUNK = 128
FOX_PAGE_GROUP = 8
FOX_RING_SLOTS = 3
FOX_RING_AHEAD = FOX_RING_SLOTS - 1
FOX_TQ = 1024
FOX_ROW_SPLIT = 2
RW_UNITS = 16
RW_CHUNK = 64

LANES = 128
SUBLANES = 8
VMEM_LIMIT = 56 * 1024 * 1024

NEG_BIG = -1e30
LOG2E = 1.4426950408889634


def _cparams(sem):
    return pltpu.CompilerParams(dimension_semantics=sem, vmem_limit_bytes=VMEM_LIMIT)


_NN = (((1,), (0,)), ((), ()))
_NT = (((1,), (1,)), ((), ()))
_TN = (((0,), (0,)), ((), ()))


def _mm(a, b, dims=_NN):
    return lax.dot_general(a.astype(BF16), b.astype(BF16), dims, preferred_element_type=F32)


def _split2(x):
    hi = x.astype(BF16)
    lo = (x - hi.astype(F32)).astype(BF16)
    return hi, lo


def _split3(x):
    hi = x.astype(BF16)
    r1 = x - hi.astype(F32)
    mid = r1.astype(BF16)
    lo = (r1 - mid.astype(F32)).astype(BF16)
    return hi, mid, lo


def _mm3(a, b, dims=_NN):
    ah, al = _split2(a)
    bh, bl = _split2(b)
    d = functools.partial(lax.dot_general, dimension_numbers=dims, preferred_element_type=F32)
    return d(ah, bh) + (d(al, bh) + d(ah, bl))


def _mm_sel_r(x, e, dims=_NN):
    h, m, l = _split3(x)
    d = functools.partial(lax.dot_general, dimension_numbers=dims, preferred_element_type=F32)
    return d(h, e) + (d(m, e) + d(l, e))


def _mm_sum_r(x, e, dims=_NN):
    h, l = _split2(x)
    d = functools.partial(lax.dot_general, dimension_numbers=dims, preferred_element_type=F32)
    return d(h, e) + d(l, e)


def _mm_sel_l(e, x, dims=_NN):
    h, m, l = _split3(x)
    d = functools.partial(lax.dot_general, dimension_numbers=dims, preferred_element_type=F32)
    return d(e, h) + (d(e, m) + d(e, l))


def _rmsnorm(x, g):
    return x * lax.rsqrt(jnp.mean(x * x, axis=-1, keepdims=True) + NORM_EPS) * g


def _sigmoid(x):
    return 1.0 / (1.0 + jnp.exp(-x))


def _silu(x):
    return x * _sigmoid(x)


def _softplus(x):
    return jnp.maximum(x, 0.0) + jnp.log(1.0 + jnp.exp(-jnp.abs(x)))


def _iota(shape, dim):
    return lax.broadcasted_iota(jnp.int32, shape, dim)


def _shifted(u, carry8, d):
    tl = u.shape[0]
    s = pltpu.roll(u, d, 0)
    c = pltpu.roll(carry8, d, 0)
    row = _iota((SUBLANES, u.shape[1]), 0)
    head = jnp.where(row < d, c, s[:SUBLANES])
    if tl == SUBLANES:
        return head
    return jnp.concatenate([head, s[SUBLANES:]], axis=0)


def _in_proj_kernel(*refs, splits, has_t):
    x_ref, g_ref, w_ref = refs[:3]
    pos = 3
    if has_t:
        wt_ref = refs[pos]
        pos += 1
    outs = refs[pos:pos + len(splits)]
    nb = _rmsnorm(x_ref[...], g_ref[...]).astype(BF16)
    off = 0
    for o_ref, width in zip(outs, splits):
        o_ref[...] = jnp.dot(nb, w_ref[:, off:off + width], preferred_element_type=F32)
        off += width
    if has_t:
        refs[pos + len(splits)][...] = lax.dot_general(wt_ref[...], nb, _NT, preferred_element_type=F32)


def _in_proj(x, g, w, splits, wt=None, seqs=None):
    m = x.shape[0]
    tm = min(512, m)
    n = sum(splits)
    in_specs = [pl.BlockSpec((tm, D_MODEL), lambda i: (i, 0)),
                pl.BlockSpec((1, D_MODEL), lambda i: (0, 0)),
                pl.BlockSpec((D_MODEL, n), lambda i: (0, 0))]
    args = [x, g, w]
    out_shape = [jax.ShapeDtypeStruct((m, s), F32) for s in splits]
    out_specs = [pl.BlockSpec((tm, s), lambda i: (i, 0)) for s in splits]
    if wt is not None:
        rows = wt.shape[0]
        in_specs.append(pl.BlockSpec((rows, D_MODEL), lambda i: (0, 0)))
        args.append(wt)
        if seqs is None:
            out_shape.append(jax.ShapeDtypeStruct((rows, m), F32))
            out_specs.append(pl.BlockSpec((rows, tm), lambda i: (0, i)))
        else:
            bsz, l = seqs
            per_seq = l // tm
            out_shape.append(jax.ShapeDtypeStruct((bsz, rows, l), F32))
            out_specs.append(pl.BlockSpec((None, rows, tm), lambda i: (i // per_seq, 0, i % per_seq)))
    return pl.pallas_call(
        functools.partial(_in_proj_kernel, splits=tuple(splits), has_t=wt is not None),
        grid=(m // tm,), in_specs=in_specs, out_specs=out_specs, out_shape=out_shape,
        compiler_params=_cparams(("parallel",)), name="in_proj")(*args)


def _ffn_kernel(x_ref, g_ref, wg_ref, wu_ref, wd_ref, gf_ref, o_ref, xn_ref, acc_ref, *, final_norm):
    j = pl.program_id(1)

    @pl.when(j == 0)
    def _():
        x = x_ref[...]
        xn_ref[...] = _rmsnorm(x, g_ref[...]).astype(BF16)
        acc_ref[...] = x

    xn = xn_ref[...]
    gate = jnp.dot(xn, wg_ref[...], preferred_element_type=F32)
    up = jnp.dot(xn, wu_ref[...], preferred_element_type=F32)
    act = (_silu(gate) * up).astype(BF16)
    acc_ref[...] += jnp.dot(act, wd_ref[...], preferred_element_type=F32)

    @pl.when(j == pl.num_programs(1) - 1)
    def _():
        y = acc_ref[...]
        if final_norm:
            y = _rmsnorm(y, gf_ref[...])
        o_ref[...] = y


def _ffn(x, g, wg, wu, wd, gfin, final_norm):
    m = x.shape[0]
    tm = min(512, m)
    th = FFN_HIDDEN // 2
    return pl.pallas_call(
        functools.partial(_ffn_kernel, final_norm=final_norm),
        grid=(m // tm, FFN_HIDDEN // th),
        in_specs=[pl.BlockSpec((tm, D_MODEL), lambda i, j: (i, 0)),
                  pl.BlockSpec((1, D_MODEL), lambda i, j: (0, 0)),
                  pl.BlockSpec((D_MODEL, th), lambda i, j: (0, j)),
                  pl.BlockSpec((D_MODEL, th), lambda i, j: (0, j)),
                  pl.BlockSpec((th, D_MODEL), lambda i, j: (j, 0)),
                  pl.BlockSpec((1, D_MODEL), lambda i, j: (0, 0))],
        out_specs=pl.BlockSpec((tm, D_MODEL), lambda i, j: (i, 0)),
        out_shape=jax.ShapeDtypeStruct((m, D_MODEL), F32),
        scratch_shapes=[pltpu.VMEM((tm, D_MODEL), BF16), pltpu.VMEM((tm, D_MODEL), F32)],
        compiler_params=_cparams(("parallel", "arbitrary")), name="ffn")(x, g, wg, wu, wd, gfin)


def _sconv_kernel(z_ref, buf_ref, w_ref, y_ref, last_ref, carry_ref):
    @pl.when(pl.program_id(1) == 0)
    def _():
        carry_ref[...] = buf_ref[...]

    z = z_ref[...]
    tl = z.shape[0]
    gb, gc, h = z[:, :SC_DIM], z[:, SC_DIM:2 * SC_DIM], z[:, 2 * SC_DIM:]
    u = gc * h
    carry = carry_ref[...]
    w = w_ref[...]
    y = w[0:1] * _shifted(u, carry, 2) + w[1:2] * _shifted(u, carry, 1) + w[2:3] * u
    y_ref[...] = (gb * y).astype(y_ref.dtype)
    tail = u[tl - SUBLANES:]
    carry_ref[...] = tail
    last_ref[...] = tail


def _sconv(z, buf8, w8, tl):
    b, l, _ = z.shape
    return pl.pallas_call(
        _sconv_kernel, grid=(b, l // tl),
        in_specs=[pl.BlockSpec((None, tl, 3 * SC_DIM), lambda i, j: (i, j, 0)),
                  pl.BlockSpec((None, SUBLANES, SC_DIM), lambda i, j: (i, 0, 0)),
                  pl.BlockSpec((SUBLANES, SC_DIM), lambda i, j: (0, 0))],
        out_specs=[pl.BlockSpec((None, tl, SC_DIM), lambda i, j: (i, j, 0)),
                   pl.BlockSpec((None, SUBLANES, SC_DIM), lambda i, j: (i, 0, 0))],
        out_shape=[jax.ShapeDtypeStruct((b, l, SC_DIM), BF16 if tl % 16 == 0 else F32),
                   jax.ShapeDtypeStruct((b, SUBLANES, SC_DIM), F32)],
        scratch_shapes=[pltpu.VMEM((SUBLANES, SC_DIM), F32)],
        compiler_params=_cparams(("parallel", "arbitrary")), name="sconv")(z, buf8, w8)


def _rwkv_pre_kernel(z_ref, sh_ref, mu_ref, w0_ref, w2_ref, a0_ref, a2_ref, g2_ref, kk_ref, ka_ref, ones_ref,
                     r_o, lw_o, kp_o, v_o, kn_o, kb_o, g_o, carry_ref):
    @pl.when(pl.program_id(1) == 0)
    def _():
        carry_ref[...] = sh_ref[...]

    z = z_ref[...]
    tl = z.shape[0]
    zs = _shifted(z, carry_ref[...], 1)
    carry_ref[...] = z[tl - SUBLANES:]
    zx = z + mu_ref[...] * (zs - z)
    r = zx[:, :RW_DIM]
    k = zx[:, RW_DIM:2 * RW_DIM]
    v = zx[:, 2 * RW_DIM:3 * RW_DIM]
    wa = zx[:, 3 * RW_DIM:3 * RW_DIM + 128]
    gd = zx[:, 3 * RW_DIM + 128:]
    w_log = -_softplus(-(w0_ref[...] + _mm(jnp.tanh(wa), w2_ref[...]))) - 0.5
    lw_o[...] = -jnp.exp(w_log)
    alpha = _sigmoid(a0_ref[...] + _mm(wa, a2_ref[...]))
    g_o[...] = _mm(_sigmoid(gd), g2_ref[...])
    kk = k * kk_ref[...]
    n2 = _mm_sum_r(kk * kk, ones_ref[...])
    kk = kk / jnp.maximum(jnp.sqrt(n2), 1e-12)
    r_o[...] = r
    kp_o[...] = k * (1.0 + (alpha - 1.0) * ka_ref[...])
    v_o[...] = v
    kn_o[...] = kk
    kb_o[...] = kk * alpha


def _rwkv_pre(z, sh8, mu, w0, w2p, a0, a2p, g2, k_k, k_a, ones_bd, tl):
    b, l, _ = z.shape
    full = lambda s: pl.BlockSpec(s, lambda i, j: (0,) * len(s))
    out = jax.ShapeDtypeStruct((b, l, RW_DIM), F32)
    ospec = pl.BlockSpec((None, tl, RW_DIM), lambda i, j: (i, j, 0))
    return pl.pallas_call(
        _rwkv_pre_kernel, grid=(b, l // tl),
        in_specs=[pl.BlockSpec((None, tl, RW_SHIFT_W), lambda i, j: (i, j, 0)),
                  pl.BlockSpec((None, SUBLANES, RW_SHIFT_W), lambda i, j: (i, 0, 0)),
                  full((1, RW_SHIFT_W)), full((1, RW_DIM)), full((128, RW_DIM)), full((1, RW_DIM)),
                  full((128, RW_DIM)), full((128, RW_DIM)), full((1, RW_DIM)), full((1, RW_DIM)),
                  full((RW_DIM, RW_DIM))],
        out_specs=[ospec] * 7, out_shape=[out] * 7,
        scratch_shapes=[pltpu.VMEM((SUBLANES, RW_SHIFT_W), F32)],
        compiler_params=_cparams(("parallel", "arbitrary")), name="rwkv_pre")(
            z, sh8, mu, w0, w2p, a0, a2p, g2, k_k, k_a, ones_bd)


def _unit_lower_inverse(ns, row, col, t):
    eye = (row == col).astype(F32)
    lower = row > col
    same8 = (row // 8) == (col // 8)
    n8 = [jnp.where(same8 & lower, n, 0.0) for n in ns]
    p2 = [_mm(a, a) for a in n8]
    p4 = [_mm(a, a) for a in p2]
    xs = [_mm(eye + a, eye + b) for a, b in zip(n8, p2)]
    xs = [_mm(x, eye + b) for x, b in zip(xs, p4)]
    s = 16
    while s <= t:
        level = ((row // s) == (col // s)) & ((row // (s // 2)) != (col // (s // 2))) & lower
        xc = [_mm(x, jnp.where(level, n, 0.0)) for x, n in zip(xs, ns)]
        xs = [x + _mm(y, x) for x, y in zip(xs, xc)]
        s *= 2
    return xs


def _rwkv_chunk_math(r, lw, k, v, kn, kb, h0s, tri, t):
    chunks = range(r.shape[0] // t)
    pairs = range(len(h0s))
    units = [(c, p) for c in chunks for p in pairs]
    lane_a = _iota((t, LANES), 1) < HEAD_DIM
    row2 = _iota((2 * t, 2 * t), 0)
    col2 = _iota((2 * t, 2 * t), 1)
    rowh = _iota((LANES, LANES), 0)
    colh = _iota((LANES, LANES), 1)
    cum = _mm_sel_l(tri, lw)
    last = [cum[(c + 1) * t - 1:(c + 1) * t] for c in chunks]
    cum_end = jnp.concatenate([jnp.broadcast_to(x, (t, x.shape[1])) for x in last], axis=0)
    g_inv = jnp.exp(-cum)
    g_end = jnp.exp(cum_end - cum)
    a_t = -kn * jnp.exp(cum - lw)
    r_t = r * jnp.exp(cum)
    b_t = kb * g_inv
    k_t = k * g_inv
    b_h = kb * g_end
    k_h = k * g_end

    def part(x, c, p):
        return x[c * t:(c + 1) * t, p * LANES:(p + 1) * LANES]

    def stack(x):
        return [jnp.concatenate([jnp.where(lane_a, part(x, c, p), 0.0), jnp.where(lane_a, 0.0, part(x, c, p))],
                                axis=0) for c, p in units]

    la, lr, lb, lk, vbd = stack(a_t), stack(r_t), stack(b_t), stack(k_t), stack(v)
    every = range(len(units))
    same_head = (row2 // t) == (col2 // t)
    strict = same_head & (row2 > col2)
    incl = same_head & (row2 >= col2)
    n = [jnp.where(strict, _mm(la[i], lb[i], _NT), 0.0) for i in every]
    aak = [jnp.where(strict, _mm(la[i], lk[i], _NT), 0.0) for i in every]
    arb = [jnp.where(incl, _mm(lr[i], lb[i], _NT), 0.0) for i in every]
    ark = [jnp.where(incl, _mm(lr[i], lk[i], _NT), 0.0) for i in every]
    aakv = [_mm(aak[i], vbd[i]) for i in every]
    arkv = [_mm(ark[i], vbd[i]) for i in every]
    inv = _unit_lower_inverse(n, row2, col2, t)
    same_h = (rowh // HEAD_DIM) == (colh // HEAD_DIM)
    hs = list(h0s)
    o_rows = []
    for c in chunks:
        idx = [c * len(pairs) + p for p in pairs]
        rhs = [_mm3(la[i], hs[p]) + aakv[i] for p, i in zip(pairs, idx)]
        u0 = [_mm(inv[i], x) for i, x in zip(idx, rhs)]
        res = [x - y + _mm3(n[i], y) for i, x, y in zip(idx, rhs, u0)]
        u = [y + _mm(inv[i], z) for i, y, z in zip(idx, u0, res)]
        o = [_mm(lr[i], hs[p]) + _mm(arb[i], u[p]) + arkv[i] for p, i in zip(pairs, idx)]
        upd = [_mm(jnp.concatenate([part(b_h, c, p), part(k_h, c, p)], axis=0),
                    jnp.concatenate([u[p][:t] + u[p][t:], part(v, c, p)], axis=0), _TN) for p in pairs]
        g_t = jnp.exp(last[c])
        hs = [jnp.broadcast_to(g_t[:, p * LANES:(p + 1) * LANES], (LANES, LANES)).T * hs[p]
              + jnp.where(same_h, upd[p], 0.0) for p in pairs]
        o_rows.append(jnp.concatenate([o[p][:t] + o[p][t:] for p in pairs], axis=1))
    return jnp.concatenate(o_rows, axis=0), hs


def _rwkv_chunk_kernel(r_ref, lw_ref, k_ref, v_ref, kn_ref, kb_ref, h0_ref, tri_ref, o_ref, hout_ref, h_ref):
    bb = r_ref.shape[0]
    npair = RW_DIM // LANES

    @pl.when(pl.program_id(1) == 0)
    def _():
        eye = (_iota((HEAD_DIM, HEAD_DIM), 0) == _iota((HEAD_DIM, HEAD_DIM), 1)).astype(BF16)
        lane_a = _iota((HEAD_DIM, LANES), 1) < HEAD_DIM
        for b in range(bb):
            for p in range(npair):
                tr = _mm_sel_l(eye, h0_ref[b, p], _NT)
                h_ref[b, p] = jnp.concatenate([jnp.where(lane_a, tr, 0.0), jnp.where(lane_a, 0.0, tr)], axis=0)

    wide = lambda ref: jnp.concatenate([ref[b] for b in range(bb)], axis=1)
    h0s = [h_ref[b, p] for b in range(bb) for p in range(npair)]
    o_wide, h_new = _rwkv_chunk_math(wide(r_ref), wide(lw_ref), wide(k_ref), wide(v_ref), wide(kn_ref),
                                     wide(kb_ref), h0s, tri_ref[...], RW_CHUNK)
    for b in range(bb):
        o_ref[b] = o_wide[:, b * RW_DIM:(b + 1) * RW_DIM]
        for p in range(npair):
            hn = h_new[b * npair + p]
            h_ref[b, p] = hn
            hout_ref[b, p] = hn[:HEAD_DIM] + hn[HEAD_DIM:]


def _rwkv_step_shape(bsz, l):
    pairs = RW_DIM // LANES
    chunks = 2 if l % (2 * RW_CHUNK) == 0 else 1
    bb = max(1, RW_UNITS // (pairs * chunks))
    while bsz % bb:
        bb //= 2
    return bb, chunks * RW_CHUNK


def _rwkv_chunk(r, lw, kp, v, kn, kb, h0, bb, t):
    b, l, _ = r.shape
    assert b % bb == 0 and l % t == 0 and t % RW_CHUNK == 0
    tri = _tri(t, RW_CHUNK)[0]
    xspec = pl.BlockSpec((bb, t, RW_DIM), lambda i, j: (i, j, 0))
    return pl.pallas_call(
        _rwkv_chunk_kernel, grid=(b // bb, l // t),
        in_specs=[xspec] * 6 + [pl.BlockSpec((bb, 4, LANES, HEAD_DIM), lambda i, j: (i, 0, 0, 0)),
                                pl.BlockSpec((t, t), lambda i, j: (0, 0))],
        out_specs=[xspec, pl.BlockSpec((bb, 4, HEAD_DIM, LANES), lambda i, j: (i, 0, 0, 0))],
        out_shape=[jax.ShapeDtypeStruct((b, l, RW_DIM), F32), jax.ShapeDtypeStruct((b, 4, HEAD_DIM, LANES), F32)],
        scratch_shapes=[pltpu.VMEM((bb, 4, LANES, LANES), F32)],
        compiler_params=_cparams(("parallel", "arbitrary")), name="rwkv_chunk")(r, lw, kp, v, kn, kb, h0, tri)


def _out0_kernel(x_ref, ya_ref, o_ref, r_ref, kp_ref, v_ref, g_ref, lnw_ref, lnb_ref, rk_ref, ones_ref,
                 wa_ref, wb_ref, out_ref):
    ones = ones_ref[...]
    o = o_ref[...]
    mean = _mm_sum_r(o, ones) * (1.0 / HEAD_DIM)
    d = o - mean
    var = _mm_sum_r(d * d, ones) * (1.0 / HEAD_DIM)
    on = d * lax.rsqrt(var + RW_GN_EPS) * lnw_ref[...] + lnb_ref[...]
    bonus = _mm_sum_r(r_ref[...] * kp_ref[...] * rk_ref[...], ones) * v_ref[...]
    yb = (on + bonus) * g_ref[...]
    out_ref[...] = x_ref[...] + _mm(ya_ref[...], wa_ref[...]) + _mm(yb, wb_ref[...])


def _out0(x, ya, o, r, kp, v, g, lnw, lnb, rk, ones_bd, wa, wb):
    m = x.shape[0]
    tm = min(512, m)
    row = lambda n: pl.BlockSpec((tm, n), lambda i: (i, 0))
    full = lambda s: pl.BlockSpec(s, lambda i: (0, 0))
    return pl.pallas_call(
        _out0_kernel, grid=(m // tm,),
        in_specs=[row(D_MODEL)] + [row(RW_DIM)] * 6 + [full((1, RW_DIM))] * 3 + [full((RW_DIM, RW_DIM))]
                 + [full((SC_DIM, D_MODEL)), full((RW_DIM, D_MODEL))],
        out_specs=row(D_MODEL), out_shape=jax.ShapeDtypeStruct((m, D_MODEL), F32),
        compiler_params=_cparams(("parallel",)), name="out0")(x, ya, o, r, kp, v, g, lnw, lnb, rk, ones_bd, wa, wb)


def _out1_kernel(x_ref, a_ref, b_ref, wa_ref, wb_ref, out_ref):
    out_ref[...] = x_ref[...] + _mm(a_ref[...], wa_ref[...]) + _mm(b_ref[...], wb_ref[...])


def _out1(x, a, b, wa, wb):
    m = x.shape[0]
    tm = min(512, m)
    row = lambda n: pl.BlockSpec((tm, n), lambda i: (i, 0))
    full = lambda s: pl.BlockSpec(s, lambda i: (0, 0))
    return pl.pallas_call(
        _out1_kernel, grid=(m // tm,),
        in_specs=[row(D_MODEL), row(FOX_DIM), row(SSM_DIM), full((FOX_DIM, D_MODEL)), full((SSM_DIM, D_MODEL))],
        out_specs=row(D_MODEL), out_shape=jax.ShapeDtypeStruct((m, D_MODEL), F32),
        compiler_params=_cparams(("parallel",)), name="out1")(x, a, b, wa, wb)


def _logf_kernel(fc_ref, fr_ref, bc_ref, br_ref, tl_ref, tu_ref, lf_o, cc_o, cr_o, carry_c, carry_r):
    @pl.when(pl.program_id(1) == 0)
    def _():
        carry_c[...] = jnp.zeros_like(carry_c)
        carry_r[...] = jnp.zeros_like(carry_r)

    tl = fc_ref.shape[0]
    lf_c = -_softplus(-(fc_ref[...] + bc_ref[...]))
    lf_o[...] = lf_c
    cc = _mm_sel_l(tl_ref[...], lf_c) + carry_c[0:1]
    cc_o[...] = cc
    carry_c[...] = jnp.broadcast_to(cc[tl - 1:tl], carry_c.shape)
    rep = tl // LANES
    lf_r = -_softplus(-(fr_ref[...] + jnp.tile(br_ref[...], (1, rep))))
    cr = _mm_sel_r(lf_r, tu_ref[...]) + jnp.tile(carry_r[...], (1, rep))
    cr_o[...] = cr
    carry_r[...] = jnp.broadcast_to(cr[:, tl - 1:tl], carry_r.shape)


def _logf(fc, fr, bias_c, bias_r, tri_l, tri_u, tl):
    b, l, _ = fc.shape
    cspec = pl.BlockSpec((None, tl, LANES), lambda i, j: (i, j, 0))
    rspec = pl.BlockSpec((None, 16, tl), lambda i, j: (i, 0, j))
    full = lambda s: pl.BlockSpec(s, lambda i, j: (0, 0))
    return pl.pallas_call(
        _logf_kernel, grid=(b, l // tl),
        in_specs=[cspec, rspec, full((1, LANES)), full((16, LANES)), full((tl, tl)), full((tl, tl))],
        out_specs=[cspec, cspec, rspec],
        out_shape=[jax.ShapeDtypeStruct((b, l, LANES), F32), jax.ShapeDtypeStruct((b, l, LANES), F32),
                   jax.ShapeDtypeStruct((b, 16, l), F32)],
        scratch_shapes=[pltpu.VMEM((SUBLANES, LANES), F32), pltpu.VMEM((16, LANES), F32)],
        compiler_params=_cparams(("parallel", "arbitrary")), name="logf")(fc, fr, bias_c, bias_r, tri_l, tri_u)


def _fox_aug_kernel(q_ref, k_ref, v_ref, cc_ref, wq_ref, wk_ref, oq_ref, ok_ref, qa_o, ka_o, va_o, vb_o):
    c_hi, c_mid, c_lo = _split3(cc_ref[...] * LOG2E)
    q = q_ref[...] * (LOG2E * HEAD_DIM ** -0.5)
    k = k_ref[...]
    for p in range(FOX_DIM // LANES):
        sl = slice(p * LANES, (p + 1) * LANES)
        xq = jnp.concatenate([q[:, sl].astype(BF16), c_hi, c_mid, c_lo], axis=1)
        xk = jnp.concatenate([k[:, sl].astype(BF16), c_hi, c_mid, c_lo], axis=1)
        for h in range(2):
            hh = 2 * p + h
            qa_o[hh] = (jnp.dot(xq, wq_ref[hh], preferred_element_type=F32) + oq_ref[...]).astype(BF16)
            ka_o[hh] = (jnp.dot(xk, wk_ref[hh], preferred_element_type=F32) + ok_ref[...]).astype(BF16)
    v = v_ref[...]
    first = (_iota(v.shape, 1) % LANES) < HEAD_DIM
    va_o[...] = jnp.where(first, v, 1.0).astype(BF16)
    vb_o[...] = jnp.where(first, 1.0, v).astype(BF16)


def _fox_aug(q, k, v, cc, tl):
    b, l, _ = q.shape
    head = jnp.arange(FOX_HEADS)
    src = jnp.arange(4 * LANES)
    dst = jnp.arange(LANES)
    vec = (src[None, :, None] < LANES) & (src[None, :, None] - (head[:, None, None] % 2) * HEAD_DIM == dst[None, None, :]) \
        & (dst[None, None, :] < HEAD_DIM)
    part = (src[None, :, None] - LANES) // LANES
    is_c = (src[None, :, None] >= LANES) & ((src[None, :, None] % LANES) == head[:, None, None])
    wq = (vec | (is_c & (dst[None, None, :] == HEAD_DIM + part))).astype(BF16)
    wk = vec.astype(BF16) - (is_c & (dst[None, None, :] == HEAD_DIM + 3 + part)).astype(BF16)
    ones_q = ((dst >= HEAD_DIM + 3) & (dst < HEAD_DIM + 6)).astype(F32)[None]
    ones_k = ((dst >= HEAD_DIM) & (dst < HEAD_DIM + 3)).astype(F32)[None]
    xspec = pl.BlockSpec((None, tl, FOX_DIM), lambda i, j: (i, j, 0))
    aspec = pl.BlockSpec((None, FOX_HEADS, tl, LANES), lambda i, j: (i, 0, j, 0))
    full = lambda s: pl.BlockSpec(s, lambda i, j: (0,) * len(s))
    aug = jax.ShapeDtypeStruct((b, FOX_HEADS, l, LANES), BF16)
    return pl.pallas_call(
        _fox_aug_kernel, grid=(b, l // tl),
        in_specs=[xspec, xspec, xspec, pl.BlockSpec((None, tl, LANES), lambda i, j: (i, j, 0)),
                  full((FOX_HEADS, 4 * LANES, LANES)), full((FOX_HEADS, 4 * LANES, LANES)),
                  full((1, LANES)), full((1, LANES))],
        out_specs=[aspec, aspec, xspec, xspec],
        out_shape=[aug, aug] + [jax.ShapeDtypeStruct((b, l, FOX_DIM), BF16)] * 2,
        compiler_params=_cparams(("parallel", "parallel")), name="fox_aug")(q, k, v, cc, wq, wk, ones_q, ones_k)


def _fox_prompt_kernel(qi_ref, ki_ref, qa_ref, ka_ref, va_ref, vb_ref, o_ref, m_ref, acc_ref):
    step = pl.program_id(2)
    qi = qi_ref[step]
    ki = ki_ref[step]
    tq = qa_ref.shape[1]
    tk = ka_ref.shape[1]
    rep = tk // LANES
    half = tq // FOX_ROW_SPLIT

    first_q = qi * tq
    first_k = ki * tk

    @pl.when(ki == 0)
    def _():
        m_ref[...] = jnp.full_like(m_ref, NEG_BIG)
        acc_ref[...] = jnp.zeros_like(acc_ref)

    def update(diagonal, first_row):
        vs = (va_ref[...], vb_ref[...])
        units = [(h, r * half) for h in range(2) for r in range(FOX_ROW_SPLIT) if r * half >= first_row]
        s = [lax.dot_general(qa_ref[h, r0:r0 + half], ka_ref[h], _NT, preferred_element_type=F32)
             for h, r0 in units]
        if diagonal:
            ahead = first_q - first_k
            s = [jnp.where(ahead + r0 + _iota((half, tk), 0) >= _iota((half, tk), 1), x, NEG_BIG)
                 for x, (h, r0) in zip(s, units)]
        m_prev = [m_ref[h, r0:r0 + half] for h, r0 in units]
        m_new = [jnp.maximum(mp, jnp.max(x, axis=1, keepdims=True)) for mp, x in zip(m_prev, s)]
        pr = [jnp.exp2(x - jnp.tile(mn, (1, rep))) for x, mn in zip(s, m_new)]
        pv = [jnp.dot(x.astype(BF16), vs[h], preferred_element_type=F32) for x, (h, r0) in zip(pr, units)]
        for (h, r0), mp, mn, y in zip(units, m_prev, m_new, pv):
            rows = slice(r0, r0 + half)
            acc_ref[h, rows] = jnp.exp2(mp - mn) * acc_ref[h, rows] + y
            m_ref[h, rows] = mn

    @pl.when(first_k + tk <= first_q)
    def _():
        update(False, 0)

    for j in range(tq // tk):
        @pl.when(first_k == first_q + j * tk)
        def _(j=j):
            update(True, (j * tk // half) * half)

    @pl.when(first_k + tk == first_q + tq)
    def _():
        first = _iota((tq, LANES), 1) < HEAD_DIM
        acc_a, acc_b = acc_ref[0], acc_ref[1]
        value = jnp.where(first, acc_a, acc_b)
        total = jnp.where(first, pltpu.roll(acc_a, HEAD_DIM, 1), pltpu.roll(acc_b, HEAD_DIM, 1))
        o_ref[...] = (value / total).astype(o_ref.dtype)


def _fox_prompt(qa, ka, va, vb, tq, tk):
    b, _, l, _ = qa.shape
    assert tq % tk == 0 and l % tq == 0
    pairs = [(qi, ki) for qi in range(l // tq) for ki in range((qi + 1) * tq // tk)]
    qi_of = jnp.array([p[0] for p in pairs], jnp.int32)
    ki_of = jnp.array([p[1] for p in pairs], jnp.int32)
    vspec = pl.BlockSpec((None, tk, LANES), lambda i, p, s, qi, ki: (i, ki[s], p))
    grid_spec = pltpu.PrefetchScalarGridSpec(
        num_scalar_prefetch=2, grid=(b, FOX_DIM // LANES, len(pairs)),
        in_specs=[pl.BlockSpec((None, 2, tq, LANES), lambda i, p, s, qi, ki: (i, p, qi[s], 0)),
                  pl.BlockSpec((None, 2, tk, LANES), lambda i, p, s, qi, ki: (i, p, ki[s], 0)),
                  vspec, vspec],
        out_specs=pl.BlockSpec((None, tq, LANES), lambda i, p, s, qi, ki: (i, qi[s], p)),
        scratch_shapes=[pltpu.VMEM((2, tq, LANES), F32), pltpu.VMEM((2, tq, LANES), F32)])
    return pl.pallas_call(
        _fox_prompt_kernel, grid_spec=grid_spec,
        out_shape=jax.ShapeDtypeStruct((b, l, FOX_DIM), BF16),
        compiler_params=_cparams(("parallel", "parallel", "arbitrary")), name="fox_prompt")(
            qi_of, ki_of, qa, ka, va, vb)


def _fox_sample_kernel(pt_ref, qbd_ref, cq_ref, kn_ref, vn_ref, cn_ref, pool_k, pool_v, pool_lf, o_ref,
                       kbuf, vbuf, lfbuf, sem, m_ref, l_ref, acc_ref, suf_ref, *, group, n_pages):
    b_id = pl.program_id(0)
    s_id = pl.program_id(1)
    nq = qbd_ref.shape[0]
    rep = nq // FOX_HEADS
    row = _iota((nq, PAGE), 0)
    col = _iota((nq, PAGE), 1)
    groups_per_seq = n_pages // group
    total_groups = pl.num_programs(0) * groups_per_seq

    def group_copies(t, slot):
        seq = t // groups_per_seq
        first = (t % groups_per_seq) * group
        out = []
        for i in range(group):
            pg = pt_ref[seq, n_pages - 1 - (first + i)]
            out.append(pltpu.make_async_copy(pool_k.at[pg], kbuf.at[slot, i], sem.at[slot]))
            out.append(pltpu.make_async_copy(pool_v.at[pg], vbuf.at[slot, i], sem.at[slot]))
            out.append(pltpu.make_async_copy(pool_lf.at[pg], lfbuf.at[slot, i], sem.at[slot]))
        return out

    def start_group(t):
        for c in group_copies(t, t % FOX_RING_SLOTS):
            c.start()

    @pl.when((b_id == 0) & (s_id == 0))
    def _():
        for t in range(FOX_RING_AHEAD):
            start_group(t)

    def attend(scores, values):
        m_prev = m_ref[...]
        m_new = m_prev
        for s in scores:
            m_new = jnp.maximum(m_new, jnp.max(s, axis=1, keepdims=True))
        corr = jnp.exp(m_prev - m_new)
        l_new = corr * l_ref[...]
        acc = corr[:, 0:1] * acc_ref[...]
        for s, pv in zip(scores, values):
            pr = jnp.exp(s - m_new)
            l_new = l_new + jnp.sum(pr, axis=1, keepdims=True)
            acc = acc + pv(pr.astype(BF16))
        l_ref[...] = l_new
        acc_ref[...] = acc
        m_ref[...] = m_new

    @pl.when(s_id == 0)
    def _():
        m_ref[...] = jnp.full_like(m_ref, NEG_BIG)
        l_ref[...] = jnp.zeros_like(l_ref)
        acc_ref[...] = jnp.zeros_like(acc_ref)
        suf_ref[...] = jnp.zeros_like(suf_ref)
        s = lax.dot_general(qbd_ref[...], kn_ref[...].astype(BF16), _NT, preferred_element_type=F32)
        s = s + (cq_ref[...] - jnp.tile(cn_ref[...], (rep, 1)))
        s = jnp.where(col <= row // FOX_HEADS, s, NEG_BIG)
        vn = vn_ref[...].astype(BF16)
        attend([s], [lambda pr: jnp.dot(pr, vn, preferred_element_type=F32)])

    @pl.when(s_id > 0)
    def _():
        t = b_id * groups_per_seq + (s_id - 1)
        slot = t % FOX_RING_SLOTS

        @pl.when(t + FOX_RING_AHEAD < total_groups)
        def _():
            start_group(t + FOX_RING_AHEAD)

        for c in group_copies(t, slot):
            c.wait()
        later = (_iota((PAGE, PAGE), 0) > _iota((PAGE, PAGE), 1)).astype(BF16)
        carry = suf_ref[...]
        qbd = qbd_ref[...]
        cq = cq_ref[...]
        scores, values = [], []
        lfs = [lfbuf[slot, i] for i in range(group)]
        local = [_mm_sel_r(lf, later) for lf in lfs]
        totals = [jnp.broadcast_to(loc[:, 0:1] + lf[:, 0:1], carry.shape) for loc, lf in zip(local, lfs)]
        for i in range(group):
            suffix = local[i] + carry
            carry = carry + totals[i]
            s = jnp.dot(qbd, kbuf[slot, i].astype(BF16), preferred_element_type=F32)
            scores.append(s + (cq + jnp.tile(suffix, (rep, 1))))
            values.append(lambda pr, i=i: lax.dot_general(pr, vbuf[slot, i].astype(BF16), _NT,
                                                          preferred_element_type=F32))
        suf_ref[...] = carry
        attend(scores, values)

    @pl.when(s_id == pl.num_programs(1) - 1)
    def _():
        lane_head = _iota((nq, FOX_DIM), 1) // HEAD_DIM
        row_head = _iota((nq, FOX_DIM), 0) % FOX_HEADS
        own = jnp.where(lane_head == row_head, acc_ref[...], 0.0)
        fold = (_iota((FOX_DIM, HEAD_DIM), 0) % HEAD_DIM == _iota((FOX_DIM, HEAD_DIM), 1)).astype(BF16)
        o_ref[...] = _mm_sel_r(own, fold) / l_ref[:, 0:HEAD_DIM]


def _fox_sample(page_table, qbd, cq, k_new, v_new, c_new, pool_k, pool_v, pool_lf):
    b, nq, _ = qbd.shape
    n_pages = page_table.shape[1]
    group = min(FOX_PAGE_GROUP, n_pages)
    assert n_pages % group == 0

    assert b * (n_pages // group) >= FOX_RING_AHEAD
    per_b = lambda shape: pl.BlockSpec((None,) + shape, lambda i, s, pt: (i, 0, 0))
    hbm = pl.BlockSpec(memory_space=pl.ANY)
    grid_spec = pltpu.PrefetchScalarGridSpec(
        num_scalar_prefetch=1, grid=(b, n_pages // group + 1),
        in_specs=[per_b((nq, FOX_DIM)), per_b((nq, LANES)), per_b((PAGE, FOX_DIM)), per_b((PAGE, FOX_DIM)),
                  per_b((FOX_HEADS, LANES)), hbm, hbm, hbm],
        out_specs=per_b((nq, HEAD_DIM)),
        scratch_shapes=[pltpu.VMEM((FOX_RING_SLOTS, group, FOX_DIM, PAGE), F32),
                        pltpu.VMEM((FOX_RING_SLOTS, group, FOX_DIM, PAGE), F32),
                        pltpu.VMEM((FOX_RING_SLOTS, group, FOX_HEADS, PAGE), F32),
                        pltpu.SemaphoreType.DMA((FOX_RING_SLOTS,)),
                        pltpu.VMEM((nq, LANES), F32), pltpu.VMEM((nq, LANES), F32),
                        pltpu.VMEM((nq, FOX_DIM), F32), pltpu.VMEM((FOX_HEADS, LANES), F32)])
    return pl.pallas_call(
        functools.partial(_fox_sample_kernel, group=group, n_pages=n_pages), grid_spec=grid_spec,
        out_shape=jax.ShapeDtypeStruct((b, nq, HEAD_DIM), F32),
        compiler_params=_cparams(("arbitrary", "arbitrary")), name="fox_sample")(
            page_table, qbd, cq, k_new, v_new, c_new, pool_k, pool_v, pool_lf)


def _ssm_kernel(z_ref, xbc_ref, fc_ref, fr_ref, buf_ref, s0_ref, cw_ref, cb_ref, dtb_c_ref, dtb_r_ref,
                a_c_ref, a_r_ref, d_ref, nw_ref, tl_ref, tu_ref, eall_ref, eexp_ref, y_ref, sout_ref, carry_ref, s_ref,
                *, valid):
    @pl.when(pl.program_id(1) == 0)
    def _():
        carry_ref[...] = buf_ref[...]
        s_ref[...] = s0_ref[...]

    raw = xbc_ref[...]
    cs = raw.shape[0]
    carry = carry_ref[...]
    cw = cw_ref[...]
    conv = (cw[0:1] * _shifted(raw, carry, 3) + cw[1:2] * _shifted(raw, carry, 2)
            + cw[2:3] * _shifted(raw, carry, 1) + cw[3:4] * raw)
    carry_ref[...] = raw[cs - SUBLANES:]
    xbc = _silu(conv + cb_ref[...])
    x = xbc[:, :SSM_DIM]

    dt_c = _softplus(fc_ref[...] + dtb_c_ref[...])
    dt_r = _softplus(fr_ref[...] + dtb_r_ref[...])
    if valid < cs:
        dt_c = jnp.where(_iota((cs, LANES), 0) < valid, dt_c, 0.0)
        dt_r = jnp.where(_iota((16, cs), 1) < valid, dt_r, 0.0)
    acum_c = _mm_sel_l(tl_ref[...], dt_c * a_c_ref[...])
    acum_r = _mm_sel_r(dt_r * a_r_ref[...], tu_ref[...])
    w_c = jnp.exp(acum_c[cs - 1:cs] - acum_c) * dt_c

    heads = range(SSM_HEADS)
    pairs = range(SSM_HEADS // 2)
    group_of = lambda p: (2 * p) // (SSM_HEADS // SSM_GROUPS)
    lane = _iota((cs, LANES), 1)
    causal = _iota((cs, cs), 0) >= _iota((cs, cs), 1)
    a_all = _mm_sel_r(acum_c, eall_ref[...])
    ea_wide = jnp.exp(_mm_sel_r(acum_c, eexp_ref[...]))
    w_wide = _mm_sel_r(w_c, eexp_ref[...])
    bm = [xbc[:, SSM_DIM + g * SSM_STATE:SSM_DIM + (g + 1) * SSM_STATE] for g in range(SSM_GROUPS)]
    cm = [xbc[:, SSM_DIM + (SSM_GROUPS + g) * SSM_STATE:SSM_DIM + (SSM_GROUPS + g + 1) * SSM_STATE]
          for g in range(SSM_GROUPS)]
    cb = [_mm(cm[g], bm[g], _NT) for g in range(SSM_GROUPS)]
    xp = [x[:, p * LANES:(p + 1) * LANES] for p in pairs]
    s_old = [s_ref[p] for p in pairs]
    a_col = [a_all[:, h * LANES:(h + 1) * LANES] for h in heads]
    gmat = [cb[group_of(h // 2)] * jnp.exp(jnp.where(causal, a_col[h] - acum_r[8 + h:9 + h, :], NEG_BIG))
            * dt_r[8 + h:9 + h, :] for h in heads]
    y_intra = [_mm(gmat[h], xp[h // 2]) for h in heads]
    y_inter = [_mm(cm[group_of(p)], s_old[p], _NT) for p in pairs]
    s_upd = [_mm(xp[p] * w_wide[:, p * LANES:(p + 1) * LANES], bm[group_of(p)], _TN) for p in pairs]
    y_pairs = []
    for p in pairs:
        dec = jnp.concatenate([jnp.broadcast_to(jnp.exp(a_col[2 * p + h][cs - 1:cs]), (HEAD_DIM, LANES))
                               for h in range(2)], axis=0)
        s_ref[p] = s_old[p] * dec + s_upd[p]
        y_pairs.append(jnp.where(lane < HEAD_DIM, y_intra[2 * p], y_intra[2 * p + 1])
                       + y_inter[p] * ea_wide[:, p * LANES:(p + 1) * LANES])

    y = jnp.concatenate(y_pairs, axis=1) + d_ref[...] * x
    y = y * _silu(z_ref[...])
    y_ref[...] = _rmsnorm(y, nw_ref[...]).astype(y_ref.dtype)

    @pl.when(pl.program_id(1) == pl.num_programs(1) - 1)
    def _():
        sout_ref[...] = s_ref[...]


def _ssm(z, xbc, fc, fr, buf8, s0, cw8, cb, dtb_c, dtb_r, a_c, a_r, d_exp, nw, tri_l, tri_u, valid):
    b, l, _ = z.shape
    cs = LANES
    src = jnp.arange(LANES)[:, None]
    e_all = (src == 8 + jnp.arange(SSM_HEADS * LANES)[None, :] // LANES).astype(BF16)
    e_exp = (src == 8 + jnp.arange(SSM_DIM)[None, :] // HEAD_DIM).astype(BF16)
    full = lambda s: pl.BlockSpec(s, lambda i, j: (0,) * len(s))
    sspec = pl.BlockSpec((None, 4, LANES, LANES), lambda i, j: (i, 0, 0, 0))
    return pl.pallas_call(
        functools.partial(_ssm_kernel, valid=valid), grid=(b, l // cs),
        in_specs=[pl.BlockSpec((None, cs, SSM_DIM), lambda i, j: (i, j, 0)),
                  pl.BlockSpec((None, cs, SSM_CONV_DIM), lambda i, j: (i, j, 0)),
                  pl.BlockSpec((None, cs, LANES), lambda i, j: (i, j, 0)),
                  pl.BlockSpec((None, 16, cs), lambda i, j: (i, 0, j)),
                  pl.BlockSpec((None, SUBLANES, SSM_CONV_DIM), lambda i, j: (i, 0, 0)),
                  sspec,
                  full((SUBLANES, SSM_CONV_DIM)), full((1, SSM_CONV_DIM)), full((1, LANES)), full((16, LANES)),
                  full((1, LANES)), full((16, LANES)), full((1, SSM_DIM)), full((1, SSM_DIM)),
                  full((cs, cs)), full((cs, cs)), full((LANES, SSM_HEADS * LANES)), full((LANES, SSM_DIM))],
        out_specs=[pl.BlockSpec((None, cs, SSM_DIM), lambda i, j: (i, j, 0)), sspec],
        out_shape=[jax.ShapeDtypeStruct((b, l, SSM_DIM), BF16), jax.ShapeDtypeStruct((b, 4, LANES, LANES), F32)],
        scratch_shapes=[pltpu.VMEM((SUBLANES, SSM_CONV_DIM), F32), pltpu.VMEM((4, LANES, LANES), F32)],
        compiler_params=_cparams(("parallel", "arbitrary")), name="ssm")(
            z, xbc, fc, fr, buf8, s0, cw8, cb, dtb_c, dtb_r, a_c, a_r, d_exp, nw, tri_l, tri_u, e_all, e_exp)


def _pad_rows(a, rows):
    return jnp.pad(a, ((0, 0), (rows - a.shape[1], 0), (0, 0)))


def _tri(n, seg):
    i = jnp.arange(n)
    same = (i[:, None] // seg) == (i[None, :] // seg)
    lower = (same & (i[:, None] >= i[None, :])).astype(BF16)
    return lower, lower.T


def _lane_vec(v, offset):
    n = v.shape[0]
    col = jnp.zeros((1, LANES), F32).at[0, offset:offset + n].set(v)
    row = jnp.zeros((16, LANES), F32).at[offset:offset + n, :].set(jnp.broadcast_to(v[:, None], (n, LANES)))
    return col, row


def _layer0(h, bsz, l, sc_buf, shift_prev, wkv0, p):
    m = bsz * l
    tl = min(512, l)
    z_sc, z_rw = _in_proj(h, p['g_mix'], p['w_in'], (3 * SC_DIM, RW_SHIFT_W))
    z_sc = z_sc.reshape(bsz, l, 3 * SC_DIM)
    z_rw = z_rw.reshape(bsz, l, RW_SHIFT_W)
    ya, sc_last = _sconv(z_sc, _pad_rows(sc_buf, SUBLANES), p['sc_w8'], tl)
    r, lw, kp, v, kn, kb, g = _rwkv_pre(z_rw, _pad_rows(shift_prev[:, None], SUBLANES), p['mu'], p['w0'], p['w2p'],
                                        p['a0'], p['a2p'], p['g2'], p['k_k'], p['k_a'], p['ones_bd'], tl)
    h_in = wkv0.reshape(bsz, 4, LANES, HEAD_DIM)
    lpad = -(-l // RW_CHUNK) * RW_CHUNK
    padl = lambda t: jnp.pad(t, ((0, 0), (0, lpad - l), (0, 0)))
    o, h_fin = _rwkv_chunk(padl(r), padl(lw), padl(kp), padl(v), padl(kn), padl(kb), h_in, *_rwkv_step_shape(bsz, lpad))
    o = o[:, :l]
    wkv_new = jnp.transpose(h_fin.reshape(bsz, 4, HEAD_DIM, 2, HEAD_DIM), (0, 1, 3, 4, 2))
    wkv_new = wkv_new.reshape(bsz, RW_HEADS, HEAD_DIM, HEAD_DIM)
    flat = lambda t: t.reshape(m, t.shape[-1])
    h = _out0(h, flat(ya), flat(o), flat(r), flat(kp), flat(v), flat(g), p['ln_w'], p['ln_b'], p['r_k'],
              p['ones_bd'], p['wo_a'], p['wo_b'])
    h = _ffn(h, p['g_ffn'], p['w_gate'], p['w_up'], p['w_down'], p['g_ffn'], False)
    return h, sc_last[:, SUBLANES - 2:], z_rw[:, -1], wkv_new


def _layer1(h, bsz, l, past, conv_buf, ssm0, p, g_final):
    m = bsz * l
    splits = (FOX_DIM, FOX_DIM, FOX_DIM, SSM_DIM, SSM_CONV_DIM, LANES)
    shape5 = (bsz, l, FOX_HEADS, HEAD_DIM)
    if past is None:
        q, k, v, zg, xbc, fdt, t_out = _in_proj(h, p['g_mix'], p['w_in'], splits, p['w_tkv'], (bsz, l))
        fr = t_out[:, 2 * FOX_DIM:]
        rows_t = lambda t: jnp.transpose(t.reshape(bsz, FOX_HEADS, HEAD_DIM, l), (0, 3, 1, 2))
        k_new, v_new = rows_t(t_out[:, :FOX_DIM]), rows_t(t_out[:, FOX_DIM:2 * FOX_DIM])
    else:
        q, k, v, zg, xbc, fdt, fdt_t = _in_proj(h, p['g_mix'], p['w_in'], splits, p['w_t'])
        fr = jnp.swapaxes(fdt_t.reshape(16, bsz, l), 0, 1)
        k_new, v_new = k.reshape(shape5), v.reshape(shape5)
    if past is None:
        tl = min(512, l)
        tri_l, tri_u = _tri(tl, tl)
        lf, cc, cr = _logf(fdt.reshape(bsz, l, LANES), fr, p['fb_c'], p['fb_r'], tri_l, tri_u, tl)
        qa, ka, va, vb = _fox_aug(q.reshape(bsz, l, FOX_DIM), k.reshape(bsz, l, FOX_DIM), v.reshape(bsz, l, FOX_DIM),
                              cc, tl)
        yf = _fox_prompt(qa, ka, va, vb, min(FOX_TQ, l), tl)
        logf_new = lf[..., :FOX_HEADS]
    else:
        page_table, pool_k, pool_v, pool_lf = past
        tri_l, tri_u = _tri(m, l)
        fr_flat = fdt_t.reshape(1, 16, m)
        lf, cc, cr = _logf(fdt.reshape(1, m, LANES), fr_flat, p['fb_c'], p['fb_r'], tri_l, tri_u, m)
        logf_new = lf.reshape(bsz, l, LANES)[..., :FOX_HEADS]
        q4 = q.reshape(bsz, l, FOX_HEADS, HEAD_DIM) * (HEAD_DIM ** -0.5)
        qbd = jnp.einsum('bqhd,hg->bqhgd', q4, jnp.eye(FOX_HEADS, dtype=F32))
        qbd = qbd.reshape(bsz, l * FOX_HEADS, FOX_DIM).astype(BF16)
        c_q = cc.reshape(bsz, l, LANES)[..., :FOX_HEADS].reshape(bsz, l * FOX_HEADS, 1)
        c_q = jnp.broadcast_to(c_q, (bsz, l * FOX_HEADS, LANES))
        c_new = jnp.pad(jnp.swapaxes(cr.reshape(16, bsz, l), 0, 1)[:, :FOX_HEADS], ((0, 0), (0, 0), (0, PAGE - l)))
        pad_new = lambda t: jnp.pad(t.reshape(bsz, l, FOX_DIM), ((0, 0), (0, PAGE - l), (0, 0)))
        yf = _fox_sample(page_table, qbd, c_q, pad_new(k), pad_new(v), c_new, pool_k, pool_v, pool_lf)
        yf = yf.reshape(bsz, l, FOX_DIM)
    ctl, ctu = _tri(SSD_CHUNK, SSD_CHUNK)
    xbc3 = xbc.reshape(bsz, l, SSM_CONV_DIM)
    if l % SSD_CHUNK == 0:
        lpad, valid = l, SSD_CHUNK
    else:
        assert l < SSD_CHUNK
        lpad, valid = SSD_CHUNK, l
    padl = lambda t: jnp.pad(t, ((0, 0), (0, lpad - l), (0, 0)))
    ym, s_fin = _ssm(padl(zg.reshape(bsz, l, SSM_DIM)), padl(xbc3), padl(fdt.reshape(bsz, l, LANES)),
                     jnp.pad(fr, ((0, 0), (0, 0), (0, lpad - l))),
                     _pad_rows(conv_buf, SUBLANES), ssm0.reshape(bsz, 4, LANES, LANES), p['cw8'], p['cb'],
                     p['dtb_c'], p['dtb_r'], p['a_c'], p['a_r'], p['d_exp'], p['nw'], ctl, ctu, valid)
    ym = ym[:, :l]
    h = _out1(h, yf.reshape(m, FOX_DIM), ym.reshape(m, SSM_DIM), p['wo_a'], p['wo_b'])
    h = _ffn(h, p['g_ffn'], p['w_gate'], p['w_up'], p['w_down'], g_final, True)
    return (h, k_new, v_new, logf_new, xbc3[:, l - 3:], s_fin.reshape(bsz, SSM_HEADS, HEAD_DIM, SSM_STATE))


def kernel(x_prompt, x_sample, state_sc, state_shift, state_wkv, cache_k, cache_v, cache_logf, state_ssm_conv, state_ssm, page_table, norm_mix, norm_ffn, norm_final, w_in0, sc_conv_w, rw_mu, rw_w0, rw_w2, rw_a0, rw_a2, rw_g2, rw_k_k, rw_k_a, rw_r_k, rw_ln_w, rw_ln_b, w_out0, w_in1, fox_f_bias, ssm_conv_w, ssm_conv_b, ssm_dt_bias, ssm_a_log, ssm_d, ssm_norm_w, w_out1, w_gate, w_up, w_down):
    bp, lp, _ = x_prompt.shape
    bs, ls, _ = x_sample.shape
    row = lambda v: v.reshape(1, -1).astype(F32)
    head_ids = jnp.arange(RW_DIM) // HEAD_DIM
    ones_bd = (head_ids[:, None] == head_ids[None, :]).astype(BF16)
    zpad = jnp.zeros((64, RW_DIM), F32)

    p0 = dict(
        g_mix=row(norm_mix[0]), g_ffn=row(norm_ffn[0]), w_in=w_in0[0].astype(BF16),
        sc_w8=jnp.pad(sc_conv_w[0], ((0, SUBLANES - sc_conv_w.shape[1]), (0, 0))),
        mu=row(rw_mu[0]), w0=row(rw_w0[0]), a0=row(rw_a0[0]),
        w2p=jnp.concatenate([rw_w2[0], zpad], axis=0).astype(BF16),
        a2p=jnp.concatenate([zpad, rw_a2[0]], axis=0).astype(BF16),
        g2=rw_g2[0].astype(BF16), k_k=row(rw_k_k[0]), k_a=row(rw_k_a[0]), r_k=row(rw_r_k[0]),
        ln_w=row(rw_ln_w[0]), ln_b=row(rw_ln_b[0]), ones_bd=ones_bd,
        wo_a=w_out0[0, :SC_DIM].astype(BF16), wo_b=w_out0[0, SC_DIM:].astype(BF16),
        w_gate=w_gate[0].astype(BF16), w_up=w_up[0].astype(BF16), w_down=w_down[0].astype(BF16))

    w1 = w_in1[0]
    c0 = 3 * FOX_DIM
    f_cols = w1[:, c0:c0 + FOX_HEADS]
    z_cols = w1[:, c0 + FOX_HEADS:c0 + FOX_HEADS + SSM_DIM]
    xbc_cols = w1[:, c0 + FOX_HEADS + SSM_DIM:c0 + FOX_HEADS + SSM_DIM + SSM_CONV_DIM]
    dt_cols = w1[:, c0 + FOX_HEADS + SSM_DIM + SSM_CONV_DIM:]
    fdt_cols = jnp.concatenate([f_cols, dt_cols, jnp.zeros((D_MODEL, LANES - 16), F32)], axis=1)
    w1r = jnp.concatenate([w1[:, :c0], z_cols, xbc_cols, fdt_cols], axis=1).astype(BF16)
    fb_c, fb_r = _lane_vec(fox_f_bias[0], 0)
    dtb_c, dtb_r = _lane_vec(ssm_dt_bias[0], 8)
    a_c, a_r = _lane_vec(-jnp.exp(ssm_a_log[0].astype(F32)), 8)
    p1 = dict(
        g_mix=row(norm_mix[1]), g_ffn=row(norm_ffn[1]), w_in=w1r, w_t=fdt_cols[:, :16].T.astype(BF16),
        w_tkv=jnp.concatenate([w1[:, FOX_DIM:3 * FOX_DIM], fdt_cols[:, :16]], axis=1).T.astype(BF16),
        fb_c=fb_c, fb_r=fb_r, dtb_c=dtb_c, dtb_r=dtb_r, a_c=a_c, a_r=a_r,
        cw8=jnp.pad(ssm_conv_w[0], ((0, SUBLANES - ssm_conv_w.shape[1]), (0, 0))), cb=row(ssm_conv_b[0]),
        d_exp=row(jnp.repeat(ssm_d[0], HEAD_DIM)), nw=row(ssm_norm_w[0]),
        wo_a=w_out1[0, :FOX_DIM].astype(BF16), wo_b=w_out1[0, FOX_DIM:].astype(BF16),
        w_gate=w_gate[1].astype(BF16), w_up=w_up[1].astype(BF16), w_down=w_down[1].astype(BF16))
    g_final = row(norm_final)

    n_pool = cache_k.shape[1]
    page_t = lambda c: jnp.transpose(c[0], (0, 2, 3, 1)).reshape(n_pool, FOX_DIM, PAGE)
    past = (page_table, page_t(cache_k), page_t(cache_v), jnp.swapaxes(cache_logf[0], 1, 2))

    hp = x_prompt.reshape(bp * lp, D_MODEL)
    hs = x_sample.reshape(bs * ls, D_MODEL)
    hp, sc_p, sh_p, wkv_p = _layer0(hp, bp, lp, jnp.zeros((bp, 2, SC_DIM), F32), jnp.zeros((bp, RW_SHIFT_W), F32),
                                    jnp.zeros((bp, RW_HEADS, HEAD_DIM, HEAD_DIM), F32), p0)
    hs, sc_s, sh_s, wkv_s = _layer0(hs, bs, ls, state_sc[0], state_shift[0], state_wkv[0], p0)
    yp, k_p, v_p, lf_p, cv_p, ssm_p = _layer1(hp, bp, lp, None, jnp.zeros((bp, 3, SSM_CONV_DIM), F32),
                                              jnp.zeros((bp, SSM_HEADS, HEAD_DIM, SSM_STATE), F32), p1, g_final)
    ys, k_s, v_s, lf_s, cv_s, ssm_s = _layer1(hs, bs, ls, past, state_ssm_conv[0], state_ssm[0], p1, g_final)
    one = lambda t: t[None]
    return (yp.reshape(bp, lp, D_MODEL), ys.reshape(bs, ls, D_MODEL),
            one(sc_p), one(sc_s), one(sh_p), one(sh_s), one(wkv_p), one(wkv_s),
            one(k_p), one(k_s), one(v_p), one(v_s), one(lf_p), one(lf_s), one(cv_p), one(cv_s),
            one(ssm_p), one(ssm_s))
```

```python
import functools

import jax
import jax.numpy as jnp
from jax import lax
from jax.experimental import pallas as pl
from jax.experimental.pallas import tpu as pltpu

F32 = jnp.float32
BF16 = jnp.bfloat16

D_MODEL = 1024
HEAD_DIM = 64
NORM_EPS = 1e-6
SC_DIM = 512
RW_DIM = 512
RW_HEADS = 8
RW_SHIFT_W = 3 * RW_DIM + 64 + 64 + 128
RW_GN_EPS = 64e-5
FOX_DIM = 512
FOX_HEADS = 8
SSM_DIM = 512
SSM_HEADS = 8
SSM_GROUPS = 2
SSM_STATE = 128
SSM_CONV_DIM = SSM_DIM + 2 * SSM_GROUPS * SSM_STATE
FFN_HIDDEN = 2816
PAGE = 128
SSD_CHUNK = 128
SSM_STEP_CHUNKS = 2
FOX_PAGE_GROUP = 8
FOX_RING_SLOTS = 3
FOX_RING_AHEAD = FOX_RING_SLOTS - 1
FOX_TQ = 1024
FOX_ROW_SPLIT = 2
RW_UNITS = 16
RW_CHUNK = 64

LANES = 128
SUBLANES = 8
VMEM_LIMIT = 56 * 1024 * 1024

NEG_BIG = -1e30
LOG2E = 1.4426950408889634


def _cparams(sem):
    return pltpu.CompilerParams(dimension_semantics=sem, vmem_limit_bytes=VMEM_LIMIT)


_NN = (((1,), (0,)), ((), ()))
_NT = (((1,), (1,)), ((), ()))
_TN = (((0,), (0,)), ((), ()))


def _mm(a, b, dims=_NN):
    return lax.dot_general(a.astype(BF16), b.astype(BF16), dims, preferred_element_type=F32)


def _split2(x):
    hi = x.astype(BF16)
    lo = (x - hi.astype(F32)).astype(BF16)
    return hi, lo


def _split3(x):
    hi = x.astype(BF16)
    r1 = x - hi.astype(F32)
    mid = r1.astype(BF16)
    lo = (r1 - mid.astype(F32)).astype(BF16)
    return hi, mid, lo


def _mm3(a, b, dims=_NN):
    ah, al = _split2(a)
    bh, bl = _split2(b)
    d = functools.partial(lax.dot_general, dimension_numbers=dims, preferred_element_type=F32)
    return d(ah, bh) + (d(al, bh) + d(ah, bl))


def _mm_sel_r(x, e, dims=_NN):
    h, m, l = _split3(x)
    d = functools.partial(lax.dot_general, dimension_numbers=dims, preferred_element_type=F32)
    return d(h, e) + (d(m, e) + d(l, e))


def _mm_sum_r(x, e, dims=_NN):
    h, l = _split2(x)
    d = functools.partial(lax.dot_general, dimension_numbers=dims, preferred_element_type=F32)
    return d(h, e) + d(l, e)


def _mm_sel_l(e, x, dims=_NN):
    h, m, l = _split3(x)
    d = functools.partial(lax.dot_general, dimension_numbers=dims, preferred_element_type=F32)
    return d(e, h) + (d(e, m) + d(e, l))


def _rmsnorm(x, g):
    return x * lax.rsqrt(jnp.mean(x * x, axis=-1, keepdims=True) + NORM_EPS) * g


def _sigmoid(x):
    return 1.0 / (1.0 + jnp.exp(-x))


def _silu(x):
    return x * _sigmoid(x)


def _softplus(x):
    return jnp.maximum(x, 0.0) + jnp.log(1.0 + jnp.exp(-jnp.abs(x)))


def _iota(shape, dim):
    return lax.broadcasted_iota(jnp.int32, shape, dim)


def _shifted(u, carry8, d):
    tl = u.shape[0]
    s = pltpu.roll(u, d, 0)
    c = pltpu.roll(carry8, d, 0)
    row = _iota((SUBLANES, u.shape[1]), 0)
    head = jnp.where(row < d, c, s[:SUBLANES])
    if tl == SUBLANES:
        return head
    return jnp.concatenate([head, s[SUBLANES:]], axis=0)


def _in_proj_kernel(*refs, splits, t_rows):
    x_ref, g_ref, w_ref = refs[:3]
    pos = 3
    if t_rows:
        wt_ref = refs[pos]
        pos += 1
    outs = refs[pos:pos + len(splits)]
    t_outs = refs[pos + len(splits):pos + len(splits) + len(t_rows)]
    nb = _rmsnorm(x_ref[...], g_ref[...]).astype(BF16)
    off = 0
    for o_ref, (width, scale) in zip(outs, splits):
        y = jnp.dot(nb, w_ref[:, off:off + width], preferred_element_type=F32)
        o_ref[...] = (y if scale == 1.0 else y * scale).astype(o_ref.dtype)
        off += width
    off = 0
    for o_ref, rows in zip(t_outs, t_rows):
        o_ref[...] = lax.dot_general(wt_ref[off:off + rows], nb, _NT, preferred_element_type=F32)
        off += rows


def _in_proj(x, g, w, splits, wt=None, t_rows=(), seqs=None):
    m = x.shape[0]
    tm = min(512, m)
    splits = [s if isinstance(s, tuple) else (s, F32, 1.0) for s in splits]
    n = sum(s[0] for s in splits)
    in_specs = [pl.BlockSpec((tm, D_MODEL), lambda i: (i, 0)),
                pl.BlockSpec((1, D_MODEL), lambda i: (0, 0)),
                pl.BlockSpec((D_MODEL, n), lambda i: (0, 0))]
    args = [x, g, w]
    out_shape = [jax.ShapeDtypeStruct((m, s[0]), s[1]) for s in splits]
    out_specs = [pl.BlockSpec((tm, s[0]), lambda i: (i, 0)) for s in splits]
    if wt is not None:
        assert sum(t_rows) == wt.shape[0]
        in_specs.append(pl.BlockSpec((wt.shape[0], D_MODEL), lambda i: (0, 0)))
        args.append(wt)
        for rows in t_rows:
            if seqs is None:
                out_shape.append(jax.ShapeDtypeStruct((rows, m), F32))
                out_specs.append(pl.BlockSpec((rows, tm), lambda i: (0, i)))
            else:
                bsz, l = seqs
                per_seq = l // tm
                out_shape.append(jax.ShapeDtypeStruct((bsz, rows, l), F32))
                out_specs.append(pl.BlockSpec((None, rows, tm), lambda i: (i // per_seq, 0, i % per_seq)))
    return pl.pallas_call(
        functools.partial(_in_proj_kernel, splits=tuple((s[0], s[2]) for s in splits), t_rows=tuple(t_rows)),
        grid=(m // tm,), in_specs=in_specs, out_specs=out_specs, out_shape=out_shape,
        compiler_params=_cparams(("parallel",)), name="in_proj")(*args)


def _ffn_kernel(x_ref, g_ref, wg_ref, wu_ref, wd_ref, gf_ref, o_ref, xn_ref, acc_ref, *, final_norm):
    j = pl.program_id(1)

    @pl.when(j == 0)
    def _():
        x = x_ref[...]
        xn_ref[...] = _rmsnorm(x, g_ref[...]).astype(BF16)
        acc_ref[...] = x

    xn = xn_ref[...]
    gate = jnp.dot(xn, wg_ref[...], preferred_element_type=F32)
    up = jnp.dot(xn, wu_ref[...], preferred_element_type=F32)
    act = (_silu(gate) * up).astype(BF16)
    acc_ref[...] += jnp.dot(act, wd_ref[...], preferred_element_type=F32)

    @pl.when(j == pl.num_programs(1) - 1)
    def _():
        y = acc_ref[...]
        if final_norm:
            y = _rmsnorm(y, gf_ref[...])
        o_ref[...] = y


def _ffn(x, g, wg, wu, wd, gfin, final_norm):
    m = x.shape[0]
    tm = min(512, m)
    th = FFN_HIDDEN // 2
    return pl.pallas_call(
        functools.partial(_ffn_kernel, final_norm=final_norm),
        grid=(m // tm, FFN_HIDDEN // th),
        in_specs=[pl.BlockSpec((tm, D_MODEL), lambda i, j: (i, 0)),
                  pl.BlockSpec((1, D_MODEL), lambda i, j: (0, 0)),
                  pl.BlockSpec((D_MODEL, th), lambda i, j: (0, j)),
                  pl.BlockSpec((D_MODEL, th), lambda i, j: (0, j)),
                  pl.BlockSpec((th, D_MODEL), lambda i, j: (j, 0)),
                  pl.BlockSpec((1, D_MODEL), lambda i, j: (0, 0))],
        out_specs=pl.BlockSpec((tm, D_MODEL), lambda i, j: (i, 0)),
        out_shape=jax.ShapeDtypeStruct((m, D_MODEL), F32),
        scratch_shapes=[pltpu.VMEM((tm, D_MODEL), BF16), pltpu.VMEM((tm, D_MODEL), F32)],
        compiler_params=_cparams(("parallel", "arbitrary")), name="ffn")(x, g, wg, wu, wd, gfin)


def _sconv_kernel(z_ref, buf_ref, w_ref, y_ref, last_ref, carry_ref):
    @pl.when(pl.program_id(1) == 0)
    def _():
        carry_ref[...] = buf_ref[...]

    z = z_ref[...]
    tl = z.shape[0]
    gb, gc, h = z[:, :SC_DIM], z[:, SC_DIM:2 * SC_DIM], z[:, 2 * SC_DIM:]
    u = gc * h
    carry = carry_ref[...]
    w = w_ref[...]
    y = w[0:1] * _shifted(u, carry, 2) + w[1:2] * _shifted(u, carry, 1) + w[2:3] * u
    y_ref[...] = (gb * y).astype(y_ref.dtype)
    tail = u[tl - SUBLANES:]
    carry_ref[...] = tail
    last_ref[...] = tail


def _sconv(z, buf8, w8, tl):
    b, l, _ = z.shape
    return pl.pallas_call(
        _sconv_kernel, grid=(b, l // tl),
        in_specs=[pl.BlockSpec((None, tl, 3 * SC_DIM), lambda i, j: (i, j, 0)),
                  pl.BlockSpec((None, SUBLANES, SC_DIM), lambda i, j: (i, 0, 0)),
                  pl.BlockSpec((SUBLANES, SC_DIM), lambda i, j: (0, 0))],
        out_specs=[pl.BlockSpec((None, tl, SC_DIM), lambda i, j: (i, j, 0)),
                   pl.BlockSpec((None, SUBLANES, SC_DIM), lambda i, j: (i, 0, 0))],
        out_shape=[jax.ShapeDtypeStruct((b, l, SC_DIM), BF16 if tl % 16 == 0 else F32),
                   jax.ShapeDtypeStruct((b, SUBLANES, SC_DIM), F32)],
        scratch_shapes=[pltpu.VMEM((SUBLANES, SC_DIM), F32)],
        compiler_params=_cparams(("parallel", "arbitrary")), name="sconv")(z, buf8, w8)


def _rwkv_pre_kernel(z_ref, sh_ref, mu_ref, w0_ref, w2_ref, a0_ref, a2_ref, g2_ref, kk_ref, ka_ref, ones_ref,
                     r_o, lw_o, kp_o, v_o, kn_o, kb_o, g_o, carry_ref):
    @pl.when(pl.program_id(1) == 0)
    def _():
        carry_ref[...] = sh_ref[...]

    z = z_ref[...]
    tl = z.shape[0]
    zs = _shifted(z, carry_ref[...], 1)
    carry_ref[...] = z[tl - SUBLANES:]
    zx = z + mu_ref[...] * (zs - z)
    r = zx[:, :RW_DIM]
    k = zx[:, RW_DIM:2 * RW_DIM]
    v = zx[:, 2 * RW_DIM:3 * RW_DIM]
    wa = zx[:, 3 * RW_DIM:3 * RW_DIM + 128]
    gd = zx[:, 3 * RW_DIM + 128:]
    w_log = -_softplus(-(w0_ref[...] + _mm(jnp.tanh(wa), w2_ref[...]))) - 0.5
    lw_o[...] = -jnp.exp(w_log)
    alpha = _sigmoid(a0_ref[...] + _mm(wa, a2_ref[...]))
    g_o[...] = _mm(_sigmoid(gd), g2_ref[...])
    kk = k * kk_ref[...]
    n2 = _mm_sum_r(kk * kk, ones_ref[...])
    kk = kk / jnp.maximum(jnp.sqrt(n2), 1e-12)
    r_o[...] = r
    kp_o[...] = k * (1.0 + (alpha - 1.0) * ka_ref[...])
    v_o[...] = v
    kn_o[...] = kk
    kb_o[...] = kk * alpha


def _rwkv_pre(z, sh8, mu, w0, w2p, a0, a2p, g2, k_k, k_a, ones_bd, tl):
    b, l, _ = z.shape
    full = lambda s: pl.BlockSpec(s, lambda i, j: (0,) * len(s))
    out = jax.ShapeDtypeStruct((b, l, RW_DIM), F32)
    ospec = pl.BlockSpec((None, tl, RW_DIM), lambda i, j: (i, j, 0))
    return pl.pallas_call(
        _rwkv_pre_kernel, grid=(b, l // tl),
        in_specs=[pl.BlockSpec((None, tl, RW_SHIFT_W), lambda i, j: (i, j, 0)),
                  pl.BlockSpec((None, SUBLANES, RW_SHIFT_W), lambda i, j: (i, 0, 0)),
                  full((1, RW_SHIFT_W)), full((1, RW_DIM)), full((128, RW_DIM)), full((1, RW_DIM)),
                  full((128, RW_DIM)), full((128, RW_DIM)), full((1, RW_DIM)), full((1, RW_DIM)),
                  full((RW_DIM, RW_DIM))],
        out_specs=[ospec] * 7, out_shape=[out] * 7,
        scratch_shapes=[pltpu.VMEM((SUBLANES, RW_SHIFT_W), F32)],
        compiler_params=_cparams(("parallel", "arbitrary")), name="rwkv_pre")(
            z, sh8, mu, w0, w2p, a0, a2p, g2, k_k, k_a, ones_bd)


def _unit_lower_inverse(ns, row, col, t):
    eye = (row == col).astype(F32)
    lower = row > col
    same8 = (row // 8) == (col // 8)
    n8 = [jnp.where(same8 & lower, n, 0.0) for n in ns]
    p2 = [_mm(a, a) for a in n8]
    p4 = [_mm(a, a) for a in p2]
    xs = [_mm(eye + a, eye + b) for a, b in zip(n8, p2)]
    xs = [_mm(x, eye + b) for x, b in zip(xs, p4)]
    s = 16
    while s <= t:
        level = ((row // s) == (col // s)) & ((row // (s // 2)) != (col // (s // 2))) & lower
        xc = [_mm(x, jnp.where(level, n, 0.0)) for x, n in zip(xs, ns)]
        xs = [x + _mm(y, x) for x, y in zip(xs, xc)]
        s *= 2
    return xs


def _rwkv_chunk_math(r, lw, k, v, kn, kb, h0s, tri, t):
    chunks = range(r.shape[0] // t)
    pairs = range(len(h0s))
    units = [(c, p) for c in chunks for p in pairs]
    lane_a = _iota((t, LANES), 1) < HEAD_DIM
    row2 = _iota((2 * t, 2 * t), 0)
    col2 = _iota((2 * t, 2 * t), 1)
    rowh = _iota((LANES, LANES), 0)
    colh = _iota((LANES, LANES), 1)
    cum = _mm_sel_l(tri, lw)
    last = [cum[(c + 1) * t - 1:(c + 1) * t] for c in chunks]
    cum_end = jnp.concatenate([jnp.broadcast_to(x, (t, x.shape[1])) for x in last], axis=0)
    g_inv = jnp.exp(-cum)
    g_end = jnp.exp(cum_end - cum)
    a_t = -kn * jnp.exp(cum - lw)
    r_t = r * jnp.exp(cum)
    b_t = kb * g_inv
    k_t = k * g_inv
    b_h = kb * g_end
    k_h = k * g_end

    def part(x, c, p):
        return x[c * t:(c + 1) * t, p * LANES:(p + 1) * LANES]

    def stack(x):
        return [jnp.concatenate([jnp.where(lane_a, part(x, c, p), 0.0), jnp.where(lane_a, 0.0, part(x, c, p))],
                                axis=0) for c, p in units]

    la, lr, lb, lk, vbd = stack(a_t), stack(r_t), stack(b_t), stack(k_t), stack(v)
    every = range(len(units))
    same_head = (row2 // t) == (col2 // t)
    strict = same_head & (row2 > col2)
    incl = same_head & (row2 >= col2)
    n = [jnp.where(strict, _mm(la[i], lb[i], _NT), 0.0) for i in every]
    aak = [jnp.where(strict, _mm(la[i], lk[i], _NT), 0.0) for i in every]
    arb = [jnp.where(incl, _mm(lr[i], lb[i], _NT), 0.0) for i in every]
    ark = [jnp.where(incl, _mm(lr[i], lk[i], _NT), 0.0) for i in every]
    aakv = [_mm(aak[i], vbd[i]) for i in every]
    arkv = [_mm(ark[i], vbd[i]) for i in every]
    inv = _unit_lower_inverse(n, row2, col2, t)
    same_h = (rowh // HEAD_DIM) == (colh // HEAD_DIM)
    hs = list(h0s)
    o_rows = []
    for c in chunks:
        idx = [c * len(pairs) + p for p in pairs]
        rhs = [_mm3(la[i], hs[p]) + aakv[i] for p, i in zip(pairs, idx)]
        u0 = [_mm(inv[i], x) for i, x in zip(idx, rhs)]
        res = [x - y + _mm3(n[i], y) for i, x, y in zip(idx, rhs, u0)]
        u = [y + _mm(inv[i], z) for i, y, z in zip(idx, u0, res)]
        o = [_mm(lr[i], hs[p]) + _mm(arb[i], u[p]) + arkv[i] for p, i in zip(pairs, idx)]
        upd = [_mm(jnp.concatenate([part(b_h, c, p), part(k_h, c, p)], axis=0),
                    jnp.concatenate([u[p][:t] + u[p][t:], part(v, c, p)], axis=0), _TN) for p in pairs]
        g_t = jnp.exp(last[c])
        hs = [jnp.broadcast_to(g_t[:, p * LANES:(p + 1) * LANES], (LANES, LANES)).T * hs[p]
              + jnp.where(same_h, upd[p], 0.0) for p in pairs]
        o_rows.append(jnp.concatenate([o[p][:t] + o[p][t:] for p in pairs], axis=1))
    return jnp.concatenate(o_rows, axis=0), hs


def _rwkv_chunk_kernel(r_ref, lw_ref, k_ref, v_ref, kn_ref, kb_ref, h0_ref, tri_ref, o_ref, hout_ref, h_ref):
    bb = r_ref.shape[0]
    npair = RW_DIM // LANES

    @pl.when(pl.program_id(1) == 0)
    def _():
        eye = (_iota((HEAD_DIM, HEAD_DIM), 0) == _iota((HEAD_DIM, HEAD_DIM), 1)).astype(BF16)
        lane_a = _iota((HEAD_DIM, LANES), 1) < HEAD_DIM
        for b in range(bb):
            for p in range(npair):
                tr = _mm_sel_l(eye, h0_ref[b, p], _NT)
                h_ref[b, p] = jnp.concatenate([jnp.where(lane_a, tr, 0.0), jnp.where(lane_a, 0.0, tr)], axis=0)

    wide = lambda ref: jnp.concatenate([ref[b] for b in range(bb)], axis=1)
    h0s = [h_ref[b, p] for b in range(bb) for p in range(npair)]
    o_wide, h_new = _rwkv_chunk_math(wide(r_ref), wide(lw_ref), wide(k_ref), wide(v_ref), wide(kn_ref),
                                     wide(kb_ref), h0s, tri_ref[...], RW_CHUNK)
    for b in range(bb):
        o_ref[b] = o_wide[:, b * RW_DIM:(b + 1) * RW_DIM]
        for p in range(npair):
            hn = h_new[b * npair + p]
            h_ref[b, p] = hn
            hout_ref[b, p] = hn[:HEAD_DIM] + hn[HEAD_DIM:]


def _rwkv_step_shape(bsz, l):
    pairs = RW_DIM // LANES
    chunks = 2 if l % (2 * RW_CHUNK) == 0 else 1
    bb = max(1, RW_UNITS // (pairs * chunks))
    while bsz % bb:
        bb //= 2
    return bb, chunks * RW_CHUNK


def _rwkv_chunk(r, lw, kp, v, kn, kb, h0, bb, t):
    b, l, _ = r.shape
    assert b % bb == 0 and l % t == 0 and t % RW_CHUNK == 0
    tri = _tri(t, RW_CHUNK)[0]
    xspec = pl.BlockSpec((bb, t, RW_DIM), lambda i, j: (i, j, 0))
    return pl.pallas_call(
        _rwkv_chunk_kernel, grid=(b // bb, l // t),
        in_specs=[xspec] * 6 + [pl.BlockSpec((bb, 4, LANES, HEAD_DIM), lambda i, j: (i, 0, 0, 0)),
                                pl.BlockSpec((t, t), lambda i, j: (0, 0))],
        out_specs=[xspec, pl.BlockSpec((bb, 4, HEAD_DIM, LANES), lambda i, j: (i, 0, 0, 0))],
        out_shape=[jax.ShapeDtypeStruct((b, l, RW_DIM), F32), jax.ShapeDtypeStruct((b, 4, HEAD_DIM, LANES), F32)],
        scratch_shapes=[pltpu.VMEM((bb, 4, LANES, LANES), F32)],
        compiler_params=_cparams(("parallel", "arbitrary")), name="rwkv_chunk")(r, lw, kp, v, kn, kb, h0, tri)


def _out0_kernel(x_ref, ya_ref, o_ref, r_ref, kp_ref, v_ref, g_ref, lnw_ref, lnb_ref, rk_ref, ones_ref,
                 wa_ref, wb_ref, out_ref):
    ones = ones_ref[...]
    o = o_ref[...]
    mean = _mm_sum_r(o, ones) * (1.0 / HEAD_DIM)
    d = o - mean
    var = _mm_sum_r(d * d, ones) * (1.0 / HEAD_DIM)
    on = d * lax.rsqrt(var + RW_GN_EPS) * lnw_ref[...] + lnb_ref[...]
    bonus = _mm_sum_r(r_ref[...] * kp_ref[...] * rk_ref[...], ones) * v_ref[...]
    yb = (on + bonus) * g_ref[...]
    out_ref[...] = x_ref[...] + _mm(ya_ref[...], wa_ref[...]) + _mm(yb, wb_ref[...])


def _out0(x, ya, o, r, kp, v, g, lnw, lnb, rk, ones_bd, wa, wb):
    m = x.shape[0]
    tm = min(512, m)
    row = lambda n: pl.BlockSpec((tm, n), lambda i: (i, 0))
    full = lambda s: pl.BlockSpec(s, lambda i: (0, 0))
    return pl.pallas_call(
        _out0_kernel, grid=(m // tm,),
        in_specs=[row(D_MODEL)] + [row(RW_DIM)] * 6 + [full((1, RW_DIM))] * 3 + [full((RW_DIM, RW_DIM))]
                 + [full((SC_DIM, D_MODEL)), full((RW_DIM, D_MODEL))],
        out_specs=row(D_MODEL), out_shape=jax.ShapeDtypeStruct((m, D_MODEL), F32),
        compiler_params=_cparams(("parallel",)), name="out0")(x, ya, o, r, kp, v, g, lnw, lnb, rk, ones_bd, wa, wb)


def _out1_kernel(x_ref, a_ref, b_ref, wa_ref, wb_ref, out_ref):
    out_ref[...] = x_ref[...] + _mm(a_ref[...], wa_ref[...]) + _mm(b_ref[...], wb_ref[...])


def _out1(x, a, b, wa, wb):
    m = x.shape[0]
    tm = min(512, m)
    row = lambda n: pl.BlockSpec((tm, n), lambda i: (i, 0))
    full = lambda s: pl.BlockSpec(s, lambda i: (0, 0))
    return pl.pallas_call(
        _out1_kernel, grid=(m // tm,),
        in_specs=[row(D_MODEL), row(FOX_DIM), row(SSM_DIM), full((FOX_DIM, D_MODEL)), full((SSM_DIM, D_MODEL))],
        out_specs=row(D_MODEL), out_shape=jax.ShapeDtypeStruct((m, D_MODEL), F32),
        compiler_params=_cparams(("parallel",)), name="out1")(x, a, b, wa, wb)


def _logf_kernel(fc_ref, fr_ref, bc_ref, br_ref, tl_ref, tu_ref, lf_o, cc_o, cr_o, carry_c, carry_r):
    @pl.when(pl.program_id(1) == 0)
    def _():
        carry_c[...] = jnp.zeros_like(carry_c)
        carry_r[...] = jnp.zeros_like(carry_r)

    tl = fc_ref.shape[0]
    lf_c = -_softplus(-(fc_ref[...] + bc_ref[...]))
    lf_o[...] = lf_c
    cc = _mm_sel_l(tl_ref[...], lf_c) + carry_c[0:1]
    cc_o[...] = cc
    carry_c[...] = jnp.broadcast_to(cc[tl - 1:tl], carry_c.shape)
    rep = tl // LANES
    lf_r = -_softplus(-(fr_ref[...] + jnp.tile(br_ref[...], (1, rep))))
    cr = _mm_sel_r(lf_r, tu_ref[...]) + jnp.tile(carry_r[...], (1, rep))
    cr_o[...] = cr
    carry_r[...] = jnp.broadcast_to(cr[:, tl - 1:tl], carry_r.shape)


def _logf(fc, fr, bias_c, bias_r, tri_l, tri_u, tl):
    b, l, _ = fc.shape
    cspec = pl.BlockSpec((None, tl, LANES), lambda i, j: (i, j, 0))
    rspec = pl.BlockSpec((None, 16, tl), lambda i, j: (i, 0, j))
    full = lambda s: pl.BlockSpec(s, lambda i, j: (0, 0))
    return pl.pallas_call(
        _logf_kernel, grid=(b, l // tl),
        in_specs=[cspec, rspec, full((1, LANES)), full((16, LANES)), full((tl, tl)), full((tl, tl))],
        out_specs=[cspec, cspec, rspec],
        out_shape=[jax.ShapeDtypeStruct((b, l, LANES), F32), jax.ShapeDtypeStruct((b, l, LANES), F32),
                   jax.ShapeDtypeStruct((b, 16, l), F32)],
        scratch_shapes=[pltpu.VMEM((SUBLANES, LANES), F32), pltpu.VMEM((16, LANES), F32)],
        compiler_params=_cparams(("parallel", "arbitrary")), name="logf")(fc, fr, bias_c, bias_r, tri_l, tri_u)


def _fox_aug_kernel(q_ref, k_ref, v_ref, cc_ref, wq_ref, wk_ref, oq_ref, ok_ref, qa_o, ka_o, va_o, vb_o):
    c_hi, c_mid, c_lo = _split3(cc_ref[...] * LOG2E)
    q = q_ref[...]
    k = k_ref[...]
    for p in range(FOX_DIM // LANES):
        sl = slice(p * LANES, (p + 1) * LANES)
        xq = jnp.concatenate([q[:, sl], c_hi, c_mid, c_lo], axis=1)
        xk = jnp.concatenate([k[:, sl], c_hi, c_mid, c_lo], axis=1)
        for h in range(2):
            hh = 2 * p + h
            qa_o[hh] = (jnp.dot(xq, wq_ref[hh], preferred_element_type=F32) + oq_ref[...]).astype(BF16)
            ka_o[hh] = (jnp.dot(xk, wk_ref[hh], preferred_element_type=F32) + ok_ref[...]).astype(BF16)
    v = v_ref[...]
    first = (_iota(v.shape, 1) % LANES) < HEAD_DIM
    one = jnp.ones_like(v)
    va_o[...] = jnp.where(first, v, one)
    vb_o[...] = jnp.where(first, one, v)


def _fox_aug(q, k, v, cc, tl):
    b, l, _ = q.shape
    head = jnp.arange(FOX_HEADS)
    src = jnp.arange(4 * LANES)
    dst = jnp.arange(LANES)
    vec = (src[None, :, None] < LANES) & (src[None, :, None] - (head[:, None, None] % 2) * HEAD_DIM == dst[None, None, :]) \
        & (dst[None, None, :] < HEAD_DIM)
    part = (src[None, :, None] - LANES) // LANES
    is_c = (src[None, :, None] >= LANES) & ((src[None, :, None] % LANES) == head[:, None, None])
    wq = (vec | (is_c & (dst[None, None, :] == HEAD_DIM + part))).astype(BF16)
    wk = vec.astype(BF16) - (is_c & (dst[None, None, :] == HEAD_DIM + 3 + part)).astype(BF16)
    ones_q = ((dst >= HEAD_DIM + 3) & (dst < HEAD_DIM + 6)).astype(F32)[None]
    ones_k = ((dst >= HEAD_DIM) & (dst < HEAD_DIM + 3)).astype(F32)[None]
    xspec = pl.BlockSpec((None, tl, FOX_DIM), lambda i, j: (i, j, 0))
    aspec = pl.BlockSpec((None, FOX_HEADS, tl, LANES), lambda i, j: (i, 0, j, 0))
    full = lambda s: pl.BlockSpec(s, lambda i, j: (0,) * len(s))
    aug = jax.ShapeDtypeStruct((b, FOX_HEADS, l, LANES), BF16)
    return pl.pallas_call(
        _fox_aug_kernel, grid=(b, l // tl),
        in_specs=[xspec, xspec, xspec, pl.BlockSpec((None, tl, LANES), lambda i, j: (i, j, 0)),
                  full((FOX_HEADS, 4 * LANES, LANES)), full((FOX_HEADS, 4 * LANES, LANES)),
                  full((1, LANES)), full((1, LANES))],
        out_specs=[aspec, aspec, xspec, xspec],
        out_shape=[aug, aug] + [jax.ShapeDtypeStruct((b, l, FOX_DIM), BF16)] * 2,
        compiler_params=_cparams(("parallel", "parallel")), name="fox_aug")(q, k, v, cc, wq, wk, ones_q, ones_k)


def _fox_prompt_kernel(qi_ref, ki_ref, qa_ref, ka_ref, va_ref, vb_ref, o_ref, m_ref, acc_ref):
    step = pl.program_id(2)
    qi = qi_ref[step]
    ki = ki_ref[step]
    tq = qa_ref.shape[1]
    tk = ka_ref.shape[1]
    rep = tk // LANES
    half = tq // FOX_ROW_SPLIT

    first_q = qi * tq
    first_k = ki * tk

    @pl.when(ki == 0)
    def _():
        m_ref[...] = jnp.full_like(m_ref, NEG_BIG)
        acc_ref[...] = jnp.zeros_like(acc_ref)

    def update(diagonal, first_row):
        vs = (va_ref[...], vb_ref[...])
        units = [(h, r * half) for h in range(2) for r in range(FOX_ROW_SPLIT) if r * half >= first_row]
        s = [lax.dot_general(qa_ref[h, r0:r0 + half], ka_ref[h], _NT, preferred_element_type=F32)
             for h, r0 in units]
        if diagonal:
            ahead = first_q - first_k
            s = [jnp.where(ahead + r0 + _iota((half, tk), 0) >= _iota((half, tk), 1), x, NEG_BIG)
                 for x, (h, r0) in zip(s, units)]
        m_prev = [m_ref[h, r0:r0 + half] for h, r0 in units]
        m_new = [jnp.maximum(mp, jnp.max(x, axis=1, keepdims=True)) for mp, x in zip(m_prev, s)]
        pr = [jnp.exp2(x - jnp.tile(mn, (1, rep))) for x, mn in zip(s, m_new)]
        pv = [jnp.dot(x.astype(BF16), vs[h], preferred_element_type=F32) for x, (h, r0) in zip(pr, units)]
        for (h, r0), mp, mn, y in zip(units, m_prev, m_new, pv):
            rows = slice(r0, r0 + half)
            acc_ref[h, rows] = jnp.exp2(mp - mn) * acc_ref[h, rows] + y
            m_ref[h, rows] = mn

    @pl.when(first_k + tk <= first_q)
    def _():
        update(False, 0)

    for j in range(tq // tk):
        @pl.when(first_k == first_q + j * tk)
        def _(j=j):
            update(True, (j * tk // half) * half)

    @pl.when(first_k + tk == first_q + tq)
    def _():
        first = _iota((tq, LANES), 1) < HEAD_DIM
        acc_a, acc_b = acc_ref[0], acc_ref[1]
        value = jnp.where(first, acc_a, acc_b)
        total = jnp.where(first, pltpu.roll(acc_a, HEAD_DIM, 1), pltpu.roll(acc_b, HEAD_DIM, 1))
        o_ref[...] = (value / total).astype(o_ref.dtype)


def _fox_prompt(qa, ka, va, vb, tq, tk):
    b, _, l, _ = qa.shape
    assert tq % tk == 0 and l % tq == 0
    pairs = [(qi, ki) for qi in range(l // tq) for ki in range((qi + 1) * tq // tk)]
    qi_of = jnp.array([p[0] for p in pairs], jnp.int32)
    ki_of = jnp.array([p[1] for p in pairs], jnp.int32)
    vspec = pl.BlockSpec((None, tk, LANES), lambda i, p, s, qi, ki: (i, ki[s], p))
    grid_spec = pltpu.PrefetchScalarGridSpec(
        num_scalar_prefetch=2, grid=(b, FOX_DIM // LANES, len(pairs)),
        in_specs=[pl.BlockSpec((None, 2, tq, LANES), lambda i, p, s, qi, ki: (i, p, qi[s], 0)),
                  pl.BlockSpec((None, 2, tk, LANES), lambda i, p, s, qi, ki: (i, p, ki[s], 0)),
                  vspec, vspec],
        out_specs=pl.BlockSpec((None, tq, LANES), lambda i, p, s, qi, ki: (i, qi[s], p)),
        scratch_shapes=[pltpu.VMEM((2, tq, LANES), F32), pltpu.VMEM((2, tq, LANES), F32)])
    return pl.pallas_call(
        _fox_prompt_kernel, grid_spec=grid_spec,
        out_shape=jax.ShapeDtypeStruct((b, l, FOX_DIM), BF16),
        compiler_params=_cparams(("parallel", "parallel", "arbitrary")), name="fox_prompt")(
            qi_of, ki_of, qa, ka, va, vb)


def _fox_sample_kernel(pt_ref, qbd_ref, cq_ref, kn_ref, vn_ref, cn_ref, pool_k, pool_v, pool_lf, o_ref,
                       kbuf, vbuf, lfbuf, sem, m_ref, l_ref, acc_ref, suf_ref, *, group, n_pages):
    b_id = pl.program_id(0)
    s_id = pl.program_id(1)
    nq = qbd_ref.shape[0]
    rep = nq // FOX_HEADS
    row = _iota((nq, PAGE), 0)
    col = _iota((nq, PAGE), 1)
    groups_per_seq = n_pages // group
    total_groups = pl.num_programs(0) * groups_per_seq

    def group_copies(t, slot):
        seq = t // groups_per_seq
        first = (t % groups_per_seq) * group
        out = []
        for i in range(group):
            pg = pt_ref[seq, n_pages - 1 - (first + i)]
            out.append(pltpu.make_async_copy(pool_k.at[pg], kbuf.at[slot, i], sem.at[slot]))
            out.append(pltpu.make_async_copy(pool_v.at[pg], vbuf.at[slot, i], sem.at[slot]))
            out.append(pltpu.make_async_copy(pool_lf.at[pg], lfbuf.at[slot, i], sem.at[slot]))
        return out

    def start_group(t):
        for c in group_copies(t, t % FOX_RING_SLOTS):
            c.start()

    @pl.when((b_id == 0) & (s_id == 0))
    def _():
        for t in range(FOX_RING_AHEAD):
            start_group(t)

    def attend(scores, values):
        m_prev = m_ref[...]
        m_new = m_prev
        for s in scores:
            m_new = jnp.maximum(m_new, jnp.max(s, axis=1, keepdims=True))
        corr = jnp.exp(m_prev - m_new)
        l_new = corr * l_ref[...]
        acc = corr[:, 0:1] * acc_ref[...]
        for s, pv in zip(scores, values):
            pr = jnp.exp(s - m_new)
            l_new = l_new + jnp.sum(pr, axis=1, keepdims=True)
            acc = acc + pv(pr.astype(BF16))
        l_ref[...] = l_new
        acc_ref[...] = acc
        m_ref[...] = m_new

    @pl.when(s_id == 0)
    def _():
        m_ref[...] = jnp.full_like(m_ref, NEG_BIG)
        l_ref[...] = jnp.zeros_like(l_ref)
        acc_ref[...] = jnp.zeros_like(acc_ref)
        suf_ref[...] = jnp.zeros_like(suf_ref)
        s = lax.dot_general(qbd_ref[...], kn_ref[...].astype(BF16), _NT, preferred_element_type=F32)
        s = s + (cq_ref[...] - jnp.tile(cn_ref[...], (rep, 1)))
        s = jnp.where(col <= row // FOX_HEADS, s, NEG_BIG)
        vn = vn_ref[...].astype(BF16)
        attend([s], [lambda pr: jnp.dot(pr, vn, preferred_element_type=F32)])

    @pl.when(s_id > 0)
    def _():
        t = b_id * groups_per_seq + (s_id - 1)
        slot = t % FOX_RING_SLOTS

        @pl.when(t + FOX_RING_AHEAD < total_groups)
        def _():
            start_group(t + FOX_RING_AHEAD)

        for c in group_copies(t, slot):
            c.wait()
        later = (_iota((PAGE, PAGE), 0) > _iota((PAGE, PAGE), 1)).astype(BF16)
        carry = suf_ref[...]
        qbd = qbd_ref[...]
        cq = cq_ref[...]
        scores, values = [], []
        lfs = [lfbuf[slot, i] for i in range(group)]
        local = [_mm_sel_r(lf, later) for lf in lfs]
        totals = [jnp.broadcast_to(loc[:, 0:1] + lf[:, 0:1], carry.shape) for loc, lf in zip(local, lfs)]
        for i in range(group):
            suffix = local[i] + carry
            carry = carry + totals[i]
            s = jnp.dot(qbd, kbuf[slot, i].astype(BF16), preferred_element_type=F32)
            scores.append(s + (cq + jnp.tile(suffix, (rep, 1))))
            values.append(lambda pr, i=i: lax.dot_general(pr, vbuf[slot, i].astype(BF16), _NT,
                                                          preferred_element_type=F32))
        suf_ref[...] = carry
        attend(scores, values)

    @pl.when(s_id == pl.num_programs(1) - 1)
    def _():
        lane_head = _iota((nq, FOX_DIM), 1) // HEAD_DIM
        row_head = _iota((nq, FOX_DIM), 0) % FOX_HEADS
        own = jnp.where(lane_head == row_head, acc_ref[...], 0.0)
        fold = (_iota((FOX_DIM, HEAD_DIM), 0) % HEAD_DIM == _iota((FOX_DIM, HEAD_DIM), 1)).astype(BF16)
        o_ref[...] = _mm_sel_r(own, fold) / l_ref[:, 0:HEAD_DIM]


def _fox_sample(page_table, qbd, cq, k_new, v_new, c_new, pool_k, pool_v, pool_lf):
    b, nq, _ = qbd.shape
    n_pages = page_table.shape[1]
    group = min(FOX_PAGE_GROUP, n_pages)
    assert n_pages % group == 0

    assert b * (n_pages // group) >= FOX_RING_AHEAD
    per_b = lambda shape: pl.BlockSpec((None,) + shape, lambda i, s, pt: (i, 0, 0))
    hbm = pl.BlockSpec(memory_space=pl.ANY)
    grid_spec = pltpu.PrefetchScalarGridSpec(
        num_scalar_prefetch=1, grid=(b, n_pages // group + 1),
        in_specs=[per_b((nq, FOX_DIM)), per_b((nq, LANES)), per_b((PAGE, FOX_DIM)), per_b((PAGE, FOX_DIM)),
                  per_b((FOX_HEADS, LANES)), hbm, hbm, hbm],
        out_specs=per_b((nq, HEAD_DIM)),
        scratch_shapes=[pltpu.VMEM((FOX_RING_SLOTS, group, FOX_DIM, PAGE), F32),
                        pltpu.VMEM((FOX_RING_SLOTS, group, FOX_DIM, PAGE), F32),
                        pltpu.VMEM((FOX_RING_SLOTS, group, FOX_HEADS, PAGE), F32),
                        pltpu.SemaphoreType.DMA((FOX_RING_SLOTS,)),
                        pltpu.VMEM((nq, LANES), F32), pltpu.VMEM((nq, LANES), F32),
                        pltpu.VMEM((nq, FOX_DIM), F32), pltpu.VMEM((FOX_HEADS, LANES), F32)])
    return pl.pallas_call(
        functools.partial(_fox_sample_kernel, group=group, n_pages=n_pages), grid_spec=grid_spec,
        out_shape=jax.ShapeDtypeStruct((b, nq, HEAD_DIM), F32),
        compiler_params=_cparams(("arbitrary", "arbitrary")), name="fox_sample")(
            page_table, qbd, cq, k_new, v_new, c_new, pool_k, pool_v, pool_lf)


def _ssm_kernel(z_ref, xbc_ref, fc_ref, fr_ref, buf_ref, s0_ref, cw_ref, cb_ref, dtb_c_ref, dtb_r_ref,
                a_c_ref, a_r_ref, d_ref, nw_ref, tl_ref, tu_ref, eall_ref, eexp_ref, y_ref, sout_ref, carry_ref, s_ref,
                *, valid):
    @pl.when(pl.program_id(1) == 0)
    def _():
        carry_ref[...] = buf_ref[...]
        s_ref[...] = s0_ref[...]

    raw = xbc_ref[...]
    rows = raw.shape[0]
    cs = SSD_CHUNK
    chunks = range(rows // cs)
    carry = carry_ref[...]
    cw = cw_ref[...]
    conv = (cw[0:1] * _shifted(raw, carry, 3) + cw[1:2] * _shifted(raw, carry, 2)
            + cw[2:3] * _shifted(raw, carry, 1) + cw[3:4] * raw)
    carry_ref[...] = raw[rows - SUBLANES:]
    xbc = _silu(conv + cb_ref[...])
    x = xbc[:, :SSM_DIM]

    dt_c = _softplus(fc_ref[...] + dtb_c_ref[...])
    dt_r = _softplus(fr_ref[...] + jnp.tile(dtb_r_ref[...], (1, rows // LANES)))
    if valid < cs:
        dt_c = jnp.where(_iota((rows, LANES), 0) < valid, dt_c, 0.0)
        dt_r = jnp.where(_iota((16, rows), 1) < valid, dt_r, 0.0)
    acum_c = _mm_sel_l(tl_ref[...], dt_c * a_c_ref[...])
    acum_r = _mm_sel_r(dt_r * jnp.tile(a_r_ref[...], (1, rows // LANES)), tu_ref[...])
    ends = [acum_c[(c + 1) * cs - 1:(c + 1) * cs] for c in chunks]
    acum_end = jnp.concatenate([jnp.broadcast_to(e, (cs, LANES)) for e in ends], axis=0)
    w_c = jnp.exp(acum_end - acum_c) * dt_c

    heads = range(SSM_HEADS)
    pairs = range(SSM_HEADS // 2)
    group_of = lambda p: (2 * p) // (SSM_HEADS // SSM_GROUPS)
    lane = _iota((cs, LANES), 1)
    causal = _iota((cs, cs), 0) >= _iota((cs, cs), 1)
    a_all = _mm_sel_r(acum_c, eall_ref[...])
    ea_wide = jnp.exp(_mm_sel_r(acum_c, eexp_ref[...]))
    w_wide = _mm_sel_r(w_c, eexp_ref[...])
    rs = [slice(c * cs, (c + 1) * cs) for c in chunks]
    bm = [[xbc[rs[c], SSM_DIM + g * SSM_STATE:SSM_DIM + (g + 1) * SSM_STATE] for g in range(SSM_GROUPS)]
          for c in chunks]
    cm = [[xbc[rs[c], SSM_DIM + (SSM_GROUPS + g) * SSM_STATE:SSM_DIM + (SSM_GROUPS + g + 1) * SSM_STATE]
           for g in range(SSM_GROUPS)] for c in chunks]
    cb = [[_mm(cm[c][g], bm[c][g], _NT) for g in range(SSM_GROUPS)] for c in chunks]
    xp = [[x[rs[c], p * LANES:(p + 1) * LANES] for p in pairs] for c in chunks]
    a_col = [[a_all[rs[c], h * LANES:(h + 1) * LANES] for h in heads] for c in chunks]
    gmat = [[cb[c][group_of(h // 2)]
             * jnp.exp(jnp.where(causal, a_col[c][h] - acum_r[8 + h:9 + h, rs[c]], NEG_BIG))
             * dt_r[8 + h:9 + h, rs[c]] for h in heads] for c in chunks]
    y_intra = [[_mm(gmat[c][h], xp[c][h // 2]) for h in heads] for c in chunks]
    s_upd = [[_mm(xp[c][p] * w_wide[rs[c], p * LANES:(p + 1) * LANES], bm[c][group_of(p)], _TN) for p in pairs]
             for c in chunks]
    state = [s_ref[p] for p in pairs]
    y_rows = []
    for c in chunks:
        y_inter = [_mm(cm[c][group_of(p)], state[p], _NT) for p in pairs]
        dec = [jnp.concatenate([jnp.broadcast_to(jnp.exp(a_col[c][2 * p + h][cs - 1:cs]), (HEAD_DIM, LANES))
                                for h in range(2)], axis=0) for p in pairs]
        state = [state[p] * dec[p] + s_upd[c][p] for p in pairs]
        y_rows.append(jnp.concatenate(
            [jnp.where(lane < HEAD_DIM, y_intra[c][2 * p], y_intra[c][2 * p + 1])
             + y_inter[p] * ea_wide[rs[c], p * LANES:(p + 1) * LANES] for p in pairs], axis=1))
    for p in pairs:
        s_ref[p] = state[p]

    y = (y_rows[0] if len(y_rows) == 1 else jnp.concatenate(y_rows, axis=0)) + d_ref[...] * x
    y = y * _silu(z_ref[...])
    y_ref[...] = _rmsnorm(y, nw_ref[...]).astype(y_ref.dtype)

    @pl.when(pl.program_id(1) == pl.num_programs(1) - 1)
    def _():
        sout_ref[...] = s_ref[...]


def _ssm(z, xbc, fc, fr, buf8, s0, cw8, cb, dtb_c, dtb_r, a_c, a_r, d_exp, nw, valid):
    b, l, _ = z.shape
    cs = SSD_CHUNK * (SSM_STEP_CHUNKS if l % (SSD_CHUNK * SSM_STEP_CHUNKS) == 0 else 1)
    tri_l, tri_u = _tri(cs, SSD_CHUNK)
    src = jnp.arange(LANES)[:, None]
    e_all = (src == 8 + jnp.arange(SSM_HEADS * LANES)[None, :] // LANES).astype(BF16)
    e_exp = (src == 8 + jnp.arange(SSM_DIM)[None, :] // HEAD_DIM).astype(BF16)
    full = lambda s: pl.BlockSpec(s, lambda i, j: (0,) * len(s))
    sspec = pl.BlockSpec((None, 4, LANES, LANES), lambda i, j: (i, 0, 0, 0))
    return pl.pallas_call(
        functools.partial(_ssm_kernel, valid=valid), grid=(b, l // cs),
        in_specs=[pl.BlockSpec((None, cs, SSM_DIM), lambda i, j: (i, j, 0)),
                  pl.BlockSpec((None, cs, SSM_CONV_DIM), lambda i, j: (i, j, 0)),
                  pl.BlockSpec((None, cs, LANES), lambda i, j: (i, j, 0)),
                  pl.BlockSpec((None, 16, cs), lambda i, j: (i, 0, j)),
                  pl.BlockSpec((None, SUBLANES, SSM_CONV_DIM), lambda i, j: (i, 0, 0)),
                  sspec,
                  full((SUBLANES, SSM_CONV_DIM)), full((1, SSM_CONV_DIM)), full((1, LANES)), full((16, LANES)),
                  full((1, LANES)), full((16, LANES)), full((1, SSM_DIM)), full((1, SSM_DIM)),
                  full((cs, cs)), full((cs, cs)), full((LANES, SSM_HEADS * LANES)), full((LANES, SSM_DIM))],
        out_specs=[pl.BlockSpec((None, cs, SSM_DIM), lambda i, j: (i, j, 0)), sspec],
        out_shape=[jax.ShapeDtypeStruct((b, l, SSM_DIM), BF16), jax.ShapeDtypeStruct((b, 4, LANES, LANES), F32)],
        scratch_shapes=[pltpu.VMEM((SUBLANES, SSM_CONV_DIM), F32), pltpu.VMEM((4, LANES, LANES), F32)],
        compiler_params=_cparams(("parallel", "arbitrary")), name="ssm")(
            z, xbc, fc, fr, buf8, s0, cw8, cb, dtb_c, dtb_r, a_c, a_r, d_exp, nw, tri_l, tri_u, e_all, e_exp)


def _pad_rows(a, rows):
    return jnp.pad(a, ((0, 0), (rows - a.shape[1], 0), (0, 0)))


def _tri(n, seg):
    i = jnp.arange(n)
    same = (i[:, None] // seg) == (i[None, :] // seg)
    lower = (same & (i[:, None] >= i[None, :])).astype(BF16)
    return lower, lower.T


def _lane_vec(v, offset):
    n = v.shape[0]
    col = jnp.zeros((1, LANES), F32).at[0, offset:offset + n].set(v)
    row = jnp.zeros((16, LANES), F32).at[offset:offset + n, :].set(jnp.broadcast_to(v[:, None], (n, LANES)))
    return col, row


def _layer0(h, bsz, l, sc_buf, shift_prev, wkv0, p):
    m = bsz * l
    tl = min(512, l)
    z_sc, z_rw = _in_proj(h, p['g_mix'], p['w_in'], (3 * SC_DIM, RW_SHIFT_W))
    z_sc = z_sc.reshape(bsz, l, 3 * SC_DIM)
    z_rw = z_rw.reshape(bsz, l, RW_SHIFT_W)
    ya, sc_last = _sconv(z_sc, _pad_rows(sc_buf, SUBLANES), p['sc_w8'], tl)
    r, lw, kp, v, kn, kb, g = _rwkv_pre(z_rw, _pad_rows(shift_prev[:, None], SUBLANES), p['mu'], p['w0'], p['w2p'],
                                        p['a0'], p['a2p'], p['g2'], p['k_k'], p['k_a'], p['ones_bd'], tl)
    h_in = wkv0.reshape(bsz, 4, LANES, HEAD_DIM)
    lpad = -(-l // RW_CHUNK) * RW_CHUNK
    padl = lambda t: jnp.pad(t, ((0, 0), (0, lpad - l), (0, 0)))
    o, h_fin = _rwkv_chunk(padl(r), padl(lw), padl(kp), padl(v), padl(kn), padl(kb), h_in, *_rwkv_step_shape(bsz, lpad))
    o = o[:, :l]
    wkv_new = jnp.transpose(h_fin.reshape(bsz, 4, HEAD_DIM, 2, HEAD_DIM), (0, 1, 3, 4, 2))
    wkv_new = wkv_new.reshape(bsz, RW_HEADS, HEAD_DIM, HEAD_DIM)
    flat = lambda t: t.reshape(m, t.shape[-1])
    h = _out0(h, flat(ya), flat(o), flat(r), flat(kp), flat(v), flat(g), p['ln_w'], p['ln_b'], p['r_k'],
              p['ones_bd'], p['wo_a'], p['wo_b'])
    h = _ffn(h, p['g_ffn'], p['w_gate'], p['w_up'], p['w_down'], p['g_ffn'], False)
    return h, sc_last[:, SUBLANES - 2:], z_rw[:, -1], wkv_new


def _layer1(h, bsz, l, past, conv_buf, ssm0, p, g_final):
    m = bsz * l
    shape5 = (bsz, l, FOX_HEADS, HEAD_DIM)
    if past is None:
        splits = ((FOX_DIM, BF16, LOG2E * HEAD_DIM ** -0.5), (FOX_DIM, BF16, 1.0), (FOX_DIM, BF16, 1.0),
                  SSM_DIM, SSM_CONV_DIM, LANES)
        q, k, v, zg, xbc, fdt, k_t, v_t, fr = _in_proj(h, p['g_mix'], p['w_in'], splits, p['w_tkv'],
                                                      (FOX_DIM, FOX_DIM, 16), (bsz, l))
        rows_t = lambda t: jnp.transpose(t.reshape(bsz, FOX_HEADS, HEAD_DIM, l), (0, 3, 1, 2))
        k_new, v_new = rows_t(k_t), rows_t(v_t)
    else:
        splits = (FOX_DIM, FOX_DIM, FOX_DIM, SSM_DIM, SSM_CONV_DIM, LANES)
        q, k, v, zg, xbc, fdt, fdt_t = _in_proj(h, p['g_mix'], p['w_in'], splits, p['w_t'], (16,))
        fr = jnp.swapaxes(fdt_t.reshape(16, bsz, l), 0, 1)
        k_new, v_new = k.reshape(shape5), v.reshape(shape5)
    if past is None:
        tl = min(512, l)
        tri_l, tri_u = _tri(tl, tl)
        lf, cc, cr = _logf(fdt.reshape(bsz, l, LANES), fr, p['fb_c'], p['fb_r'], tri_l, tri_u, tl)
        qa, ka, va, vb = _fox_aug(q.reshape(bsz, l, FOX_DIM), k.reshape(bsz, l, FOX_DIM), v.reshape(bsz, l, FOX_DIM),
                              cc, tl)
        yf = _fox_prompt(qa, ka, va, vb, min(FOX_TQ, l), tl)
        logf_new = lf[..., :FOX_HEADS]
    else:
        page_table, pool_k, pool_v, pool_lf = past
        tri_l, tri_u = _tri(m, l)
        fr_flat = fdt_t.reshape(1, 16, m)
        lf, cc, cr = _logf(fdt.reshape(1, m, LANES), fr_flat, p['fb_c'], p['fb_r'], tri_l, tri_u, m)
        logf_new = lf.reshape(bsz, l, LANES)[..., :FOX_HEADS]
        q4 = q.reshape(bsz, l, FOX_HEADS, HEAD_DIM) * (HEAD_DIM ** -0.5)
        qbd = jnp.einsum('bqhd,hg->bqhgd', q4, jnp.eye(FOX_HEADS, dtype=F32))
        qbd = qbd.reshape(bsz, l * FOX_HEADS, FOX_DIM).astype(BF16)
        c_q = cc.reshape(bsz, l, LANES)[..., :FOX_HEADS].reshape(bsz, l * FOX_HEADS, 1)
        c_q = jnp.broadcast_to(c_q, (bsz, l * FOX_HEADS, LANES))
        c_new = jnp.pad(jnp.swapaxes(cr.reshape(16, bsz, l), 0, 1)[:, :FOX_HEADS], ((0, 0), (0, 0), (0, PAGE - l)))
        pad_new = lambda t: jnp.pad(t.reshape(bsz, l, FOX_DIM), ((0, 0), (0, PAGE - l), (0, 0)))
        yf = _fox_sample(page_table, qbd, c_q, pad_new(k), pad_new(v), c_new, pool_k, pool_v, pool_lf)
        yf = yf.reshape(bsz, l, FOX_DIM)
    xbc3 = xbc.reshape(bsz, l, SSM_CONV_DIM)
    if l % SSD_CHUNK == 0:
        lpad, valid = l, SSD_CHUNK
    else:
        assert l < SSD_CHUNK
        lpad, valid = SSD_CHUNK, l
    padl = lambda t: jnp.pad(t, ((0, 0), (0, lpad - l), (0, 0)))
    ym, s_fin = _ssm(padl(zg.reshape(bsz, l, SSM_DIM)), padl(xbc3), padl(fdt.reshape(bsz, l, LANES)),
                     jnp.pad(fr, ((0, 0), (0, 0), (0, lpad - l))),
                     _pad_rows(conv_buf, SUBLANES), ssm0.reshape(bsz, 4, LANES, LANES), p['cw8'], p['cb'],
                     p['dtb_c'], p['dtb_r'], p['a_c'], p['a_r'], p['d_exp'], p['nw'], valid)
    ym = ym[:, :l]
    h = _out1(h, yf.reshape(m, FOX_DIM), ym.reshape(m, SSM_DIM), p['wo_a'], p['wo_b'])
    h = _ffn(h, p['g_ffn'], p['w_gate'], p['w_up'], p['w_down'], g_final, True)
    return (h, k_new, v_new, logf_new, xbc3[:, l - 3:], s_fin.reshape(bsz, SSM_HEADS, HEAD_DIM, SSM_STATE))


def kernel(x_prompt, x_sample, state_sc, state_shift, state_wkv, cache_k, cache_v, cache_logf, state_ssm_conv, state_ssm, page_table, norm_mix, norm_ffn, norm_final, w_in0, sc_conv_w, rw_mu, rw_w0, rw_w2, rw_a0, rw_a2, rw_g2, rw_k_k, rw_k_a, rw_r_k, rw_ln_w, rw_ln_b, w_out0, w_in1, fox_f_bias, ssm_conv_w, ssm_conv_b, ssm_dt_bias, ssm_a_log, ssm_d, ssm_norm_w, w_out1, w_gate, w_up, w_down):
    bp, lp, _ = x_prompt.shape
    bs, ls, _ = x_sample.shape
    row = lambda v: v.reshape(1, -1).astype(F32)
    head_ids = jnp.arange(RW_DIM) // HEAD_DIM
    ones_bd = (head_ids[:, None] == head_ids[None, :]).astype(BF16)
    zpad = jnp.zeros((64, RW_DIM), F32)

    p0 = dict(
        g_mix=row(norm_mix[0]), g_ffn=row(norm_ffn[0]), w_in=w_in0[0].astype(BF16),
        sc_w8=jnp.pad(sc_conv_w[0], ((0, SUBLANES - sc_conv_w.shape[1]), (0, 0))),
        mu=row(rw_mu[0]), w0=row(rw_w0[0]), a0=row(rw_a0[0]),
        w2p=jnp.concatenate([rw_w2[0], zpad], axis=0).astype(BF16),
        a2p=jnp.concatenate([zpad, rw_a2[0]], axis=0).astype(BF16),
        g2=rw_g2[0].astype(BF16), k_k=row(rw_k_k[0]), k_a=row(rw_k_a[0]), r_k=row(rw_r_k[0]),
        ln_w=row(rw_ln_w[0]), ln_b=row(rw_ln_b[0]), ones_bd=ones_bd,
        wo_a=w_out0[0, :SC_DIM].astype(BF16), wo_b=w_out0[0, SC_DIM:].astype(BF16),
        w_gate=w_gate[0].astype(BF16), w_up=w_up[0].astype(BF16), w_down=w_down[0].astype(BF16))

    w1 = w_in1[0]
    c0 = 3 * FOX_DIM
    f_cols = w1[:, c0:c0 + FOX_HEADS]
    z_cols = w1[:, c0 + FOX_HEADS:c0 + FOX_HEADS + SSM_DIM]
    xbc_cols = w1[:, c0 + FOX_HEADS + SSM_DIM:c0 + FOX_HEADS + SSM_DIM + SSM_CONV_DIM]
    dt_cols = w1[:, c0 + FOX_HEADS + SSM_DIM + SSM_CONV_DIM:]
    fdt_cols = jnp.concatenate([f_cols, dt_cols, jnp.zeros((D_MODEL, LANES - 16), F32)], axis=1)
    w1r = jnp.concatenate([w1[:, :c0], z_cols, xbc_cols, fdt_cols], axis=1).astype(BF16)
    fb_c, fb_r = _lane_vec(fox_f_bias[0], 0)
    dtb_c, dtb_r = _lane_vec(ssm_dt_bias[0], 8)
    a_c, a_r = _lane_vec(-jnp.exp(ssm_a_log[0].astype(F32)), 8)
    p1 = dict(
        g_mix=row(norm_mix[1]), g_ffn=row(norm_ffn[1]), w_in=w1r, w_t=fdt_cols[:, :16].T.astype(BF16),
        w_tkv=jnp.concatenate([w1[:, FOX_DIM:3 * FOX_DIM], fdt_cols[:, :16]], axis=1).T.astype(BF16),
        fb_c=fb_c, fb_r=fb_r, dtb_c=dtb_c, dtb_r=dtb_r, a_c=a_c, a_r=a_r,
        cw8=jnp.pad(ssm_conv_w[0], ((0, SUBLANES - ssm_conv_w.shape[1]), (0, 0))), cb=row(ssm_conv_b[0]),
        d_exp=row(jnp.repeat(ssm_d[0], HEAD_DIM)), nw=row(ssm_norm_w[0]),
        wo_a=w_out1[0, :FOX_DIM].astype(BF16), wo_b=w_out1[0, FOX_DIM:].astype(BF16),
        w_gate=w_gate[1].astype(BF16), w_up=w_up[1].astype(BF16), w_down=w_down[1].astype(BF16))
    g_final = row(norm_final)

    n_pool = cache_k.shape[1]
    page_t = lambda c: jnp.transpose(c[0], (0, 2, 3, 1)).reshape(n_pool, FOX_DIM, PAGE)
    past = (page_table, page_t(cache_k), page_t(cache_v), jnp.swapaxes(cache_logf[0], 1, 2))

    hp = x_prompt.reshape(bp * lp, D_MODEL)
    hs = x_sample.reshape(bs * ls, D_MODEL)
    hp, sc_p, sh_p, wkv_p = _layer0(hp, bp, lp, jnp.zeros((bp, 2, SC_DIM), F32), jnp.zeros((bp, RW_SHIFT_W), F32),
                                    jnp.zeros((bp, RW_HEADS, HEAD_DIM, HEAD_DIM), F32), p0)
    hs, sc_s, sh_s, wkv_s = _layer0(hs, bs, ls, state_sc[0], state_shift[0], state_wkv[0], p0)
    yp, k_p, v_p, lf_p, cv_p, ssm_p = _layer1(hp, bp, lp, None, jnp.zeros((bp, 3, SSM_CONV_DIM), F32),
                                              jnp.zeros((bp, SSM_HEADS, HEAD_DIM, SSM_STATE), F32), p1, g_final)
    ys, k_s, v_s, lf_s, cv_s, ssm_s = _layer1(hs, bs, ls, past, state_ssm_conv[0], state_ssm[0], p1, g_final)
    one = lambda t: t[None]
    return (yp.reshape(bp, lp, D_MODEL), ys.reshape(bs, ls, D_MODEL),
            one(sc_p), one(sc_s), one(sh_p), one(sh_s), one(wkv_p), one(wkv_s),
            one(k_p), one(k_s), one(v_p), one(v_s), one(lf_p), one(lf_s), one(cv_p), one(cv_s),
            one(ssm_p), one(ssm_s))
```
